```python
import math
import jax, jax.numpy as jnp
from jax import lax
import numpy as np

D_MODEL = 1024
BATCH = 32
SEQ = 256
DEPTH = 2
DEC_BATCH = 2
DEC_SEQ = 4096
PAST_LEN = 256

GRID_W = 64
A_HEADS = 8
A_KV_HEADS = 2
A_HEAD_DIM = 64
B_WIDTH = D_MODEL // 4
CONV_WIDTH = 3
C_HEADS = 4
C_QK_DIM = 32
C_V_DIM = 64
MIX_WIDTH = A_HEADS * A_HEAD_DIM + B_WIDTH + C_HEADS * C_V_DIM
W_IN_WIDTH = (A_HEADS + 2 * A_KV_HEADS) * A_HEAD_DIM + 3 * B_WIDTH + C_HEADS * (4 * C_QK_DIM + C_V_DIM)
N_EXPERTS = 16
N_EXPERT_GROUPS = 4
EXPERTS_PER_GROUP = N_EXPERTS // N_EXPERT_GROUPS
TOP_K = 2
EXPERT_FF = 512
ROPE_THETA = 10000.0
EPS = 1e-6
Q_BLOCK = 128
N_MOD = 6

kernel_name = "hybrid_prefix_diffusion_step"


def rms_norm(x, g):
    xf = x.astype(jnp.float32)
    y = xf * lax.rsqrt(jnp.mean(xf * xf, axis=-1, keepdims=True) + EPS)
    return (y * g.astype(jnp.float32)).astype(x.dtype)


def axial_rope_angles(n_tokens, rot_dim):
    rows = n_tokens // GRID_W
    r, col = jnp.meshgrid(jnp.arange(rows, dtype=jnp.float32),
                          jnp.arange(GRID_W, dtype=jnp.float32), indexing="ij")
    n_freq = rot_dim // 4
    inv_freq = ROPE_THETA ** (-jnp.arange(n_freq, dtype=jnp.float32) / n_freq)
    ang = jnp.concatenate([r.reshape(-1, 1) * inv_freq, col.reshape(-1, 1) * inv_freq], axis=-1)
    return jnp.cos(ang), jnp.sin(ang)


def apply_rope(x, cos, sin):
    d = x.shape[-1]
    xf = x.astype(jnp.float32).reshape(x.shape[:-1] + (d // 2, 2))
    bshape = (x.shape[1],) + (1,) * (x.ndim - 3) + (d // 2,)
    cs, sn = cos.reshape(bshape), sin.reshape(bshape)
    xe, xo = xf[..., 0], xf[..., 1]
    out = jnp.stack([xe * cs - xo * sn, xe * sn + xo * cs], axis=-1)
    return out.reshape(x.shape).astype(x.dtype)


def sweep_query_blocks(attend, q):
    b, t = q.shape[0], q.shape[1]
    nb = t // Q_BLOCK
    qb = jnp.moveaxis(q.reshape((b, nb, Q_BLOCK) + q.shape[2:]), 1, 0)
    ob = lax.map(attend, qb)
    return jnp.moveaxis(ob, 0, 1).reshape((b, t) + ob.shape[3:])


def gqa_attend(qb, k, v):
    b, nq, h, d = qb.shape
    kv = k.shape[2]
    qg = qb.reshape(b, nq, kv, h // kv, d)
    s = jnp.einsum("bqkgd,bskd->bkgqs", qg, k, preferred_element_type=jnp.float32) * (d ** -0.5)
    p = jax.nn.softmax(s, axis=-1).astype(v.dtype)
    o = jnp.einsum("bkgqs,bskd->bqkgd", p, v)
    return o.reshape(b, nq, h, d)


def diff_attend(qb, k, v, lam):
    dh = qb.shape[-1]
    s = jnp.einsum("bqhcd,bshcd->bhcqs", qb, k, preferred_element_type=jnp.float32) * (dh ** -0.5)
    p = jax.nn.softmax(s, axis=-1)
    a = p[:, :, 0] - lam * p[:, :, 1]
    return jnp.einsum("bhqs,bshd->bqhd", a.astype(v.dtype), v)


def short_conv(u, w):
    up = jnp.pad(u, ((0, 0), (1, 1), (0, 0)))
    return up[:, :-2] * w[0] + up[:, 1:-1] * w[1] + up[:, 2:] * w[2]


def token_mixers(h, lp, ctx_cache):
    b, t, _ = h.shape
    proj = h @ lp["w_in"]
    sizes = [A_HEADS * A_HEAD_DIM, A_KV_HEADS * A_HEAD_DIM, A_KV_HEADS * A_HEAD_DIM,
             B_WIDTH, B_WIDTH, B_WIDTH, C_HEADS * 2 * C_QK_DIM, C_HEADS * 2 * C_QK_DIM]
    splits = np.cumsum(sizes).tolist()
    aq, ak, av, gate_b, gate_c, u, cq, ck, cv = jnp.split(proj, splits, axis=-1)
    aq = rms_norm(aq.reshape(b, t, A_HEADS, A_HEAD_DIM), lp["q_g"])
    ak = rms_norm(ak.reshape(b, t, A_KV_HEADS, A_HEAD_DIM), lp["k_g"])
    av = av.reshape(b, t, A_KV_HEADS, A_HEAD_DIM)
    cq = cq.reshape(b, t, C_HEADS, 2, C_QK_DIM)
    ck = ck.reshape(b, t, C_HEADS, 2, C_QK_DIM)
    cv = cv.reshape(b, t, C_HEADS, C_V_DIM)
    lv = lp["diff_lambda"].astype(jnp.float32)
    lam = jnp.exp(jnp.sum(lv[0] * lv[1])) - jnp.exp(jnp.sum(lv[2] * lv[3])) + lp["lam_init"]
    if ctx_cache is None:
        ka, va, kc, vc = ak, av, ck, cv
        ctx_state = (ak, av, ck.reshape(b, t, C_HEADS, 2 * C_QK_DIM), cv)
    else:
        cos_a, sin_a = axial_rope_angles(t, A_HEAD_DIM)
        cos_c, sin_c = axial_rope_angles(t, C_QK_DIM)
        aq, ak = apply_rope(aq, cos_a, sin_a), apply_rope(ak, cos_a, sin_a)
        cq, ck = apply_rope(cq, cos_c, sin_c), apply_rope(ck, cos_c, sin_c)
        pk_a, pv_a, pk_c, pv_c = ctx_cache
        ka = jnp.concatenate([pk_a, ak], axis=1)
        va = jnp.concatenate([pv_a, av], axis=1)
        kc = jnp.concatenate([pk_c.reshape(pk_c.shape[:3] + (2, C_QK_DIM)), ck], axis=1)
        vc = jnp.concatenate([pv_c, cv], axis=1)
        ctx_state = None
    oa = sweep_query_blocks(lambda qb: gqa_attend(qb, ka, va), aq).reshape(b, t, -1)
    ob = gate_b * short_conv(gate_c * u, lp["conv_w"])
    oc = sweep_query_blocks(lambda qb: diff_attend(qb, kc, vc, lam), cq)
    oc = rms_norm(oc, lp["diff_subln_g"]) * (1.0 - lp["lam_init"])
    out = jnp.concatenate([oa, ob, oc.reshape(b, t, -1)], axis=-1) @ lp["w_out"]
    return out, ctx_state


def grouped_moe(h, router_w, router_bias, w1, w3, w2):
    b, t, d = h.shape
    x = h.reshape(-1, d)
    scores = jax.nn.sigmoid(jnp.dot(x, router_w, preferred_element_type=jnp.float32))
    sel = scores + router_bias.astype(jnp.float32)
    grp_score = lax.top_k(sel.reshape(-1, N_EXPERT_GROUPS, EXPERTS_PER_GROUP), TOP_K)[0].sum(-1)
    top_grp = jnp.argmax(grp_score, axis=-1)
    in_grp = (jnp.arange(N_EXPERTS) // EXPERTS_PER_GROUP)[None, :] == top_grp[:, None]
    _, idx = lax.top_k(jnp.where(in_grp, sel, -jnp.inf), TOP_K)
    wts = jnp.take_along_axis(scores, idx, axis=-1)
    wts = wts / jnp.sum(wts, axis=-1, keepdims=True)
    gates = jnp.sum(jax.nn.one_hot(idx, N_EXPERTS, dtype=jnp.float32) * wts[..., None], axis=1)
    hid = jax.nn.silu(jnp.einsum("nd,edf->nef", x, w1)) * jnp.einsum("nd,edf->nef", x, w3)
    y = jnp.einsum("nef,efd->nd", hid * gates.astype(x.dtype)[:, :, None], w2)
    return y.reshape(b, t, d)


def trunk_layer(x, mod, lp, router_w, router_bias, ctx_cache):
    shift1, scale1, gate1, shift2, scale2, gate2 = jnp.split(mod, N_MOD, axis=-1)
    h = rms_norm(x, lp["norm1_g"]) * (1.0 + scale1) + shift1
    mix, ctx_state = token_mixers(h, lp, ctx_cache)
    x = x + gate1 * mix
    h = rms_norm(x, lp["norm2_g"]) * (1.0 + scale2) + shift2
    x = x + gate2 * grouped_moe(h, router_w, router_bias, lp["moe_w1"], lp["moe_w3"], lp["moe_w2"])
    return x, ctx_state


def setup_inputs(seed: int = 0) -> dict:
    key = jax.random.key(seed)
    ks = jax.random.split(key, 26)
    nrm = lambda k, shape, s: jax.random.normal(k, shape, jnp.float32) * s
    return {
        "x_prompt": nrm(ks[0], (BATCH, SEQ, D_MODEL), 1.0),
        "x_sample": nrm(ks[1], (DEC_BATCH, DEC_SEQ, D_MODEL), 1.0),
        "cache_gqa_k": nrm(ks[2], (DEC_BATCH, DEPTH, PAST_LEN, A_KV_HEADS, A_HEAD_DIM), 1.0),
        "cache_gqa_v": nrm(ks[3], (DEC_BATCH, DEPTH, PAST_LEN, A_KV_HEADS, A_HEAD_DIM), 1.0),
        "cache_diff_k": nrm(ks[4], (DEC_BATCH, DEPTH, PAST_LEN, C_HEADS, 2 * C_QK_DIM), 1.0),
        "cache_diff_v": nrm(ks[5], (DEC_BATCH, DEPTH, PAST_LEN, C_HEADS, C_V_DIM), 1.0),
        "c": nrm(ks[6], (DEC_BATCH, D_MODEL), 1.0),
        "c_ctx": nrm(ks[7], (D_MODEL,), 1.0),
        "w_in": nrm(ks[8], (DEPTH, D_MODEL, W_IN_WIDTH), D_MODEL ** -0.5),
        "w_out": nrm(ks[9], (DEPTH, MIX_WIDTH, D_MODEL), MIX_WIDTH ** -0.5),
        "norm1_g": 1.0 + nrm(ks[10], (DEPTH, D_MODEL), 0.1),
        "norm2_g": 1.0 + nrm(ks[11], (DEPTH, D_MODEL), 0.1),
        "ada_w": nrm(ks[12], (DEPTH, D_MODEL, N_MOD * D_MODEL), 0.5 * D_MODEL ** -0.5),
        "ada_b": nrm(ks[13], (DEPTH, N_MOD * D_MODEL), 0.01),
        "gqa_qnorm_g": 1.0 + nrm(ks[14], (DEPTH, A_HEAD_DIM), 0.1),
        "gqa_knorm_g": 1.0 + nrm(ks[15], (DEPTH, A_HEAD_DIM), 0.1),
        "conv_w": nrm(ks[16], (DEPTH, CONV_WIDTH, B_WIDTH), CONV_WIDTH ** -0.5),
        "diff_lambda": nrm(ks[17], (DEPTH, 4, C_QK_DIM), 0.1),
        "diff_subln_g": 1.0 + nrm(ks[18], (DEPTH, C_V_DIM), 0.1),
        "router_w": nrm(ks[19], (D_MODEL, N_EXPERTS), D_MODEL ** -0.5),
        "router_bias": nrm(ks[20], (N_EXPERTS,), 0.01),
        "moe_w1": nrm(ks[21], (DEPTH, N_EXPERTS, D_MODEL, EXPERT_FF), D_MODEL ** -0.5),
        "moe_w3": nrm(ks[22], (DEPTH, N_EXPERTS, D_MODEL, EXPERT_FF), D_MODEL ** -0.5),
        "moe_w2": nrm(ks[23], (DEPTH, N_EXPERTS, EXPERT_FF, D_MODEL), EXPERT_FF ** -0.5),
        "final_g": 1.0 + nrm(ks[24], (D_MODEL,), 0.1),
    }


def reference(x_prompt, x_sample, cache_gqa_k, cache_gqa_v, cache_diff_k, cache_diff_v, c,
              c_ctx, w_in, w_out, norm1_g, norm2_g, ada_w, ada_b, gqa_qnorm_g, gqa_knorm_g,
              conv_w, diff_lambda, diff_subln_g, router_w, router_bias, moe_w1, moe_w3, moe_w2,
              final_g):
    xp, xs = x_prompt, x_sample
    silu_ctx = jax.nn.silu(c_ctx)
    silu_c = jax.nn.silu(c)
    new_ak, new_av, new_ck, new_cv = [], [], [], []
    for l in range(DEPTH):
        lp = {
            "w_in": w_in[l], "w_out": w_out[l], "norm1_g": norm1_g[l], "norm2_g": norm2_g[l],
            "q_g": gqa_qnorm_g[l], "k_g": gqa_knorm_g[l], "conv_w": conv_w[l],
            "diff_lambda": diff_lambda[l], "diff_subln_g": diff_subln_g[l],
            "lam_init": 0.8 - 0.6 * math.exp(-0.3 * l),
            "moe_w1": moe_w1[l], "moe_w3": moe_w3[l], "moe_w2": moe_w2[l],
        }
        mod_p = (silu_ctx @ ada_w[l] + ada_b[l]).reshape(1, 1, -1)
        mod_s = (silu_c @ ada_w[l] + ada_b[l])[:, None, :]
        xp, (ak, av, ck, cv) = trunk_layer(xp, mod_p, lp, router_w, router_bias, None)
        new_ak.append(ak)
        new_av.append(av)
        new_ck.append(ck)
        new_cv.append(cv)
        ctx = (cache_gqa_k[:, l], cache_gqa_v[:, l], cache_diff_k[:, l], cache_diff_v[:, l])
        xs, _ = trunk_layer(xs, mod_s, lp, router_w, router_bias, ctx)
    y_prompt = rms_norm(xp, final_g)
    y_sample = rms_norm(xs, final_g)
    new_gqa_k = jnp.stack(new_ak, axis=1)
    new_gqa_v = jnp.stack(new_av, axis=1)
    new_diff_k = jnp.stack(new_ck, axis=1)
    new_diff_v = jnp.stack(new_cv, axis=1)
    return (y_prompt, y_sample, new_gqa_k, new_gqa_v, new_diff_k, new_diff_v)
```

```python
import functools
import math

import numpy as np
import jax
import jax.numpy as jnp
from jax import lax
from jax.experimental import pallas as pl
from jax.experimental.pallas import tpu as pltpu

D_MODEL = 1024
DEPTH = 2
GRID_W = 64
A_HEADS = 8
A_KV_HEADS = 2
A_HEAD_DIM = 64
B_WIDTH = 256
C_HEADS = 4
C_QK_DIM = 32
C_V_DIM = 64
N_EXPERTS = 16
EXPERTS_PER_GROUP = 4
N_EXPERT_GROUPS = 4
EXPERT_FF = 512
ROPE_THETA = 10000.0
EPS = 1e-6
N_MOD = 6
W_IN_WIDTH = 2304

LANES = 128
VMEM_LIMIT_BYTES = 56 * 1024 * 1024

F32 = jnp.float32
BF16 = jnp.bfloat16

_dot = functools.partial(jnp.dot, preferred_element_type=F32)


def _dot_nt(a, b):
    return lax.dot_general(a, b, (((1,), (1,)), ((), ())), preferred_element_type=F32)


def _split(a):
    hi = a.astype(BF16)
    lo = (a - hi.astype(F32)).astype(BF16)
    return hi, lo


def _dot3(a, b):
    ah, al = _split(a)
    bh, bl = _split(b)
    return _dot(ah, bh) + _dot(ah, bl) + _dot(al, bh)


def _sigmoid(x):
    return 1.0 / (1.0 + jnp.exp(-x))


def _seg_rms(x, bd, g):
    hi, lo = _split(x * x)
    ss = _dot(hi, bd) + _dot(lo, bd)
    return x * lax.rsqrt(ss * (1.0 / 64.0) + EPS) * g


def _rope(x, cos, sin_signed):
    lane = lax.broadcasted_iota(jnp.int32, x.shape, 1)
    even = (lane & 1) == 0
    partner = jnp.where(even, pltpu.roll(x, LANES - 1, 1), pltpu.roll(x, 1, 1))
    return x * cos + partner * sin_signed


def _dup_halves(a):
    lane = lax.broadcasted_iota(jnp.int32, a.shape, 1)
    lo = lane < 64
    r = pltpu.roll(a, 64, 1)
    return jnp.where(lo, a, r), jnp.where(lo, r, a)


def _params():
    return pltpu.CompilerParams(vmem_limit_bytes=VMEM_LIMIT_BYTES)


def _adaln_kernel(c_ref, w_ref, b_ref, o_ref):
    c = c_ref[...]
    s = c * _sigmoid(c)
    o_ref[0] = _dot3(s, w_ref[0]) + b_ref[0]


def _adaln(cvec, ada_w, ada_b):
    tn = 1536
    nt = (N_MOD * D_MODEL) // tn
    return pl.pallas_call(
        _adaln_kernel,
        grid=(DEPTH, nt),
        in_specs=[
            pl.BlockSpec((8, D_MODEL), lambda l, n: (0, 0)),
            pl.BlockSpec((1, D_MODEL, tn), lambda l, n: (l, 0, n)),
            pl.BlockSpec((1, 1, tn), lambda l, n: (l, 0, n)),
        ],
        out_specs=pl.BlockSpec((1, 8, tn), lambda l, n: (l, 0, n)),
        out_shape=jax.ShapeDtypeStruct((DEPTH, 8, N_MOD * D_MODEL), F32),
        compiler_params=_params(),
        name="adaln",
    )(cvec, ada_w, ada_b.reshape(DEPTH, 1, N_MOD * D_MODEL))


def _inproj_kernel(*refs, rope, ctx):
    it = iter(refs)
    x_ref, mod_ref, g1_ref, w_ref, qg_ref, kg_ref, bd_ref = (next(it) for _ in range(7))
    if rope:
        cos_a, sin_a, cos_c, sin_c = (next(it) for _ in range(4))
    q_ref, kd_ref, vd_ref, gb_ref, gu_ref, cq_ref, ck_ref, cv_ref = (next(it) for _ in range(8))
    if ctx:
        k32_ref, v32_ref, ck32_ref, cv32_ref = (next(it) for _ in range(4))

    x = x_ref[...]
    ms = jnp.mean(x * x, axis=-1, keepdims=True)
    y = x * lax.rsqrt(ms + EPS) * g1_ref[...]
    h = y * (1.0 + mod_ref[0, 0, 1:2, :]) + mod_ref[0, 0, 0:1, :]
    hb = h.astype(BF16)
    bd = bd_ref[...]

    def seg(lo, width):
        return _dot(hb, w_ref[:, lo:lo + width])

    for j in range(4):
        q = _seg_rms(seg(LANES * j, LANES), bd, qg_ref[...])
        if rope:
            q = _rope(q, cos_a[...], sin_a[...])
        q_ref[j] = (q * (A_HEAD_DIM ** -0.5)).astype(BF16)

    k = _seg_rms(seg(512, LANES), bd, kg_ref[...])
    if rope:
        k = _rope(k, cos_a[...], sin_a[...])
    v = seg(640, LANES)
    if ctx:
        k32_ref[...] = k
        v32_ref[...] = v
    k0, k1 = _dup_halves(k)
    kd_ref[0] = k0.astype(BF16)
    kd_ref[1] = k1.astype(BF16)
    v0, v1 = _dup_halves(v)
    vd_ref[0] = v0.astype(BF16)
    vd_ref[1] = v1.astype(BF16)

    gb_ref[...] = seg(768, B_WIDTH).astype(BF16)
    gu_ref[...] = (seg(1024, B_WIDTH) * seg(1280, B_WIDTH)).astype(BF16)

    for j in range(2):
        cq = seg(1536 + LANES * j, LANES)
        ck = seg(1792 + LANES * j, LANES)
        cv = seg(2048 + LANES * j, LANES)
        if ctx:
            ck32_ref[:, LANES * j:LANES * (j + 1)] = ck
            cv32_ref[:, LANES * j:LANES * (j + 1)] = cv
        if rope:
            cq = _rope(cq, cos_c[...], sin_c[...])
            ck = _rope(ck, cos_c[...], sin_c[...])
        cq_ref[j] = (cq * (C_QK_DIM ** -0.5)).astype(BF16)
        ck_ref[j] = ck.astype(BF16)
        cv_ref[j] = cv.astype(BF16)


def _inproj(x, mod, layer, mod_row, g1, w_in, qg, kg, bd, tables, *, tm, ctx):
    n = x.shape[0]
    rope = tables is not None
    row = lambda i: (i, 0)
    const2 = lambda i: (0, 0)
    pair3 = lambda i: (0, i, 0)
    in_specs = [
        pl.BlockSpec((tm, D_MODEL), row),
        pl.BlockSpec((1, 1, N_MOD, D_MODEL), lambda i: (layer, mod_row(i), 0, 0)),
        pl.BlockSpec((1, D_MODEL), const2),
        pl.BlockSpec((D_MODEL, W_IN_WIDTH), const2),
        pl.BlockSpec((1, LANES), const2),
        pl.BlockSpec((1, LANES), const2),
        pl.BlockSpec((LANES, LANES), const2),
    ]
    args = [x, mod, g1, w_in, qg, kg, bd]
    if rope:
        t = tables[0].shape[0]
        tab = lambda i: (i % (t // tm), 0)
        in_specs += [pl.BlockSpec((tm, LANES), tab)] * 4
        args += list(tables)
    out_specs = [
        pl.BlockSpec((4, tm, LANES), pair3),
        pl.BlockSpec((2, tm, LANES), pair3),
        pl.BlockSpec((2, tm, LANES), pair3),
        pl.BlockSpec((tm, B_WIDTH), row),
        pl.BlockSpec((tm, B_WIDTH), row),
        pl.BlockSpec((2, tm, LANES), pair3),
        pl.BlockSpec((2, tm, LANES), pair3),
        pl.BlockSpec((2, tm, LANES), pair3),
    ]
    out_shape = [
        jax.ShapeDtypeStruct((4, n, LANES), BF16),
        jax.ShapeDtypeStruct((2, n, LANES), BF16),
        jax.ShapeDtypeStruct((2, n, LANES), BF16),
        jax.ShapeDtypeStruct((n, B_WIDTH), BF16),
        jax.ShapeDtypeStruct((n, B_WIDTH), BF16),
        jax.ShapeDtypeStruct((2, n, LANES), BF16),
        jax.ShapeDtypeStruct((2, n, LANES), BF16),
        jax.ShapeDtypeStruct((2, n, LANES), BF16),
    ]
    if ctx:
        out_specs += [
            pl.BlockSpec((tm, LANES), row),
            pl.BlockSpec((tm, LANES), row),
            pl.BlockSpec((tm, 2 * LANES), row),
            pl.BlockSpec((tm, 2 * LANES), row),
        ]
        out_shape += [
            jax.ShapeDtypeStruct((n, LANES), F32),
            jax.ShapeDtypeStruct((n, LANES), F32),
            jax.ShapeDtypeStruct((n, 2 * LANES), F32),
            jax.ShapeDtypeStruct((n, 2 * LANES), F32),
        ]
    return pl.pallas_call(
        functools.partial(_inproj_kernel, rope=rope, ctx=ctx),
        grid=(n // tm,),
        in_specs=in_specs,
        out_specs=out_specs,
        out_shape=out_shape,
        compiler_params=_params(),
        name="inproj_ctx" if ctx else "inproj_lat",
    )(*args)


def _softmax_tile(qm, k, v, m, l, acc):
    s = _dot_nt(qm, k)
    m_new = jnp.maximum(m, jnp.max(s, axis=-1, keepdims=True))
    alpha = jnp.exp(m - m_new)
    p = jnp.exp(s - m_new)
    l_new = alpha * l + jnp.sum(p, axis=-1, keepdims=True)
    acc_new = alpha * acc + _dot(p.astype(BF16), v)
    return m_new, l_new, acc_new


def _attend(q_masks, k_ref, v_ref, cache_kv, tk):
    tq = q_masks[0].shape[0]
    nq = len(q_masks)
    state = []
    for qm in q_masks:
        m = jnp.full((tq, 1), -jnp.inf, F32)
        l = jnp.zeros((tq, 1), F32)
        acc = jnp.zeros((tq, LANES), F32)
        if cache_kv is not None:
            m, l, acc = _softmax_tile(qm, cache_kv[0], cache_kv[1], m, l, acc)
        state.append((m, l, acc))
    n_tiles = k_ref.shape[1] // tk

    def step(t, carry):
        start = pl.multiple_of(t * tk, tk)
        k = k_ref[0, pl.ds(start, tk), :]
        v = v_ref[0, pl.ds(start, tk), :]
        return tuple(_softmax_tile(q_masks[i], k, v, *carry[i]) for i in range(nq))

    if n_tiles == 1:
        state = step(0, tuple(state))
    else:
        state = lax.fori_loop(0, n_tiles, step, tuple(state))
    return [(acc, l) for (_, l, acc) in state]


def _gqa_kernel(*refs, has_cache, tk):
    if has_cache:
        q_ref, k_ref, v_ref, ck_ref, cv_ref, o_ref = refs
    else:
        q_ref, k_ref, v_ref, o_ref = refs
    q = q_ref[0]
    lane = lax.broadcasted_iota(jnp.int32, q.shape, 1)
    lo = lane < 64
    zero = jnp.zeros_like(q)
    q_masks = [jnp.where(lo, q, zero), jnp.where(lo, zero, q)]
    cache_kv = None
    if has_cache:
        first = (pl.program_id(2) // 2) == 0
        ka, kb = _dup_halves(ck_ref[0])
        va, vb = _dup_halves(cv_ref[0])
        cache_kv = (jnp.where(first, ka, kb).astype(BF16), jnp.where(first, va, vb).astype(BF16))
    (a0, l0), (a1, l1) = _attend(q_masks, k_ref, v_ref, cache_kv, tk)
    lane_o = lax.broadcasted_iota(jnp.int32, a0.shape, 1)
    o = jnp.where(lane_o < 64, a0 / l0, a1 / l1)
    o_ref[...] = o.astype(BF16)


def _gqa_attention(q, kd, vd, cache_k, cache_v, layer, *, n_batch, t_seq, tq, tk):
    n = q.shape[1]
    nq = t_seq // tq
    has_cache = cache_k is not None
    in_specs = [
        pl.BlockSpec((1, tq, LANES), lambda b, i, j: (j, b * nq + i, 0)),
        pl.BlockSpec((1, t_seq, LANES), lambda b, i, j: (j // 2, b, 0)),
        pl.BlockSpec((1, t_seq, LANES), lambda b, i, j: (j // 2, b, 0)),
    ]
    args = [q, kd, vd]
    if has_cache:
        past = cache_k.shape[1]
        cspec = pl.BlockSpec((1, past, LANES), lambda b, i, j: (b * DEPTH + layer, 0, 0))
        in_specs += [cspec, cspec]
        args += [cache_k, cache_v]
    return pl.pallas_call(
        functools.partial(_gqa_kernel, has_cache=has_cache, tk=tk),
        grid=(n_batch, nq, 4),
        in_specs=in_specs,
        out_specs=pl.BlockSpec((tq, LANES), lambda b, i, j: (b * nq + i, j)),
        out_shape=jax.ShapeDtypeStruct((n, 4 * LANES), BF16),
        compiler_params=_params(),
        name="gqa_lat" if has_cache else "gqa_ctx",
    )(*args)


def _diff_kernel(*refs, has_cache, tk, lam_init):
    if has_cache:
        q_ref, k_ref, v_ref, ck_ref, cv_ref, lam_ref, g_ref, bd_ref, o_ref = refs
    else:
        q_ref, k_ref, v_ref, lam_ref, g_ref, bd_ref, o_ref = refs
    q = q_ref[0]
    lane = lax.broadcasted_iota(jnp.int32, q.shape, 1)
    zero = jnp.zeros_like(q)
    q_masks = [jnp.where((lane >= 32 * c) & (lane < 32 * (c + 1)), q, zero) for c in range(4)]
    cache_kv = None
    if has_cache:
        cache_kv = (ck_ref[0].astype(BF16), cv_ref[0].astype(BF16))
    res = _attend(q_masks, k_ref, v_ref, cache_kv, tk)
    lv = lam_ref[0]
    lam = (jnp.exp(jnp.sum(lv[0:1] * lv[1:2], axis=-1, keepdims=True))
           - jnp.exp(jnp.sum(lv[2:3] * lv[3:4], axis=-1, keepdims=True)) + lam_init)
    heads = []
    for i in range(2):
        (a1, l1), (a2, l2) = res[2 * i], res[2 * i + 1]
        heads.append(a1 / l1 - lam * (a2 / l2))
    lane_o = lax.broadcasted_iota(jnp.int32, heads[0].shape, 1)
    o = jnp.where(lane_o < 64, heads[0], heads[1])
    o = _seg_rms(o, bd_ref[...], g_ref[...]) * (1.0 - lam_init)
    o_ref[...] = o.astype(BF16)


def _diff_attention(cq, ck, cv, cache_k, cache_v, layer, diff_lambda, subln_g, bd, *,
                    n_batch, t_seq, tq, tk, lam_init):
    n = cq.shape[1]
    nq = t_seq // tq
    has_cache = cache_k is not None
    in_specs = [
        pl.BlockSpec((1, tq, LANES), lambda b, i, j: (j, b * nq + i, 0)),
        pl.BlockSpec((1, t_seq, LANES), lambda b, i, j: (j, b, 0)),
        pl.BlockSpec((1, t_seq, LANES), lambda b, i, j: (j, b, 0)),
    ]
    args = [cq, ck, cv]
    if has_cache:
        past = cache_k.shape[1]
        cspec = pl.BlockSpec((1, past, LANES), lambda b, i, j: (b * DEPTH + layer, 0, j))
        in_specs += [cspec, cspec]
        args += [cache_k, cache_v]
    in_specs += [
        pl.BlockSpec((1, 4, C_QK_DIM), lambda b, i, j: (layer, 0, 0)),
        pl.BlockSpec((1, LANES), lambda b, i, j: (0, 0)),
        pl.BlockSpec((LANES, LANES), lambda b, i, j: (0, 0)),
    ]
    args += [diff_lambda, subln_g, bd]
    return pl.pallas_call(
        functools.partial(_diff_kernel, has_cache=has_cache, tk=tk, lam_init=lam_init),
        grid=(n_batch, nq, 2),
        in_specs=in_specs,
        out_specs=pl.BlockSpec((tq, LANES), lambda b, i, j: (b * nq + i, j)),
        out_shape=jax.ShapeDtypeStruct((n, 2 * LANES), BF16),
        compiler_params=_params(),
        name="diff_lat" if has_cache else "diff_ctx",
    )(*args)


HALO = 16


def _route(logits_t, bias_col):
    scores = _sigmoid(logits_t)
    sel = scores + bias_col
    row = lambda a, e: a[e:e + 1, :]
    grp = []
    for g in range(N_EXPERT_GROUPS):
        s0, s1, s2, s3 = (row(sel, 4 * g + k) for k in range(4))
        hi1, lo1 = jnp.maximum(s0, s1), jnp.minimum(s0, s1)
        hi2, lo2 = jnp.maximum(s2, s3), jnp.minimum(s2, s3)
        grp.append(jnp.maximum(hi1, hi2) + jnp.maximum(jnp.minimum(hi1, hi2), jnp.maximum(lo1, lo2)))
    best = grp[0]
    tg = jnp.zeros_like(best, dtype=jnp.int32)
    for g in range(1, N_EXPERT_GROUPS):
        better = grp[g] > best
        best = jnp.where(better, grp[g], best)
        tg = jnp.where(better, g, tg)

    def in_group(a, k):
        out = row(a, k)
        for g in range(1, N_EXPERT_GROUPS):
            out = jnp.where(tg == g, row(a, 4 * g + k), out)
        return out

    vals = [in_group(sel, k) for k in range(EXPERTS_PER_GROUP)]
    scs = [in_group(scores, k) for k in range(EXPERTS_PER_GROUP)]

    def first_argmax(vs):
        bv, bi = vs[0], jnp.zeros_like(tg)
        for k in range(1, len(vs)):
            better = vs[k] > bv
            bv = jnp.where(better, vs[k], bv)
            bi = jnp.where(better, k, bi)
        return bi

    i1 = first_argmax(vals)
    i2 = first_argmax([jnp.where(i1 == k, -jnp.inf, vals[k]) for k in range(EXPERTS_PER_GROUP)])

    def pick(vs, idx):
        out = vs[0]
        for k in range(1, len(vs)):
            out = jnp.where(idx == k, vs[k], out)
        return out

    w1, w2 = pick(scs, i1), pick(scs, i2)
    tot = w1 + w2
    return 4 * tg + i1, 4 * tg + i2, w1 / tot, w2 / tot


def _outproj_kernel(oa_ref, gb_ref, gu_ref, gup_ref, gun_ref, cw_ref, oc_ref, x_ref, mod_ref,
                    w_ref, g2_ref, rwh_ref, rwl_ref, rb_ref,
                    x1_ref, h2_ref, route_ref, gates_ref, *, tm, t_seq):
    i = pl.program_id(0)
    gu = gu_ref[...].astype(F32)
    rowi = lax.broadcasted_iota(jnp.int32, gu.shape, 0)
    pos = (i * tm + rowi) % t_seq
    prev = pltpu.roll(gu, 1, 0)
    prev = jnp.where(rowi == 0, gup_ref[HALO - 1:HALO, :].astype(F32), prev)
    prev = jnp.where(pos == 0, 0.0, prev)
    nxt = pltpu.roll(gu, tm - 1, 0)
    nxt = jnp.where(rowi == tm - 1, gun_ref[0:1, :].astype(F32), nxt)
    nxt = jnp.where(pos == t_seq - 1, 0.0, nxt)
    cw = cw_ref[0]
    ob = gb_ref[...].astype(F32) * (prev * cw[0:1] + gu * cw[1:2] + nxt * cw[2:3])

    mix = (_dot(oa_ref[...], w_ref[0:512, :])
           + _dot(ob.astype(BF16), w_ref[512:768, :])
           + _dot(oc_ref[...], w_ref[768:1024, :]))
    x1 = x_ref[...] + mod_ref[0, 0, 2:3, :] * mix
    x1_ref[...] = x1
    ms = jnp.mean(x1 * x1, axis=-1, keepdims=True)
    h2 = x1 * lax.rsqrt(ms + EPS) * g2_ref[...]
    h2 = h2 * (1.0 + mod_ref[0, 0, 4:5, :]) + mod_ref[0, 0, 3:4, :]
    h2_ref[...] = h2.astype(BF16)

    hh, hl = _split(h2)
    logits = _dot(hh, rwh_ref[...]) + _dot(hh, rwl_ref[...]) + _dot(hl, rwh_ref[...])
    logits_t = logits.T[0:N_EXPERTS, :]
    e0, e1, w0, w1 = _route(logits_t, rb_ref[...])
    zeros = jnp.zeros_like(w0)
    route_ref[...] = jnp.concatenate(
        [e0.astype(F32), e1.astype(F32), w0, w1, zeros, zeros, zeros, zeros], axis=0)
    erow = lax.broadcasted_iota(jnp.int32, (LANES, tm), 0)
    gates_t = jnp.where(erow == e0, w0, 0.0) + jnp.where(erow == e1, w1, 0.0)
    gates_ref[...] = gates_t.T


def _outproj(oa, gb, gu, conv_w, oc, x, mod, layer, mod_row, w_out, g2, rwh, rwl, rb, *, tm, t_seq):
    n = x.shape[0]
    row = lambda i: (i, 0)
    const2 = lambda i: (0, 0)
    hb = tm // HALO
    last = n // HALO - 1
    return pl.pallas_call(
        functools.partial(_outproj_kernel, tm=tm, t_seq=t_seq),
        grid=(n // tm,),
        in_specs=[
            pl.BlockSpec((tm, 4 * LANES), row),
            pl.BlockSpec((tm, B_WIDTH), row),
            pl.BlockSpec((tm, B_WIDTH), row),
            pl.BlockSpec((HALO, B_WIDTH), lambda i: (jnp.maximum(i * hb - 1, 0), 0)),
            pl.BlockSpec((HALO, B_WIDTH), lambda i: (jnp.minimum((i + 1) * hb, last), 0)),
            pl.BlockSpec((1, 3, B_WIDTH), lambda i: (layer, 0, 0)),
            pl.BlockSpec((tm, 2 * LANES), row),
            pl.BlockSpec((tm, D_MODEL), row),
            pl.BlockSpec((1, 1, N_MOD, D_MODEL), lambda i: (layer, mod_row(i), 0, 0)),
            pl.BlockSpec((D_MODEL, D_MODEL), const2),
            pl.BlockSpec((1, D_MODEL), const2),
            pl.BlockSpec((D_MODEL, LANES), const2),
            pl.BlockSpec((D_MODEL, LANES), const2),
            pl.BlockSpec((N_EXPERTS, 1), const2),
        ],
        out_specs=[
            pl.BlockSpec((tm, D_MODEL), row),
            pl.BlockSpec((tm, D_MODEL), row),
            pl.BlockSpec((8, tm), lambda i: (0, i)),
            pl.BlockSpec((tm, LANES), row),
        ],
        out_shape=[
            jax.ShapeDtypeStruct((n, D_MODEL), F32),
            jax.ShapeDtypeStruct((n, D_MODEL), BF16),
            jax.ShapeDtypeStruct((8, n), F32),
            jax.ShapeDtypeStruct((n, LANES), F32),
        ],
        compiler_params=_params(),
        name="outproj",
    )(oa, gb, gu, gu, gu, conv_w, oc, x, mod, w_out, g2, rwh, rwl, rb)


def _moe_kernel(h_ref, gates_ref, x1_ref, mod_ref, w1_ref, w3_ref, w2_ref, fg_ref, o_ref, acc_ref,
                *, final):
    e = pl.program_id(1)

    @pl.when(e == 0)
    def _():
        acc_ref[...] = jnp.zeros_like(acc_ref)

    h = h_ref[...]
    a = _dot(h, w1_ref[0])
    b = _dot(h, w3_ref[0])
    gates = gates_ref[...]
    lane = lax.broadcasted_iota(jnp.int32, gates.shape, 1)
    g = jnp.sum(jnp.where(lane == e, gates, 0.0), axis=-1, keepdims=True)
    hid = (a * _sigmoid(a)) * b * g
    acc_ref[...] += _dot(hid.astype(BF16), w2_ref[0])

    @pl.when(e == N_EXPERTS - 1)
    def _():
        x2 = x1_ref[...] + mod_ref[0, 0, 5:6, :] * acc_ref[...]
        if final:
            ms = jnp.mean(x2 * x2, axis=-1, keepdims=True)
            x2 = x2 * lax.rsqrt(ms + EPS) * fg_ref[...]
        o_ref[...] = x2


def _moe(h2, gates, x1, mod, layer, mod_row, w1, w3, w2, fg, *, tm, final):
    n = x1.shape[0]
    row = lambda i, e: (i, 0)
    return pl.pallas_call(
        functools.partial(_moe_kernel, final=final),
        grid=(n // tm, N_EXPERTS),
        in_specs=[
            pl.BlockSpec((tm, D_MODEL), row),
            pl.BlockSpec((tm, LANES), row),
            pl.BlockSpec((tm, D_MODEL), row),
            pl.BlockSpec((1, 1, N_MOD, D_MODEL), lambda i, e: (layer, mod_row(i), 0, 0)),
            pl.BlockSpec((1, D_MODEL, EXPERT_FF), lambda i, e: (layer * N_EXPERTS + e, 0, 0)),
            pl.BlockSpec((1, D_MODEL, EXPERT_FF), lambda i, e: (layer * N_EXPERTS + e, 0, 0)),
            pl.BlockSpec((1, EXPERT_FF, D_MODEL), lambda i, e: (layer * N_EXPERTS + e, 0, 0)),
            pl.BlockSpec((1, D_MODEL), lambda i, e: (0, 0)),
        ],
        out_specs=pl.BlockSpec((tm, D_MODEL), row),
        out_shape=jax.ShapeDtypeStruct((n, D_MODEL), F32),
        scratch_shapes=[pltpu.VMEM((tm, D_MODEL), F32)],
        compiler_params=_params(),
        name="moe",
    )(h2, gates, x1, mod, w1, w3, w2, fg)


def _rope_tables(n_tokens, rot_dim):
    pos = np.arange(n_tokens)
    r = (pos // GRID_W).astype(np.float64)[:, None]
    col = (pos % GRID_W).astype(np.float64)[:, None]
    n_freq = rot_dim // 4
    inv_freq = ROPE_THETA ** (-np.arange(n_freq, dtype=np.float64) / n_freq)
    ang = np.concatenate([r * inv_freq, col * inv_freq], axis=-1)
    ang = np.repeat(ang, 2, axis=-1)
    sign = np.tile(np.array([-1.0, 1.0]), rot_dim // 2)
    cos = np.tile(np.cos(ang), (1, LANES // rot_dim))
    sin = np.tile(np.sin(ang) * sign, (1, LANES // rot_dim))
    return jnp.asarray(cos, F32), jnp.asarray(sin, F32)


def _block_diag_ones():
    idx = np.arange(LANES) // 64
    return jnp.asarray((idx[:, None] == idx[None, :]).astype(np.float32), BF16)


def kernel(x_prompt, x_sample, cache_gqa_k, cache_gqa_v, cache_diff_k, cache_diff_v, c, c_ctx, w_in,
           w_out, norm1_g, norm2_g, ada_w, ada_b, gqa_qnorm_g, gqa_knorm_g, conv_w, diff_lambda,
           diff_subln_g, router_w, router_bias, moe_w1, moe_w3, moe_w2, final_g):
    n_ctx_b, t_ctx, _ = x_prompt.shape
    n_lat_b, t_lat, _ = x_sample.shape
    past = cache_gqa_k.shape[2]

    cvec = jnp.concatenate([c_ctx[None, :], c, jnp.zeros((8 - 1 - n_lat_b, D_MODEL), F32)], axis=0)
    mod = _adaln(cvec, ada_w, ada_b).reshape(DEPTH, 8, N_MOD, D_MODEL)

    w_in_b = w_in.astype(BF16)
    w_out_b = w_out.astype(BF16)
    w1_b = moe_w1.astype(BF16).reshape(DEPTH * N_EXPERTS, D_MODEL, EXPERT_FF)
    w3_b = moe_w3.astype(BF16).reshape(DEPTH * N_EXPERTS, D_MODEL, EXPERT_FF)
    w2_b = moe_w2.astype(BF16).reshape(DEPTH * N_EXPERTS, EXPERT_FF, D_MODEL)
    rw = jnp.pad(router_w, ((0, 0), (0, LANES - N_EXPERTS)))
    rwh = rw.astype(BF16)
    rwl = (rw - rwh.astype(F32)).astype(BF16)
    rb = router_bias.reshape(N_EXPERTS, 1)
    bd = _block_diag_ones()
    fg = final_g.reshape(1, D_MODEL)
    tables = _rope_tables(t_lat, A_HEAD_DIM) + _rope_tables(t_lat, C_QK_DIM)

    ck_cache = cache_gqa_k.reshape(n_lat_b * DEPTH, past, A_KV_HEADS * A_HEAD_DIM)
    cv_cache = cache_gqa_v.reshape(n_lat_b * DEPTH, past, A_KV_HEADS * A_HEAD_DIM)
    dk_cache = cache_diff_k.reshape(n_lat_b * DEPTH, past, C_HEADS * 2 * C_QK_DIM)
    dv_cache = cache_diff_v.reshape(n_lat_b * DEPTH, past, C_HEADS * C_V_DIM)

    xp = x_prompt.reshape(n_ctx_b * t_ctx, D_MODEL)
    xs = x_sample.reshape(n_lat_b * t_lat, D_MODEL)
    tm = 512
    tm_moe = 1024
    new_state = []
    for l in range(DEPTH):
        lam_init = 0.8 - 0.6 * math.exp(-0.3 * l)
        g1 = norm1_g[l].reshape(1, D_MODEL)
        g2 = norm2_g[l].reshape(1, D_MODEL)
        qg = jnp.tile(gqa_qnorm_g[l], 2).reshape(1, LANES)
        kg = jnp.tile(gqa_knorm_g[l], 2).reshape(1, LANES)
        sg = jnp.tile(diff_subln_g[l], 2).reshape(1, LANES)
        final = l == DEPTH - 1

        def run_stream(x, n_batch, t_seq, mod_row_of, tabs, caches, tq, tk):
            ctx = caches is None
            outs = _inproj(x, mod, l, lambda i: mod_row_of(i, tm), g1, w_in_b[l], qg, kg, bd, tabs,
                           tm=tm, ctx=ctx)
            q, kd, vd, gb, gu, cq, ck, cv = outs[:8]
            c_gk, c_gv, c_dk, c_dv = caches if caches is not None else (None,) * 4
            oa = _gqa_attention(q, kd, vd, c_gk, c_gv, l, n_batch=n_batch, t_seq=t_seq, tq=tq, tk=tk)
            oc = _diff_attention(cq, ck, cv, c_dk, c_dv, l, diff_lambda, sg, bd, n_batch=n_batch,
                                 t_seq=t_seq, tq=tq, tk=tk, lam_init=lam_init)
            x1, h2, _, gates = _outproj(oa, gb, gu, conv_w, oc, x, mod, l,
                                        lambda i: mod_row_of(i, tm), w_out_b[l], g2, rwh, rwl, rb,
                                        tm=tm, t_seq=t_seq)
            x2 = _moe(h2, gates, x1, mod, l, lambda i: mod_row_of(i, tm_moe), w1_b, w3_b, w2_b, fg,
                      tm=tm_moe, final=final)
            return x2, outs[8:]

        xp, state = run_stream(xp, n_ctx_b, t_ctx, lambda i, t: 0, None, None, t_ctx, t_ctx)
        new_state.append(state)
        xs, _ = run_stream(xs, n_lat_b, t_lat, lambda i, t: 1 + (i * t) // t_lat, tables,
                           (ck_cache, cv_cache, dk_cache, dv_cache), 512, 512)

    def stack(k, heads, dim):
        return jnp.stack([s[k].reshape(n_ctx_b, t_ctx, heads, dim) for s in new_state], axis=1)

    return (xp.reshape(n_ctx_b, t_ctx, D_MODEL), xs.reshape(n_lat_b, t_lat, D_MODEL),
            stack(0, A_KV_HEADS, A_HEAD_DIM), stack(1, A_KV_HEADS, A_HEAD_DIM),
            stack(2, C_HEADS, 2 * C_QK_DIM), stack(3, C_HEADS, C_V_DIM))
```

```python
import functools
import math

import numpy as np
import jax
import jax.numpy as jnp
from jax import lax
from jax.experimental import pallas as pl
from jax.experimental.pallas import tpu as pltpu

D_MODEL = 1024
DEPTH = 2
GRID_W = 64
A_HEADS = 8
A_KV_HEADS = 2
A_HEAD_DIM = 64
B_WIDTH = 256
C_HEADS = 4
C_QK_DIM = 32
C_V_DIM = 64
N_EXPERTS = 16
EXPERTS_PER_GROUP = 4
N_EXPERT_GROUPS = 4
EXPERT_FF = 512
ROPE_THETA = 10000.0
EPS = 1e-6
N_MOD = 6
W_IN_WIDTH = 2304
LOG2E = math.log2(math.e)

LANES = 128
CHUNK = 128
HEAD_ROWS = 64
VMEM_LIMIT_BYTES = 56 * 1024 * 1024

F32 = jnp.float32
BF16 = jnp.bfloat16

_dot = functools.partial(jnp.dot, preferred_element_type=F32)


def _dot_nt(a, b):
    return lax.dot_general(a, b, (((1,), (1,)), ((), ())), preferred_element_type=F32)


def _split(a):
    hi = a.astype(BF16)
    lo = (a - hi.astype(F32)).astype(BF16)
    return hi, lo


def _dot3(a, b):
    ah, al = _split(a)
    bh, bl = _split(b)
    return _dot(ah, bh) + _dot(ah, bl) + _dot(al, bh)


def _sigmoid(x):
    return 1.0 / (1.0 + jnp.exp(-x))


def _seg_rms(x, bd, g):
    hi, lo = _split(x * x)
    ss = _dot(hi, bd) + _dot(lo, bd)
    return x * lax.rsqrt(ss * (1.0 / 64.0) + EPS) * g


def _rope(x, cos, sin_signed):
    lane = lax.broadcasted_iota(jnp.int32, x.shape, 1)
    even = (lane & 1) == 0
    partner = jnp.where(even, pltpu.roll(x, LANES - 1, 1), pltpu.roll(x, 1, 1))
    return x * cos + partner * sin_signed


def _params():
    return pltpu.CompilerParams(vmem_limit_bytes=VMEM_LIMIT_BYTES)


def _adaln_kernel(c_ref, w_ref, b_ref, o_ref):
    c = c_ref[...]
    s = c * _sigmoid(c)
    o_ref[0] = _dot3(s, w_ref[0]) + b_ref[0]


def _adaln(cvec, ada_w, ada_b):
    tn = 1536
    nt = (N_MOD * D_MODEL) // tn
    return pl.pallas_call(
        _adaln_kernel,
        grid=(DEPTH, nt),
        in_specs=[
            pl.BlockSpec((8, D_MODEL), lambda l, n: (0, 0)),
            pl.BlockSpec((1, D_MODEL, tn), lambda l, n: (l, 0, n)),
            pl.BlockSpec((1, 1, tn), lambda l, n: (l, 0, n)),
        ],
        out_specs=pl.BlockSpec((1, 8, tn), lambda l, n: (l, 0, n)),
        out_shape=jax.ShapeDtypeStruct((DEPTH, 8, N_MOD * D_MODEL), F32),
        compiler_params=_params(),
        name="adaln",
    )(cvec, ada_w, ada_b.reshape(DEPTH, 1, N_MOD * D_MODEL))


def _inproj_kernel(*refs, rope, ctx):
    it = iter(refs)
    x_ref, mod_ref, g1_ref, w_ref, qg_ref, kg_ref, bd_ref = (next(it) for _ in range(7))
    if rope:
        cos_a, sin_a, cos_c, sin_c = (next(it) for _ in range(4))
    q_ref, k_ref, vt_ref, gb_ref, gu_ref, cq_ref, ck_ref, cvt_ref = (next(it) for _ in range(8))
    if ctx:
        k32_ref, v32_ref, ck32_ref, cv32_ref = (next(it) for _ in range(4))

    x = x_ref[...]
    ms = jnp.mean(x * x, axis=-1, keepdims=True)
    y = x * lax.rsqrt(ms + EPS) * g1_ref[...]
    h = y * (1.0 + mod_ref[0, 0, 1:2, :]) + mod_ref[0, 0, 0:1, :]
    hb = h.astype(BF16)
    bd = bd_ref[...]

    def seg(lo, width):
        return _dot(hb, w_ref[:, lo:lo + width])

    def store_transposed(dst, a):
        at = a.T.astype(BF16)
        for c in range(a.shape[0] // CHUNK):
            dst[c] = at[:, CHUNK * c:CHUNK * (c + 1)]

    for j in range(4):
        q = _seg_rms(seg(LANES * j, LANES), bd, qg_ref[...])
        if rope:
            q = _rope(q, cos_a[...], sin_a[...])
        q_ref[j] = (q * (A_HEAD_DIM ** -0.5 * LOG2E)).astype(BF16)

    k = _seg_rms(seg(512, LANES), bd, kg_ref[...])
    if rope:
        k = _rope(k, cos_a[...], sin_a[...])
    v = seg(640, LANES)
    if ctx:
        k32_ref[...] = k
        v32_ref[...] = v
    k_ref[0] = k.astype(BF16)
    store_transposed(vt_ref.at[0], v)

    gb_ref[...] = seg(768, B_WIDTH).astype(BF16)
    gu_ref[...] = (seg(1024, B_WIDTH) * seg(1280, B_WIDTH)).astype(BF16)

    for j in range(2):
        cq = seg(1536 + LANES * j, LANES)
        ck = seg(1792 + LANES * j, LANES)
        cv = seg(2048 + LANES * j, LANES)
        if ctx:
            ck32_ref[:, LANES * j:LANES * (j + 1)] = ck
            cv32_ref[:, LANES * j:LANES * (j + 1)] = cv
        if rope:
            cq = _rope(cq, cos_c[...], sin_c[...])
            ck = _rope(ck, cos_c[...], sin_c[...])
        cq_ref[j] = (cq * (C_QK_DIM ** -0.5 * LOG2E)).astype(BF16)
        ck_ref[j] = ck.astype(BF16)
        store_transposed(cvt_ref.at[j], cv)


def _inproj(x, mod, layer, mod_row, g1, w_in, qg, kg, bd, tables, *, tm, ctx):
    n = x.shape[0]
    rope = tables is not None
    row = lambda i: (i, 0)
    const2 = lambda i: (0, 0)
    pair3 = lambda i: (0, i, 0)
    tile4 = lambda i: (0, i, 0, 0)
    nc = tm // CHUNK
    in_specs = [
        pl.BlockSpec((tm, D_MODEL), row),
        pl.BlockSpec((1, 1, N_MOD, D_MODEL), lambda i: (layer, mod_row(i), 0, 0)),
        pl.BlockSpec((1, D_MODEL), const2),
        pl.BlockSpec((D_MODEL, W_IN_WIDTH), const2),
        pl.BlockSpec((1, LANES), const2),
        pl.BlockSpec((1, LANES), const2),
        pl.BlockSpec((LANES, LANES), const2),
    ]
    args = [x, mod, g1, w_in, qg, kg, bd]
    if rope:
        t = tables[0].shape[0]
        tab = lambda i: (i % (t // tm), 0)
        in_specs += [pl.BlockSpec((tm, LANES), tab)] * 4
        args += list(tables)
    out_specs = [
        pl.BlockSpec((4, tm, LANES), pair3),
        pl.BlockSpec((1, tm, LANES), pair3),
        pl.BlockSpec((1, nc, LANES, CHUNK), tile4),
        pl.BlockSpec((tm, B_WIDTH), row),
        pl.BlockSpec((tm, B_WIDTH), row),
        pl.BlockSpec((2, tm, LANES), pair3),
        pl.BlockSpec((2, tm, LANES), pair3),
        pl.BlockSpec((2, nc, LANES, CHUNK), tile4),
    ]
    out_shape = [
        jax.ShapeDtypeStruct((4, n, LANES), BF16),
        jax.ShapeDtypeStruct((1, n, LANES), BF16),
        jax.ShapeDtypeStruct((1, n // CHUNK, LANES, CHUNK), BF16),
        jax.ShapeDtypeStruct((n, B_WIDTH), BF16),
        jax.ShapeDtypeStruct((n, B_WIDTH), BF16),
        jax.ShapeDtypeStruct((2, n, LANES), BF16),
        jax.ShapeDtypeStruct((2, n, LANES), BF16),
        jax.ShapeDtypeStruct((2, n // CHUNK, LANES, CHUNK), BF16),
    ]
    if ctx:
        out_specs += [
            pl.BlockSpec((tm, LANES), row),
            pl.BlockSpec((tm, LANES), row),
            pl.BlockSpec((tm, 2 * LANES), row),
            pl.BlockSpec((tm, 2 * LANES), row),
        ]
        out_shape += [
            jax.ShapeDtypeStruct((n, LANES), F32),
            jax.ShapeDtypeStruct((n, LANES), F32),
            jax.ShapeDtypeStruct((n, 2 * LANES), F32),
            jax.ShapeDtypeStruct((n, 2 * LANES), F32),
        ]
    return pl.pallas_call(
        functools.partial(_inproj_kernel, rope=rope, ctx=ctx),
        grid=(n // tm,),
        in_specs=in_specs,
        out_specs=out_specs,
        out_shape=out_shape,
        compiler_params=_params(),
        name="inproj_ctx" if ctx else "inproj_lat",
    )(*args)


def _chunks_update(ks, vts, q_masks, halves, state):
    nu = len(q_masks)
    rows = lambda vt, u: vt[HEAD_ROWS * halves[u]:HEAD_ROWS * (halves[u] + 1), :]
    scores = [[_dot_nt(k, q_masks[u]) for u in range(nu)] for k in ks]
    ms = [[None] * nu for _ in ks]
    ps = [[None] * nu for _ in ks]
    for u in range(nu):
        m = state[u][0]
        for c in range(len(ks)):
            m = jnp.maximum(m, jnp.max(scores[c][u], axis=0, keepdims=True))
            ms[c][u] = m
            ps[c][u] = jnp.exp2(scores[c][u] - m)
    pvs = [[_dot(rows(vts[c], u), ps[c][u].astype(BF16)) for u in range(nu)] for c in range(len(ks))]
    out = []
    for u in range(nu):
        m, l, acc = state[u]
        for c in range(len(ks)):
            alpha = jnp.exp2(m - ms[c][u])
            m = ms[c][u]
            l = alpha * l + jnp.sum(ps[c][u], axis=0, keepdims=True)
            acc = alpha * acc + pvs[c][u]
        out.append((m, l, acc))
    return tuple(out)


def _attend(q_masks, halves, k_ref, kj, vt_ref, vj, cache, unroll=4):
    tq = q_masks[0].shape[0]
    nu = len(q_masks)
    state = tuple((jnp.full((1, tq), -jnp.inf, F32), jnp.zeros((1, tq), F32),
                   jnp.zeros((HEAD_ROWS, tq), F32)) for _ in range(nu))
    if cache is not None:
        kc, vc = cache
        vct = vc.T
        nc = kc.shape[0] // CHUNK
        ks = [kc[CHUNK * c:CHUNK * (c + 1), :].astype(BF16) for c in range(nc)]
        vts = [vct[:, CHUNK * c:CHUNK * (c + 1)].astype(BF16) for c in range(nc)]
        state = _chunks_update(ks, vts, q_masks, halves, state)
    n_chunks = k_ref.shape[1] // CHUNK
    group = min(n_chunks, unroll)

    def step(g, carry):
        ks, vts = [], []
        for cc in range(group):
            c = g * group + cc
            start = c * CHUNK
            if not isinstance(c, int):
                start = pl.multiple_of(start, CHUNK)
            ks.append(k_ref[kj, pl.ds(start, CHUNK), :])
            vts.append(vt_ref[vj, c])
        return _chunks_update(ks, vts, q_masks, halves, carry)

    if n_chunks == group:
        state = step(0, state)
    else:
        state = lax.fori_loop(0, n_chunks // group, step, state)
    return [(acc, l) for (_, l, acc) in state]


def _lane_mask(q, lo, width):
    lane = lax.broadcasted_iota(jnp.int32, q.shape, 1)
    return jnp.where((lane >= lo) & (lane < lo + width), q, jnp.zeros_like(q))


def _gqa_kernel(*refs, has_cache, nj):
    if has_cache:
        q_ref, k_ref, vt_ref, ck_ref, cv_ref, o_ref = refs
        cache = (ck_ref[0], cv_ref[0])
    else:
        q_ref, k_ref, vt_ref, o_ref = refs
        cache = None
    for j in range(nj):
        q = q_ref[j]
        q_masks = [_lane_mask(q, 0, 64), _lane_mask(q, 64, 64)]
        (a0, l0), (a1, l1) = _attend(q_masks, (0, 1), k_ref, 0, vt_ref, 0, cache)
        ot = jnp.concatenate([a0 / l0, a1 / l1], axis=0)
        o_ref[:, LANES * j:LANES * (j + 1)] = ot.T.astype(BF16)


def _gqa_attention(q, k, vt, cache_k, cache_v, layer, *, n_batch, t_seq, tq, nj):
    n = q.shape[1]
    nq = t_seq // tq
    has_cache = cache_k is not None
    in_specs = [
        pl.BlockSpec((nj, tq, LANES), lambda b, i, j: (j, b * nq + i, 0)),
        pl.BlockSpec((1, t_seq, LANES), lambda b, i, j: (0, b, 0)),
        pl.BlockSpec((1, t_seq // CHUNK, LANES, CHUNK), lambda b, i, j: (0, b, 0, 0)),
    ]
    args = [q, k, vt]
    if has_cache:
        past = cache_k.shape[1]
        cspec = pl.BlockSpec((1, past, LANES), lambda b, i, j: (b * DEPTH + layer, 0, 0))
        in_specs += [cspec, cspec]
        args += [cache_k, cache_v]
    return pl.pallas_call(
        functools.partial(_gqa_kernel, has_cache=has_cache, nj=nj),
        grid=(n_batch, nq, 4 // nj),
        in_specs=in_specs,
        out_specs=pl.BlockSpec((tq, LANES * nj), lambda b, i, j: (b * nq + i, j)),
        out_shape=jax.ShapeDtypeStruct((n, 4 * LANES), BF16),
        compiler_params=_params(),
        name="gqa_lat" if has_cache else "gqa_ctx",
    )(*args)


def _diff_kernel(*refs, has_cache, nj, lam_init):
    if has_cache:
        q_ref, k_ref, vt_ref, ck_ref, cv_ref, lam_ref, g_ref, o_ref = refs
    else:
        q_ref, k_ref, vt_ref, lam_ref, g_ref, o_ref = refs
    lv = lam_ref[0]
    lam = (jnp.exp(jnp.sum(lv[0:1] * lv[1:2], axis=-1, keepdims=True))
           - jnp.exp(jnp.sum(lv[2:3] * lv[3:4], axis=-1, keepdims=True)) + lam_init)
    for j in range(nj):
        q = q_ref[j]
        q_masks = [_lane_mask(q, 32 * c, 32) for c in range(4)]
        cache = None
        if has_cache:
            cache = (ck_ref[0, :, LANES * j:LANES * (j + 1)], cv_ref[0, :, LANES * j:LANES * (j + 1)])
        res = _attend(q_masks, (0, 0, 1, 1), k_ref, j, vt_ref, j, cache)
        heads = []
        for i in range(2):
            (a1, l1), (a2, l2) = res[2 * i], res[2 * i + 1]
            o = a1 / l1 - lam * (a2 / l2)
            ms = jnp.mean(o * o, axis=0, keepdims=True)
            heads.append(o * lax.rsqrt(ms + EPS))
        o = jnp.concatenate(heads, axis=0).T * g_ref[...] * (1.0 - lam_init)
        o_ref[:, LANES * j:LANES * (j + 1)] = o.astype(BF16)


def _diff_attention(cq, ck, cvt, cache_k, cache_v, layer, diff_lambda, subln_g, *,
                    n_batch, t_seq, tq, nj, lam_init):
    n = cq.shape[1]
    nq = t_seq // tq
    has_cache = cache_k is not None
    in_specs = [
        pl.BlockSpec((nj, tq, LANES), lambda b, i, j: (j, b * nq + i, 0)),
        pl.BlockSpec((nj, t_seq, LANES), lambda b, i, j: (j, b, 0)),
        pl.BlockSpec((nj, t_seq // CHUNK, LANES, CHUNK), lambda b, i, j: (j, b, 0, 0)),
    ]
    args = [cq, ck, cvt]
    if has_cache:
        past = cache_k.shape[1]
        cspec = pl.BlockSpec((1, past, LANES * nj), lambda b, i, j: (b * DEPTH + layer, 0, j))
        in_specs += [cspec, cspec]
        args += [cache_k, cache_v]
    in_specs += [
        pl.BlockSpec((1, 4, C_QK_DIM), lambda b, i, j: (layer, 0, 0)),
        pl.BlockSpec((1, LANES), lambda b, i, j: (0, 0)),
    ]
    args += [diff_lambda, subln_g]
    return pl.pallas_call(
        functools.partial(_diff_kernel, has_cache=has_cache, nj=nj, lam_init=lam_init),
        grid=(n_batch, nq, 2 // nj),
        in_specs=in_specs,
        out_specs=pl.BlockSpec((tq, LANES * nj), lambda b, i, j: (b * nq + i, j)),
        out_shape=jax.ShapeDtypeStruct((n, 2 * LANES), BF16),
        compiler_params=_params(),
        name="diff_lat" if has_cache else "diff_ctx",
    )(*args)


HALO = 16


def _route(logits_t, bias_col):
    scores = _sigmoid(logits_t)
    sel = scores + bias_col
    row = lambda a, e: a[e:e + 1, :]
    grp = []
    for g in range(N_EXPERT_GROUPS):
        s0, s1, s2, s3 = (row(sel, 4 * g + k) for k in range(4))
        hi1, lo1 = jnp.maximum(s0, s1), jnp.minimum(s0, s1)
        hi2, lo2 = jnp.maximum(s2, s3), jnp.minimum(s2, s3)
        grp.append(jnp.maximum(hi1, hi2) + jnp.maximum(jnp.minimum(hi1, hi2), jnp.maximum(lo1, lo2)))
    best = grp[0]
    tg = jnp.zeros_like(best, dtype=jnp.int32)
    for g in range(1, N_EXPERT_GROUPS):
        better = grp[g] > best
        best = jnp.where(better, grp[g], best)
        tg = jnp.where(better, g, tg)

    def in_group(a, k):
        out = row(a, k)
        for g in range(1, N_EXPERT_GROUPS):
            out = jnp.where(tg == g, row(a, 4 * g + k), out)
        return out

    vals = [in_group(sel, k) for k in range(EXPERTS_PER_GROUP)]
    scs = [in_group(scores, k) for k in range(EXPERTS_PER_GROUP)]

    def first_argmax(vs):
        bv, bi = vs[0], jnp.zeros_like(tg)
        for k in range(1, len(vs)):
            better = vs[k] > bv
            bv = jnp.where(better, vs[k], bv)
            bi = jnp.where(better, k, bi)
        return bi

    i1 = first_argmax(vals)
    i2 = first_argmax([jnp.where(i1 == k, -jnp.inf, vals[k]) for k in range(EXPERTS_PER_GROUP)])

    def pick(vs, idx):
        out = vs[0]
        for k in range(1, len(vs)):
            out = jnp.where(idx == k, vs[k], out)
        return out

    w1, w2 = pick(scs, i1), pick(scs, i2)
    tot = w1 + w2
    return 4 * tg + i1, 4 * tg + i2, w1 / tot, w2 / tot


def _outproj_kernel(oa_ref, gb_ref, gu_ref, gup_ref, gun_ref, cw_ref, oc_ref, x_ref, mod_ref,
                    w_ref, g2_ref, rwh_ref, rwl_ref, rb_ref,
                    x1_ref, h2_ref, route_ref, gates_ref, *, tm, t_seq):
    i = pl.program_id(0)
    gu = gu_ref[...].astype(F32)
    rowi = lax.broadcasted_iota(jnp.int32, gu.shape, 0)
    pos = (i * tm + rowi) % t_seq
    prev = pltpu.roll(gu, 1, 0)
    prev = jnp.where(rowi == 0, gup_ref[HALO - 1:HALO, :].astype(F32), prev)
    prev = jnp.where(pos == 0, 0.0, prev)
    nxt = pltpu.roll(gu, tm - 1, 0)
    nxt = jnp.where(rowi == tm - 1, gun_ref[0:1, :].astype(F32), nxt)
    nxt = jnp.where(pos == t_seq - 1, 0.0, nxt)
    cw = cw_ref[0]
    ob = gb_ref[...].astype(F32) * (prev * cw[0:1] + gu * cw[1:2] + nxt * cw[2:3])

    mix = (_dot(oa_ref[...], w_ref[0:512, :])
           + _dot(ob.astype(BF16), w_ref[512:768, :])
           + _dot(oc_ref[...], w_ref[768:1024, :]))
    x1 = x_ref[...] + mod_ref[0, 0, 2:3, :] * mix
    x1_ref[...] = x1
    ms = jnp.mean(x1 * x1, axis=-1, keepdims=True)
    h2 = x1 * lax.rsqrt(ms + EPS) * g2_ref[...]
    h2 = h2 * (1.0 + mod_ref[0, 0, 4:5, :]) + mod_ref[0, 0, 3:4, :]
    h2_ref[...] = h2.astype(BF16)

    hh, hl = _split(h2)
    logits = _dot(hh, rwh_ref[...]) + _dot(hh, rwl_ref[...]) + _dot(hl, rwh_ref[...])
    logits_t = logits.T[0:N_EXPERTS, :]
    e0, e1, w0, w1 = _route(logits_t, rb_ref[...])
    zeros = jnp.zeros_like(w0)
    route_ref[...] = jnp.concatenate(
        [e0.astype(F32), e1.astype(F32), w0, w1, zeros, zeros, zeros, zeros], axis=0)
    erow = lax.broadcasted_iota(jnp.int32, (LANES, tm), 0)
    gates_t = jnp.where(erow == e0, w0, 0.0) + jnp.where(erow == e1, w1, 0.0)
    gates_ref[...] = gates_t.T


def _outproj(oa, gb, gu, conv_w, oc, x, mod, layer, mod_row, w_out, g2, rwh, rwl, rb, *, tm, t_seq):
    n = x.shape[0]
    row = lambda i: (i, 0)
    const2 = lambda i: (0, 0)
    hb = tm // HALO
    last = n // HALO - 1
    return pl.pallas_call(
        functools.partial(_outproj_kernel, tm=tm, t_seq=t_seq),
        grid=(n // tm,),
        in_specs=[
            pl.BlockSpec((tm, 4 * LANES), row),
            pl.BlockSpec((tm, B_WIDTH), row),
            pl.BlockSpec((tm, B_WIDTH), row),
            pl.BlockSpec((HALO, B_WIDTH), lambda i: (jnp.maximum(i * hb - 1, 0), 0)),
            pl.BlockSpec((HALO, B_WIDTH), lambda i: (jnp.minimum((i + 1) * hb, last), 0)),
            pl.BlockSpec((1, 3, B_WIDTH), lambda i: (layer, 0, 0)),
            pl.BlockSpec((tm, 2 * LANES), row),
            pl.BlockSpec((tm, D_MODEL), row),
            pl.BlockSpec((1, 1, N_MOD, D_MODEL), lambda i: (layer, mod_row(i), 0, 0)),
            pl.BlockSpec((D_MODEL, D_MODEL), const2),
            pl.BlockSpec((1, D_MODEL), const2),
            pl.BlockSpec((D_MODEL, LANES), const2),
            pl.BlockSpec((D_MODEL, LANES), const2),
            pl.BlockSpec((N_EXPERTS, 1), const2),
        ],
        out_specs=[
            pl.BlockSpec((tm, D_MODEL), row),
            pl.BlockSpec((tm, D_MODEL), row),
            pl.BlockSpec((8, tm), lambda i: (0, i)),
            pl.BlockSpec((tm, LANES), row),
        ],
        out_shape=[
            jax.ShapeDtypeStruct((n, D_MODEL), F32),
            jax.ShapeDtypeStruct((n, D_MODEL), BF16),
            jax.ShapeDtypeStruct((8, n), F32),
            jax.ShapeDtypeStruct((n, LANES), F32),
        ],
        compiler_params=_params(),
        name="outproj",
    )(oa, gb, gu, gu, gu, conv_w, oc, x, mod, w_out, g2, rwh, rwl, rb)


def _moe_kernel(h_ref, gates_ref, x1_ref, mod_ref, w1_ref, w3_ref, w2_ref, fg_ref, o_ref, acc_ref,
                *, final):
    e = pl.program_id(1)

    @pl.when(e == 0)
    def _():
        acc_ref[...] = jnp.zeros_like(acc_ref)

    h = h_ref[...]
    a = _dot(h, w1_ref[0])
    b = _dot(h, w3_ref[0])
    gates = gates_ref[...]
    lane = lax.broadcasted_iota(jnp.int32, gates.shape, 1)
    g = jnp.sum(jnp.where(lane == e, gates, 0.0), axis=-1, keepdims=True)
    hid = (a * _sigmoid(a)) * b * g
    acc_ref[...] += _dot(hid.astype(BF16), w2_ref[0])

    @pl.when(e == N_EXPERTS - 1)
    def _():
        x2 = x1_ref[...] + mod_ref[0, 0, 5:6, :] * acc_ref[...]
        if final:
            ms = jnp.mean(x2 * x2, axis=-1, keepdims=True)
            x2 = x2 * lax.rsqrt(ms + EPS) * fg_ref[...]
        o_ref[...] = x2


def _moe(h2, gates, x1, mod, layer, mod_row, w1, w3, w2, fg, *, tm, final):
    n = x1.shape[0]
    row = lambda i, e: (i, 0)
    return pl.pallas_call(
        functools.partial(_moe_kernel, final=final),
        grid=(n // tm, N_EXPERTS),
        in_specs=[
            pl.BlockSpec((tm, D_MODEL), row),
            pl.BlockSpec((tm, LANES), row),
            pl.BlockSpec((tm, D_MODEL), row),
            pl.BlockSpec((1, 1, N_MOD, D_MODEL), lambda i, e: (layer, mod_row(i), 0, 0)),
            pl.BlockSpec((1, D_MODEL, EXPERT_FF), lambda i, e: (layer * N_EXPERTS + e, 0, 0)),
            pl.BlockSpec((1, D_MODEL, EXPERT_FF), lambda i, e: (layer * N_EXPERTS + e, 0, 0)),
            pl.BlockSpec((1, EXPERT_FF, D_MODEL), lambda i, e: (layer * N_EXPERTS + e, 0, 0)),
            pl.BlockSpec((1, D_MODEL), lambda i, e: (0, 0)),
        ],
        out_specs=pl.BlockSpec((tm, D_MODEL), row),
        out_shape=jax.ShapeDtypeStruct((n, D_MODEL), F32),
        scratch_shapes=[pltpu.VMEM((tm, D_MODEL), F32)],
        compiler_params=_params(),
        name="moe",
    )(h2, gates, x1, mod, w1, w3, w2, fg)


def _rope_tables(n_tokens, rot_dim):
    pos = np.arange(n_tokens)
    r = (pos // GRID_W).astype(np.float64)[:, None]
    col = (pos % GRID_W).astype(np.float64)[:, None]
    n_freq = rot_dim // 4
    inv_freq = ROPE_THETA ** (-np.arange(n_freq, dtype=np.float64) / n_freq)
    ang = np.concatenate([r * inv_freq, col * inv_freq], axis=-1)
    ang = np.repeat(ang, 2, axis=-1)
    sign = np.tile(np.array([-1.0, 1.0]), rot_dim // 2)
    cos = np.tile(np.cos(ang), (1, LANES // rot_dim))
    sin = np.tile(np.sin(ang) * sign, (1, LANES // rot_dim))
    return jnp.asarray(cos, F32), jnp.asarray(sin, F32)


def _block_diag_ones():
    idx = np.arange(LANES) // 64
    return jnp.asarray((idx[:, None] == idx[None, :]).astype(np.float32), BF16)


def _pair_heads_by_kv_group(w, axis):
    shape = w.shape
    lead, tail = shape[:axis], shape[axis + 1:]
    w = w.reshape(lead + (A_KV_HEADS, A_HEADS // A_KV_HEADS, A_HEAD_DIM) + tail)
    w = jnp.swapaxes(w, axis, axis + 1)
    return w.reshape(shape)


def kernel(x_prompt, x_sample, cache_gqa_k, cache_gqa_v, cache_diff_k, cache_diff_v, c, c_ctx, w_in,
           w_out, norm1_g, norm2_g, ada_w, ada_b, gqa_qnorm_g, gqa_knorm_g, conv_w, diff_lambda,
           diff_subln_g, router_w, router_bias, moe_w1, moe_w3, moe_w2, final_g):
    n_ctx_b, t_ctx, _ = x_prompt.shape
    n_lat_b, t_lat, _ = x_sample.shape
    past = cache_gqa_k.shape[2]
    n_q = A_HEADS * A_HEAD_DIM

    cvec = jnp.concatenate([c_ctx[None, :], c, jnp.zeros((8 - 1 - n_lat_b, D_MODEL), F32)], axis=0)
    mod = _adaln(cvec, ada_w, ada_b).reshape(DEPTH, 8, N_MOD, D_MODEL)

    w_in_b = jnp.concatenate(
        [_pair_heads_by_kv_group(w_in[:, :, :n_q], 2), w_in[:, :, n_q:]], axis=2).astype(BF16)
    w_out_b = jnp.concatenate(
        [_pair_heads_by_kv_group(w_out[:, :n_q, :], 1), w_out[:, n_q:, :]], axis=1).astype(BF16)
    w1_b = moe_w1.astype(BF16).reshape(DEPTH * N_EXPERTS, D_MODEL, EXPERT_FF)
    w3_b = moe_w3.astype(BF16).reshape(DEPTH * N_EXPERTS, D_MODEL, EXPERT_FF)
    w2_b = moe_w2.astype(BF16).reshape(DEPTH * N_EXPERTS, EXPERT_FF, D_MODEL)
    rw = jnp.pad(router_w, ((0, 0), (0, LANES - N_EXPERTS)))
    rwh = rw.astype(BF16)
    rwl = (rw - rwh.astype(F32)).astype(BF16)
    rb = router_bias.reshape(N_EXPERTS, 1)
    bd = _block_diag_ones()
    fg = final_g.reshape(1, D_MODEL)
    tables = _rope_tables(t_lat, A_HEAD_DIM) + _rope_tables(t_lat, C_QK_DIM)

    ck_cache = cache_gqa_k.reshape(n_lat_b * DEPTH, past, A_KV_HEADS * A_HEAD_DIM)
    cv_cache = cache_gqa_v.reshape(n_lat_b * DEPTH, past, A_KV_HEADS * A_HEAD_DIM)
    dk_cache = cache_diff_k.reshape(n_lat_b * DEPTH, past, C_HEADS * 2 * C_QK_DIM)
    dv_cache = cache_diff_v.reshape(n_lat_b * DEPTH, past, C_HEADS * C_V_DIM)

    xp = x_prompt.reshape(n_ctx_b * t_ctx, D_MODEL)
    xs = x_sample.reshape(n_lat_b * t_lat, D_MODEL)
    tm = 512
    tm_moe = 1024
    tq = 256
    new_state = []
    for l in range(DEPTH):
        lam_init = 0.8 - 0.6 * math.exp(-0.3 * l)
        g1 = norm1_g[l].reshape(1, D_MODEL)
        g2 = norm2_g[l].reshape(1, D_MODEL)
        qg = jnp.tile(gqa_qnorm_g[l], 2).reshape(1, LANES)
        kg = jnp.tile(gqa_knorm_g[l], 2).reshape(1, LANES)
        sg = jnp.tile(diff_subln_g[l], 2).reshape(1, LANES)
        final = l == DEPTH - 1

        def run_stream(x, n_batch, t_seq, mod_row_of, tabs, caches, nj_gqa, nj_diff):
            ctx = caches is None
            outs = _inproj(x, mod, l, lambda i: mod_row_of(i, tm), g1, w_in_b[l], qg, kg, bd, tabs,
                           tm=tm, ctx=ctx)
            q, k, vt, gb, gu, cq, ck, cvt = outs[:8]
            c_gk, c_gv, c_dk, c_dv = caches if caches is not None else (None,) * 4
            oa = _gqa_attention(q, k, vt, c_gk, c_gv, l, n_batch=n_batch, t_seq=t_seq, tq=tq,
                                nj=nj_gqa)
            oc = _diff_attention(cq, ck, cvt, c_dk, c_dv, l, diff_lambda, sg, n_batch=n_batch,
                                 t_seq=t_seq, tq=tq, nj=nj_diff, lam_init=lam_init)
            x1, h2, _, gates = _outproj(oa, gb, gu, conv_w, oc, x, mod, l,
                                        lambda i: mod_row_of(i, tm), w_out_b[l], g2, rwh, rwl, rb,
                                        tm=tm, t_seq=t_seq)
            x2 = _moe(h2, gates, x1, mod, l, lambda i: mod_row_of(i, tm_moe), w1_b, w3_b, w2_b, fg,
                      tm=tm_moe, final=final)
            return x2, outs[8:]

        xp, state = run_stream(xp, n_ctx_b, t_ctx, lambda i, t: 0, None, None, 4, 2)
        new_state.append(state)
        xs, _ = run_stream(xs, n_lat_b, t_lat, lambda i, t: 1 + (i * t) // t_lat, tables,
                           (ck_cache, cv_cache, dk_cache, dv_cache), 1, 1)

    def stack(k, heads, dim):
        return jnp.stack([s[k].reshape(n_ctx_b, t_ctx, heads, dim) for s in new_state], axis=1)

    return (xp.reshape(n_ctx_b, t_ctx, D_MODEL), xs.reshape(n_lat_b, t_lat, D_MODEL),
            stack(0, A_KV_HEADS, A_HEAD_DIM), stack(1, A_KV_HEADS, A_HEAD_DIM),
            stack(2, C_HEADS, 2 * C_QK_DIM), stack(3, C_HEADS, C_V_DIM))
```

```python
import functools
import math

import numpy as np
import jax
import jax.numpy as jnp
from jax import lax
from jax.experimental import pallas as pl
from jax.experimental.pallas import tpu as pltpu

D_MODEL = 1024
DEPTH = 2
GRID_W = 64
A_HEADS = 8
A_KV_HEADS = 2
A_HEAD_DIM = 64
B_WIDTH = 256
C_HEADS = 4
C_QK_DIM = 32
C_V_DIM = 64
N_EXPERTS = 16
EXPERTS_PER_GROUP = 4
N_EXPERT_GROUPS = 4
EXPERT_FF = 512
ROPE_THETA = 10000.0
EPS = 1e-6
N_MOD = 6
W_IN_WIDTH = 2304
LOG2E = math.log2(math.e)

LANES = 128
HEAD_ROWS = 64
ONES_ROWS = 16
VMEM_LIMIT_BYTES = 56 * 1024 * 1024

F32 = jnp.float32
BF16 = jnp.bfloat16

_dot = functools.partial(jnp.dot, preferred_element_type=F32)


def _dot_nt(a, b):
    return lax.dot_general(a, b, (((1,), (1,)), ((), ())), preferred_element_type=F32)


def _split(a):
    hi = a.astype(BF16)
    lo = (a - hi.astype(F32)).astype(BF16)
    return hi, lo


def _dot3(a, b):
    ah, al = _split(a)
    bh, bl = _split(b)
    return _dot(ah, bh) + _dot(ah, bl) + _dot(al, bh)


def _sigmoid(x):
    return 1.0 / (1.0 + jnp.exp(-x))


def _seg_rms(x, bd, g):
    hi, lo = _split(x * x)
    ss = _dot(hi, bd) + _dot(lo, bd)
    return x * lax.rsqrt(ss * (1.0 / 64.0) + EPS) * g


def _rope(x, cos, sin_signed):
    lane = lax.broadcasted_iota(jnp.int32, x.shape, 1)
    even = (lane & 1) == 0
    partner = jnp.where(even, pltpu.roll(x, LANES - 1, 1), pltpu.roll(x, 1, 1))
    return x * cos + partner * sin_signed


def _params():
    return pltpu.CompilerParams(vmem_limit_bytes=VMEM_LIMIT_BYTES)


def _adaln_kernel(c_ref, w_ref, b_ref, o_ref):
    c = c_ref[...]
    s = c * _sigmoid(c)
    o_ref[0] = _dot3(s, w_ref[0]) + b_ref[0]


def _adaln(cvec, ada_w, ada_b):
    tn = 1536
    nt = (N_MOD * D_MODEL) // tn
    return pl.pallas_call(
        _adaln_kernel,
        grid=(DEPTH, nt),
        in_specs=[
            pl.BlockSpec((8, D_MODEL), lambda l, n: (0, 0)),
            pl.BlockSpec((1, D_MODEL, tn), lambda l, n: (l, 0, n)),
            pl.BlockSpec((1, 1, tn), lambda l, n: (l, 0, n)),
        ],
        out_specs=pl.BlockSpec((1, 8, tn), lambda l, n: (l, 0, n)),
        out_shape=jax.ShapeDtypeStruct((DEPTH, 8, N_MOD * D_MODEL), F32),
        compiler_params=_params(),
        name="adaln",
    )(cvec, ada_w, ada_b.reshape(DEPTH, 1, N_MOD * D_MODEL))


def _inproj_kernel(*refs, rope, ctx, kg):
    it = iter(refs)
    x_ref, mod_ref, g1_ref, w_ref, qg_ref, kg_ref, bd_ref = (next(it) for _ in range(7))
    if rope:
        cos_a, sin_a, cos_c, sin_c = (next(it) for _ in range(4))
    q_ref, k_ref, vt_ref, gb_ref, gu_ref, cq_ref, ck_ref, cvt_ref = (next(it) for _ in range(8))
    if ctx:
        k32_ref, v32_ref, ck32_ref, cv32_ref = (next(it) for _ in range(4))

    x = x_ref[...]
    ms = jnp.mean(x * x, axis=-1, keepdims=True)
    y = x * lax.rsqrt(ms + EPS) * g1_ref[...]
    h = y * (1.0 + mod_ref[0, 0, 1:2, :]) + mod_ref[0, 0, 0:1, :]
    hb = h.astype(BF16)
    bd = bd_ref[...]

    def seg(lo, width):
        return _dot(hb, w_ref[:, lo:lo + width])

    def store_transposed(dst, a):
        at = a.T.astype(BF16)
        for c in range(a.shape[0] // kg):
            dst[c] = at[:, kg * c:kg * (c + 1)]

    for j in range(4):
        q = _seg_rms(seg(LANES * j, LANES), bd, qg_ref[...])
        if rope:
            q = _rope(q, cos_a[...], sin_a[...])
        q_ref[j] = (q * (A_HEAD_DIM ** -0.5 * LOG2E)).astype(BF16)

    k = _seg_rms(seg(512, LANES), bd, kg_ref[...])
    if rope:
        k = _rope(k, cos_a[...], sin_a[...])
    v = seg(640, LANES)
    if ctx:
        k32_ref[...] = k
        v32_ref[...] = v
    k_ref[0] = k.astype(BF16)
    store_transposed(vt_ref.at[0], v)

    gb_ref[...] = seg(768, B_WIDTH).astype(BF16)
    gu_ref[...] = (seg(1024, B_WIDTH) * seg(1280, B_WIDTH)).astype(BF16)

    for j in range(2):
        cq = seg(1536 + LANES * j, LANES)
        ck = seg(1792 + LANES * j, LANES)
        cv = seg(2048 + LANES * j, LANES)
        if ctx:
            ck32_ref[:, LANES * j:LANES * (j + 1)] = ck
            cv32_ref[:, LANES * j:LANES * (j + 1)] = cv
        if rope:
            cq = _rope(cq, cos_c[...], sin_c[...])
            ck = _rope(ck, cos_c[...], sin_c[...])
        cq_ref[j] = (cq * (C_QK_DIM ** -0.5 * LOG2E)).astype(BF16)
        ck_ref[j] = ck.astype(BF16)
        store_transposed(cvt_ref.at[j], cv)


def _inproj(x, mod, layer, mod_row, g1, w_in, qg, kg, bd, tables, *, tm, ctx, key_group):
    n = x.shape[0]
    rope = tables is not None
    row = lambda i: (i, 0)
    const2 = lambda i: (0, 0)
    pair3 = lambda i: (0, i, 0)
    tile4 = lambda i: (0, i, 0, 0)
    nc = tm // key_group
    in_specs = [
        pl.BlockSpec((tm, D_MODEL), row),
        pl.BlockSpec((1, 1, N_MOD, D_MODEL), lambda i: (layer, mod_row(i), 0, 0)),
        pl.BlockSpec((1, D_MODEL), const2),
        pl.BlockSpec((D_MODEL, W_IN_WIDTH), const2),
        pl.BlockSpec((1, LANES), const2),
        pl.BlockSpec((1, LANES), const2),
        pl.BlockSpec((LANES, LANES), const2),
    ]
    args = [x, mod, g1, w_in, qg, kg, bd]
    if rope:
        t = tables[0].shape[0]
        tab = lambda i: (i % (t // tm), 0)
        in_specs += [pl.BlockSpec((tm, LANES), tab)] * 4
        args += list(tables)
    out_specs = [
        pl.BlockSpec((4, tm, LANES), pair3),
        pl.BlockSpec((1, tm, LANES), pair3),
        pl.BlockSpec((1, nc, LANES, key_group), tile4),
        pl.BlockSpec((tm, B_WIDTH), row),
        pl.BlockSpec((tm, B_WIDTH), row),
        pl.BlockSpec((2, tm, LANES), pair3),
        pl.BlockSpec((2, tm, LANES), pair3),
        pl.BlockSpec((2, nc, LANES, key_group), tile4),
    ]
    out_shape = [
        jax.ShapeDtypeStruct((4, n, LANES), BF16),
        jax.ShapeDtypeStruct((1, n, LANES), BF16),
        jax.ShapeDtypeStruct((1, n // key_group, LANES, key_group), BF16),
        jax.ShapeDtypeStruct((n, B_WIDTH), BF16),
        jax.ShapeDtypeStruct((n, B_WIDTH), BF16),
        jax.ShapeDtypeStruct((2, n, LANES), BF16),
        jax.ShapeDtypeStruct((2, n, LANES), BF16),
        jax.ShapeDtypeStruct((2, n // key_group, LANES, key_group), BF16),
    ]
    if ctx:
        out_specs += [
            pl.BlockSpec((tm, LANES), row),
            pl.BlockSpec((tm, LANES), row),
            pl.BlockSpec((tm, 2 * LANES), row),
            pl.BlockSpec((tm, 2 * LANES), row),
        ]
        out_shape += [
            jax.ShapeDtypeStruct((n, LANES), F32),
            jax.ShapeDtypeStruct((n, LANES), F32),
            jax.ShapeDtypeStruct((n, 2 * LANES), F32),
            jax.ShapeDtypeStruct((n, 2 * LANES), F32),
        ]
    return pl.pallas_call(
        functools.partial(_inproj_kernel, rope=rope, ctx=ctx, kg=key_group),
        grid=(n // tm,),
        in_specs=in_specs,
        out_specs=out_specs,
        out_shape=out_shape,
        compiler_params=_params(),
        name="inproj_ctx" if ctx else "inproj_lat",
    )(*args)


def _attend(q_masks, halves, k_ref, kj, vt_ref, vj, cache):
    tq = q_masks[0].shape[0]
    nu = len(q_masks)
    kg = vt_ref.shape[-1]
    n_groups = k_ref.shape[1] // kg
    loaders = []
    if cache is not None:
        kc, vc = cache
        loaders.append(lambda: (kc.astype(BF16), vc.T.astype(BF16)))
    for g in range(n_groups):
        loaders.append(lambda g=g: (k_ref[kj, kg * g:kg * (g + 1), :], vt_ref[vj, g]))

    def value_rows(vt, u):
        own = vt[HEAD_ROWS * halves[u]:HEAD_ROWS * (halves[u] + 1), :]
        return jnp.concatenate([own, jnp.ones((ONES_ROWS, vt.shape[1]), BF16)], axis=0)

    ms = [jnp.full((1, tq), -jnp.inf, F32)] * nu
    accs = [jnp.zeros((HEAD_ROWS + ONES_ROWS, tq), F32)] * nu
    k, vt = loaders[0]()
    scores = [_dot_nt(k, q_masks[u]) for u in range(nu)]
    for g in range(len(loaders)):
        cur_scores, cur_vt = scores, vt
        if g + 1 < len(loaders):
            k, vt = loaders[g + 1]()
            scores = [_dot_nt(k, q_masks[u]) for u in range(nu)]
        new_ms, ps = [], []
        for u in range(nu):
            m = jnp.maximum(ms[u], jnp.max(cur_scores[u], axis=0, keepdims=True))
            new_ms.append(m)
            ps.append(jnp.exp2(cur_scores[u] - m).astype(BF16))
        pvs = [_dot(value_rows(cur_vt, u), ps[u]) for u in range(nu)]
        accs = [jnp.exp2(ms[u] - new_ms[u]) * accs[u] + pvs[u] for u in range(nu)]
        ms = new_ms
    return [acc[0:HEAD_ROWS] / acc[HEAD_ROWS:HEAD_ROWS + 1] for acc in accs]


def _lane_mask(q, lo, width):
    lane = lax.broadcasted_iota(jnp.int32, q.shape, 1)
    return jnp.where((lane >= lo) & (lane < lo + width), q, jnp.zeros_like(q))


def _gqa_kernel(*refs, has_cache, nj):
    if has_cache:
        q_ref, k_ref, vt_ref, ck_ref, cv_ref, o_ref = refs
        cache = (ck_ref[0], cv_ref[0])
    else:
        q_ref, k_ref, vt_ref, o_ref = refs
        cache = None
    for j in range(nj):
        q = q_ref[j]
        q_masks = [_lane_mask(q, 0, 64), _lane_mask(q, 64, 64)]
        o0, o1 = _attend(q_masks, (0, 1), k_ref, 0, vt_ref, 0, cache)
        o_ref[:, LANES * j:LANES * (j + 1)] = jnp.concatenate([o0, o1], axis=0).T.astype(BF16)


def _gqa_attention(q, k, vt, cache_k, cache_v, layer, *, n_batch, t_seq, tq, nj):
    n = q.shape[1]
    nq = t_seq // tq
    kg = vt.shape[-1]
    has_cache = cache_k is not None
    in_specs = [
        pl.BlockSpec((nj, tq, LANES), lambda b, i, j: (j, b * nq + i, 0)),
        pl.BlockSpec((1, t_seq, LANES), lambda b, i, j: (0, b, 0)),
        pl.BlockSpec((1, t_seq // kg, LANES, kg), lambda b, i, j: (0, b, 0, 0)),
    ]
    args = [q, k, vt]
    if has_cache:
        past = cache_k.shape[1]
        cspec = pl.BlockSpec((1, past, LANES), lambda b, i, j: (b * DEPTH + layer, 0, 0))
        in_specs += [cspec, cspec]
        args += [cache_k, cache_v]
    return pl.pallas_call(
        functools.partial(_gqa_kernel, has_cache=has_cache, nj=nj),
        grid=(n_batch, nq, 4 // nj),
        in_specs=in_specs,
        out_specs=pl.BlockSpec((tq, LANES * nj), lambda b, i, j: (b * nq + i, j)),
        out_shape=jax.ShapeDtypeStruct((n, 4 * LANES), BF16),
        compiler_params=_params(),
        name="gqa_lat" if has_cache else "gqa_ctx",
    )(*args)


def _diff_kernel(*refs, has_cache, nj, lam_init):
    if has_cache:
        q_ref, k_ref, vt_ref, ck_ref, cv_ref, lam_ref, g_ref, o_ref = refs
    else:
        q_ref, k_ref, vt_ref, lam_ref, g_ref, o_ref = refs
    lv = lam_ref[0]
    lam = (jnp.exp(jnp.sum(lv[0:1] * lv[1:2], axis=-1, keepdims=True))
           - jnp.exp(jnp.sum(lv[2:3] * lv[3:4], axis=-1, keepdims=True)) + lam_init)
    for j in range(nj):
        q = q_ref[j]
        q_masks = [_lane_mask(q, 32 * c, 32) for c in range(4)]
        cache = None
        if has_cache:
            cache = (ck_ref[0, :, LANES * j:LANES * (j + 1)], cv_ref[0, :, LANES * j:LANES * (j + 1)])
        res = _attend(q_masks, (0, 0, 1, 1), k_ref, j, vt_ref, j, cache)
        heads = []
        for i in range(2):
            o = res[2 * i] - lam * res[2 * i + 1]
            ms = jnp.mean(o * o, axis=0, keepdims=True)
            heads.append(o * lax.rsqrt(ms + EPS))
        o = jnp.concatenate(heads, axis=0).T * g_ref[...] * (1.0 - lam_init)
        o_ref[:, LANES * j:LANES * (j + 1)] = o.astype(BF16)


def _diff_attention(cq, ck, cvt, cache_k, cache_v, layer, diff_lambda, subln_g, *,
                    n_batch, t_seq, tq, nj, lam_init):
    n = cq.shape[1]
    nq = t_seq // tq
    kg = cvt.shape[-1]
    has_cache = cache_k is not None
    in_specs = [
        pl.BlockSpec((nj, tq, LANES), lambda b, i, j: (j, b * nq + i, 0)),
        pl.BlockSpec((nj, t_seq, LANES), lambda b, i, j: (j, b, 0)),
        pl.BlockSpec((nj, t_seq // kg, LANES, kg), lambda b, i, j: (j, b, 0, 0)),
    ]
    args = [cq, ck, cvt]
    if has_cache:
        past = cache_k.shape[1]
        cspec = pl.BlockSpec((1, past, LANES * nj), lambda b, i, j: (b * DEPTH + layer, 0, j))
        in_specs += [cspec, cspec]
        args += [cache_k, cache_v]
    in_specs += [
        pl.BlockSpec((1, 4, C_QK_DIM), lambda b, i, j: (layer, 0, 0)),
        pl.BlockSpec((1, LANES), lambda b, i, j: (0, 0)),
    ]
    args += [diff_lambda, subln_g]
    return pl.pallas_call(
        functools.partial(_diff_kernel, has_cache=has_cache, nj=nj, lam_init=lam_init),
        grid=(n_batch, nq, 2 // nj),
        in_specs=in_specs,
        out_specs=pl.BlockSpec((tq, LANES * nj), lambda b, i, j: (b * nq + i, j)),
        out_shape=jax.ShapeDtypeStruct((n, 2 * LANES), BF16),
        compiler_params=_params(),
        name="diff_lat" if has_cache else "diff_ctx",
    )(*args)


HALO = 16


def _route(logits_t, bias_col):
    scores = _sigmoid(logits_t)
    sel = scores + bias_col
    row = lambda a, e: a[e:e + 1, :]
    grp = []
    for g in range(N_EXPERT_GROUPS):
        s0, s1, s2, s3 = (row(sel, 4 * g + k) for k in range(4))
        hi1, lo1 = jnp.maximum(s0, s1), jnp.minimum(s0, s1)
        hi2, lo2 = jnp.maximum(s2, s3), jnp.minimum(s2, s3)
        grp.append(jnp.maximum(hi1, hi2) + jnp.maximum(jnp.minimum(hi1, hi2), jnp.maximum(lo1, lo2)))
    best = grp[0]
    tg = jnp.zeros_like(best, dtype=jnp.int32)
    for g in range(1, N_EXPERT_GROUPS):
        better = grp[g] > best
        best = jnp.where(better, grp[g], best)
        tg = jnp.where(better, g, tg)

    def in_group(a, k):
        out = row(a, k)
        for g in range(1, N_EXPERT_GROUPS):
            out = jnp.where(tg == g, row(a, 4 * g + k), out)
        return out

    vals = [in_group(sel, k) for k in range(EXPERTS_PER_GROUP)]
    scs = [in_group(scores, k) for k in range(EXPERTS_PER_GROUP)]

    def first_argmax(vs):
        bv, bi = vs[0], jnp.zeros_like(tg)
        for k in range(1, len(vs)):
            better = vs[k] > bv
            bv = jnp.where(better, vs[k], bv)
            bi = jnp.where(better, k, bi)
        return bi

    i1 = first_argmax(vals)
    i2 = first_argmax([jnp.where(i1 == k, -jnp.inf, vals[k]) for k in range(EXPERTS_PER_GROUP)])

    def pick(vs, idx):
        out = vs[0]
        for k in range(1, len(vs)):
            out = jnp.where(idx == k, vs[k], out)
        return out

    w1, w2 = pick(scs, i1), pick(scs, i2)
    tot = w1 + w2
    return 4 * tg + i1, 4 * tg + i2, w1 / tot, w2 / tot


def _outproj_kernel(oa_ref, gb_ref, gu_ref, gup_ref, gun_ref, cw_ref, oc_ref, x_ref, mod_ref,
                    w_ref, g2_ref, rwh_ref, rwl_ref, rb_ref,
                    x1_ref, h2_ref, route_ref, gates_ref, *, tm, t_seq):
    i = pl.program_id(0)
    gu = gu_ref[...].astype(F32)
    rowi = lax.broadcasted_iota(jnp.int32, gu.shape, 0)
    pos = (i * tm + rowi) % t_seq
    prev = pltpu.roll(gu, 1, 0)
    prev = jnp.where(rowi == 0, gup_ref[HALO - 1:HALO, :].astype(F32), prev)
    prev = jnp.where(pos == 0, 0.0, prev)
    nxt = pltpu.roll(gu, tm - 1, 0)
    nxt = jnp.where(rowi == tm - 1, gun_ref[0:1, :].astype(F32), nxt)
    nxt = jnp.where(pos == t_seq - 1, 0.0, nxt)
    cw = cw_ref[0]
    ob = gb_ref[...].astype(F32) * (prev * cw[0:1] + gu * cw[1:2] + nxt * cw[2:3])

    mix = (_dot(oa_ref[...], w_ref[0:512, :])
           + _dot(ob.astype(BF16), w_ref[512:768, :])
           + _dot(oc_ref[...], w_ref[768:1024, :]))
    x1 = x_ref[...] + mod_ref[0, 0, 2:3, :] * mix
    x1_ref[...] = x1
    ms = jnp.mean(x1 * x1, axis=-1, keepdims=True)
    h2 = x1 * lax.rsqrt(ms + EPS) * g2_ref[...]
    h2 = h2 * (1.0 + mod_ref[0, 0, 4:5, :]) + mod_ref[0, 0, 3:4, :]
    h2_ref[...] = h2.astype(BF16)

    hh, hl = _split(h2)
    logits = _dot(hh, rwh_ref[...]) + _dot(hh, rwl_ref[...]) + _dot(hl, rwh_ref[...])
    logits_t = logits.T[0:N_EXPERTS, :]
    e0, e1, w0, w1 = _route(logits_t, rb_ref[...])
    zeros = jnp.zeros_like(w0)
    route_ref[...] = jnp.concatenate(
        [e0.astype(F32), e1.astype(F32), w0, w1, zeros, zeros, zeros, zeros], axis=0)
    erow = lax.broadcasted_iota(jnp.int32, (LANES, tm), 0)
    gates_t = jnp.where(erow == e0, w0, 0.0) + jnp.where(erow == e1, w1, 0.0)
    gates_ref[...] = gates_t.T


def _outproj(oa, gb, gu, conv_w, oc, x, mod, layer, mod_row, w_out, g2, rwh, rwl, rb, *, tm, t_seq):
    n = x.shape[0]
    row = lambda i: (i, 0)
    const2 = lambda i: (0, 0)
    hb = tm // HALO
    last = n // HALO - 1
    return pl.pallas_call(
        functools.partial(_outproj_kernel, tm=tm, t_seq=t_seq),
        grid=(n // tm,),
        in_specs=[
            pl.BlockSpec((tm, 4 * LANES), row),
            pl.BlockSpec((tm, B_WIDTH), row),
            pl.BlockSpec((tm, B_WIDTH), row),
            pl.BlockSpec((HALO, B_WIDTH), lambda i: (jnp.maximum(i * hb - 1, 0), 0)),
            pl.BlockSpec((HALO, B_WIDTH), lambda i: (jnp.minimum((i + 1) * hb, last), 0)),
            pl.BlockSpec((1, 3, B_WIDTH), lambda i: (layer, 0, 0)),
            pl.BlockSpec((tm, 2 * LANES), row),
            pl.BlockSpec((tm, D_MODEL), row),
            pl.BlockSpec((1, 1, N_MOD, D_MODEL), lambda i: (layer, mod_row(i), 0, 0)),
            pl.BlockSpec((D_MODEL, D_MODEL), const2),
            pl.BlockSpec((1, D_MODEL), const2),
            pl.BlockSpec((D_MODEL, LANES), const2),
            pl.BlockSpec((D_MODEL, LANES), const2),
            pl.BlockSpec((N_EXPERTS, 1), const2),
        ],
        out_specs=[
            pl.BlockSpec((tm, D_MODEL), row),
            pl.BlockSpec((tm, D_MODEL), row),
            pl.BlockSpec((8, tm), lambda i: (0, i)),
            pl.BlockSpec((tm, LANES), row),
        ],
        out_shape=[
            jax.ShapeDtypeStruct((n, D_MODEL), F32),
            jax.ShapeDtypeStruct((n, D_MODEL), BF16),
            jax.ShapeDtypeStruct((8, n), F32),
            jax.ShapeDtypeStruct((n, LANES), F32),
        ],
        compiler_params=_params(),
        name="outproj",
    )(oa, gb, gu, gu, gu, conv_w, oc, x, mod, w_out, g2, rwh, rwl, rb)


def _moe_kernel(h_ref, gates_ref, x1_ref, mod_ref, w1_ref, w3_ref, w2_ref, fg_ref, o_ref, acc_ref,
                *, final):
    e = pl.program_id(1)

    @pl.when(e == 0)
    def _():
        acc_ref[...] = jnp.zeros_like(acc_ref)

    h = h_ref[...]
    a = _dot(h, w1_ref[0])
    b = _dot(h, w3_ref[0])
    gates = gates_ref[...]
    lane = lax.broadcasted_iota(jnp.int32, gates.shape, 1)
    g = jnp.sum(jnp.where(lane == e, gates, 0.0), axis=-1, keepdims=True)
    hid = (a * _sigmoid(a)) * b * g
    acc_ref[...] += _dot(hid.astype(BF16), w2_ref[0])

    @pl.when(e == N_EXPERTS - 1)
    def _():
        x2 = x1_ref[...] + mod_ref[0, 0, 5:6, :] * acc_ref[...]
        if final:
            ms = jnp.mean(x2 * x2, axis=-1, keepdims=True)
            x2 = x2 * lax.rsqrt(ms + EPS) * fg_ref[...]
        o_ref[...] = x2


def _moe(h2, gates, x1, mod, layer, mod_row, w1, w3, w2, fg, *, tm, final):
    n = x1.shape[0]
    row = lambda i, e: (i, 0)
    return pl.pallas_call(
        functools.partial(_moe_kernel, final=final),
        grid=(n // tm, N_EXPERTS),
        in_specs=[
            pl.BlockSpec((tm, D_MODEL), row),
            pl.BlockSpec((tm, LANES), row),
            pl.BlockSpec((tm, D_MODEL), row),
            pl.BlockSpec((1, 1, N_MOD, D_MODEL), lambda i, e: (layer, mod_row(i), 0, 0)),
            pl.BlockSpec((1, D_MODEL, EXPERT_FF), lambda i, e: (layer * N_EXPERTS + e, 0, 0)),
            pl.BlockSpec((1, D_MODEL, EXPERT_FF), lambda i, e: (layer * N_EXPERTS + e, 0, 0)),
            pl.BlockSpec((1, EXPERT_FF, D_MODEL), lambda i, e: (layer * N_EXPERTS + e, 0, 0)),
            pl.BlockSpec((1, D_MODEL), lambda i, e: (0, 0)),
        ],
        out_specs=pl.BlockSpec((tm, D_MODEL), row),
        out_shape=jax.ShapeDtypeStruct((n, D_MODEL), F32),
        scratch_shapes=[pltpu.VMEM((tm, D_MODEL), F32)],
        compiler_params=_params(),
        name="moe",
    )(h2, gates, x1, mod, w1, w3, w2, fg)


def _rope_tables(n_tokens, rot_dim):
    pos = np.arange(n_tokens)
    r = (pos // GRID_W).astype(np.float64)[:, None]
    col = (pos % GRID_W).astype(np.float64)[:, None]
    n_freq = rot_dim // 4
    inv_freq = ROPE_THETA ** (-np.arange(n_freq, dtype=np.float64) / n_freq)
    ang = np.concatenate([r * inv_freq, col * inv_freq], axis=-1)
    ang = np.repeat(ang, 2, axis=-1)
    sign = np.tile(np.array([-1.0, 1.0]), rot_dim // 2)
    cos = np.tile(np.cos(ang), (1, LANES // rot_dim))
    sin = np.tile(np.sin(ang) * sign, (1, LANES // rot_dim))
    return jnp.asarray(cos, F32), jnp.asarray(sin, F32)


def _block_diag_ones():
    idx = np.arange(LANES) // 64
    return jnp.asarray((idx[:, None] == idx[None, :]).astype(np.float32), BF16)


def _pair_heads_by_kv_group(w, axis):
    shape = w.shape
    lead, tail = shape[:axis], shape[axis + 1:]
    w = w.reshape(lead + (A_KV_HEADS, A_HEADS // A_KV_HEADS, A_HEAD_DIM) + tail)
    w = jnp.swapaxes(w, axis, axis + 1)
    return w.reshape(shape)


def kernel(x_prompt, x_sample, cache_gqa_k, cache_gqa_v, cache_diff_k, cache_diff_v, c, c_ctx, w_in,
           w_out, norm1_g, norm2_g, ada_w, ada_b, gqa_qnorm_g, gqa_knorm_g, conv_w, diff_lambda,
           diff_subln_g, router_w, router_bias, moe_w1, moe_w3, moe_w2, final_g):
    n_ctx_b, t_ctx, _ = x_prompt.shape
    n_lat_b, t_lat, _ = x_sample.shape
    past = cache_gqa_k.shape[2]
    n_q = A_HEADS * A_HEAD_DIM

    cvec = jnp.concatenate([c_ctx[None, :], c, jnp.zeros((8 - 1 - n_lat_b, D_MODEL), F32)], axis=0)
    mod = _adaln(cvec, ada_w, ada_b).reshape(DEPTH, 8, N_MOD, D_MODEL)

    w_in_b = jnp.concatenate(
        [_pair_heads_by_kv_group(w_in[:, :, :n_q], 2), w_in[:, :, n_q:]], axis=2).astype(BF16)
    w_out_b = jnp.concatenate(
        [_pair_heads_by_kv_group(w_out[:, :n_q, :], 1), w_out[:, n_q:, :]], axis=1).astype(BF16)
    w1_b = moe_w1.astype(BF16).reshape(DEPTH * N_EXPERTS, D_MODEL, EXPERT_FF)
    w3_b = moe_w3.astype(BF16).reshape(DEPTH * N_EXPERTS, D_MODEL, EXPERT_FF)
    w2_b = moe_w2.astype(BF16).reshape(DEPTH * N_EXPERTS, EXPERT_FF, D_MODEL)
    rw = jnp.pad(router_w, ((0, 0), (0, LANES - N_EXPERTS)))
    rwh = rw.astype(BF16)
    rwl = (rw - rwh.astype(F32)).astype(BF16)
    rb = router_bias.reshape(N_EXPERTS, 1)
    bd = _block_diag_ones()
    fg = final_g.reshape(1, D_MODEL)
    tables = _rope_tables(t_lat, A_HEAD_DIM) + _rope_tables(t_lat, C_QK_DIM)

    ck_cache = cache_gqa_k.reshape(n_lat_b * DEPTH, past, A_KV_HEADS * A_HEAD_DIM)
    cv_cache = cache_gqa_v.reshape(n_lat_b * DEPTH, past, A_KV_HEADS * A_HEAD_DIM)
    dk_cache = cache_diff_k.reshape(n_lat_b * DEPTH, past, C_HEADS * 2 * C_QK_DIM)
    dv_cache = cache_diff_v.reshape(n_lat_b * DEPTH, past, C_HEADS * C_V_DIM)

    xp = x_prompt.reshape(n_ctx_b * t_ctx, D_MODEL)
    xs = x_sample.reshape(n_lat_b * t_lat, D_MODEL)
    tm = 512
    tm_moe = 1024
    tq = 256
    new_state = []
    for l in range(DEPTH):
        lam_init = 0.8 - 0.6 * math.exp(-0.3 * l)
        g1 = norm1_g[l].reshape(1, D_MODEL)
        g2 = norm2_g[l].reshape(1, D_MODEL)
        qg = jnp.tile(gqa_qnorm_g[l], 2).reshape(1, LANES)
        kg = jnp.tile(gqa_knorm_g[l], 2).reshape(1, LANES)
        sg = jnp.tile(diff_subln_g[l], 2).reshape(1, LANES)
        final = l == DEPTH - 1

        def run_stream(x, n_batch, t_seq, mod_row_of, tabs, caches, nj_gqa, nj_diff, key_group):
            ctx = caches is None
            outs = _inproj(x, mod, l, lambda i: mod_row_of(i, tm), g1, w_in_b[l], qg, kg, bd, tabs,
                           tm=tm, ctx=ctx, key_group=key_group)
            q, k, vt, gb, gu, cq, ck, cvt = outs[:8]
            c_gk, c_gv, c_dk, c_dv = caches if caches is not None else (None,) * 4
            oa = _gqa_attention(q, k, vt, c_gk, c_gv, l, n_batch=n_batch, t_seq=t_seq, tq=tq,
                                nj=nj_gqa)
            oc = _diff_attention(cq, ck, cvt, c_dk, c_dv, l, diff_lambda, sg, n_batch=n_batch,
                                 t_seq=t_seq, tq=tq, nj=nj_diff, lam_init=lam_init)
            x1, h2, _, gates = _outproj(oa, gb, gu, conv_w, oc, x, mod, l,
                                        lambda i: mod_row_of(i, tm), w_out_b[l], g2, rwh, rwl, rb,
                                        tm=tm, t_seq=t_seq)
            x2 = _moe(h2, gates, x1, mod, l, lambda i: mod_row_of(i, tm_moe), w1_b, w3_b, w2_b, fg,
                      tm=tm_moe, final=final)
            return x2, outs[8:]

        xp, state = run_stream(xp, n_ctx_b, t_ctx, lambda i, t: 0, None, None, 4, 2, t_ctx)
        new_state.append(state)
        xs, _ = run_stream(xs, n_lat_b, t_lat, lambda i, t: 1 + (i * t) // t_lat, tables,
                           (ck_cache, cv_cache, dk_cache, dv_cache), 1, 1, 512)

    def stack(k, heads, dim):
        return jnp.stack([s[k].reshape(n_ctx_b, t_ctx, heads, dim) for s in new_state], axis=1)

    return (xp.reshape(n_ctx_b, t_ctx, D_MODEL), xs.reshape(n_lat_b, t_lat, D_MODEL),
            stack(0, A_KV_HEADS, A_HEAD_DIM), stack(1, A_KV_HEADS, A_HEAD_DIM),
            stack(2, C_HEADS, 2 * C_QK_DIM), stack(3, C_HEADS, C_V_DIM))
```

```python
import functools
import math

import numpy as np
import jax
import jax.numpy as jnp
from jax import lax
from jax.experimental import pallas as pl
from jax.experimental.pallas import tpu as pltpu

D_MODEL = 1024
DEPTH = 2
GRID_W = 64
A_HEADS = 8
A_KV_HEADS = 2
A_HEAD_DIM = 64
B_WIDTH = 256
C_HEADS = 4
C_QK_DIM = 32
C_V_DIM = 64
N_EXPERTS = 16
EXPERTS_PER_GROUP = 4
N_EXPERT_GROUPS = 4
EXPERT_FF = 512
ROPE_THETA = 10000.0
EPS = 1e-6
N_MOD = 6
W_IN_WIDTH = 2304
LOG2E = math.log2(math.e)

LANES = 128
HEAD_ROWS = 64
ONES_ROWS = 16
VMEM_LIMIT_BYTES = 56 * 1024 * 1024

F32 = jnp.float32
BF16 = jnp.bfloat16

_dot = functools.partial(jnp.dot, preferred_element_type=F32)


def _dot_nt(a, b):
    return lax.dot_general(a, b, (((1,), (1,)), ((), ())), preferred_element_type=F32)


def _split(a):
    hi = a.astype(BF16)
    lo = (a - hi.astype(F32)).astype(BF16)
    return hi, lo


def _dot3(a, b):
    ah, al = _split(a)
    bh, bl = _split(b)
    return _dot(ah, bh) + _dot(ah, bl) + _dot(al, bh)


def _sigmoid(x):
    return 1.0 / (1.0 + jnp.exp(-x))


def _seg_rms(x, bd, g):
    hi, lo = _split(x * x)
    ss = _dot(hi, bd) + _dot(lo, bd)
    return x * lax.rsqrt(ss * (1.0 / 64.0) + EPS) * g


def _rope(x, cos, sin_signed):
    lane = lax.broadcasted_iota(jnp.int32, x.shape, 1)
    even = (lane & 1) == 0
    partner = jnp.where(even, pltpu.roll(x, LANES - 1, 1), pltpu.roll(x, 1, 1))
    return x * cos + partner * sin_signed


def _params():
    return pltpu.CompilerParams(vmem_limit_bytes=VMEM_LIMIT_BYTES)


def _adaln_kernel(c_ref, w_ref, b_ref, o_ref):
    c = c_ref[...]
    s = c * _sigmoid(c)
    o_ref[0] = _dot3(s, w_ref[0]) + b_ref[0]


def _adaln(cvec, ada_w, ada_b):
    tn = 1536
    nt = (N_MOD * D_MODEL) // tn
    return pl.pallas_call(
        _adaln_kernel,
        grid=(DEPTH, nt),
        in_specs=[
            pl.BlockSpec((8, D_MODEL), lambda l, n: (0, 0)),
            pl.BlockSpec((1, D_MODEL, tn), lambda l, n: (l, 0, n)),
            pl.BlockSpec((1, 1, tn), lambda l, n: (l, 0, n)),
        ],
        out_specs=pl.BlockSpec((1, 8, tn), lambda l, n: (l, 0, n)),
        out_shape=jax.ShapeDtypeStruct((DEPTH, 8, N_MOD * D_MODEL), F32),
        compiler_params=_params(),
        name="adaln",
    )(cvec, ada_w, ada_b.reshape(DEPTH, 1, N_MOD * D_MODEL))


def _inproj_kernel(*refs, rope, ctx, kg):
    it = iter(refs)
    x_ref, mod_ref, g1_ref, w_ref, qg_ref, kg_ref, bd_ref = (next(it) for _ in range(7))
    if rope:
        cos_a, sin_a, cos_c, sin_c = (next(it) for _ in range(4))
    q_ref, k_ref, vt_ref, gb_ref, gu_ref, cq_ref, ck_ref, cvt_ref = (next(it) for _ in range(8))
    if ctx:
        k32_ref, v32_ref, ck32_ref, cv32_ref = (next(it) for _ in range(4))

    x = x_ref[...]
    ms = jnp.mean(x * x, axis=-1, keepdims=True)
    y = x * lax.rsqrt(ms + EPS) * g1_ref[...]
    h = y * (1.0 + mod_ref[0, 0, 1:2, :]) + mod_ref[0, 0, 0:1, :]
    hb = h.astype(BF16)
    bd = bd_ref[...]

    def seg(lo, width):
        return _dot(hb, w_ref[:, lo:lo + width])

    def store_transposed(dst, a):
        at = a.T.astype(BF16)
        for c in range(a.shape[0] // kg):
            dst[c] = at[:, kg * c:kg * (c + 1)]

    for j in range(4):
        q = _seg_rms(seg(LANES * j, LANES), bd, qg_ref[...])
        if rope:
            q = _rope(q, cos_a[...], sin_a[...])
        q_ref[j] = (q * (A_HEAD_DIM ** -0.5 * LOG2E)).astype(BF16)

    k = _seg_rms(seg(512, LANES), bd, kg_ref[...])
    if rope:
        k = _rope(k, cos_a[...], sin_a[...])
    v = seg(640, LANES)
    if ctx:
        k32_ref[...] = k
        v32_ref[...] = v
    k_ref[0] = k.astype(BF16)
    store_transposed(vt_ref.at[0], v)

    gb_ref[...] = seg(768, B_WIDTH).astype(BF16)
    gu_ref[...] = (seg(1024, B_WIDTH) * seg(1280, B_WIDTH)).astype(BF16)

    for j in range(2):
        cq = seg(1536 + LANES * j, LANES)
        ck = seg(1792 + LANES * j, LANES)
        cv = seg(2048 + LANES * j, LANES)
        if ctx:
            ck32_ref[:, LANES * j:LANES * (j + 1)] = ck
            cv32_ref[:, LANES * j:LANES * (j + 1)] = cv
        if rope:
            cq = _rope(cq, cos_c[...], sin_c[...])
            ck = _rope(ck, cos_c[...], sin_c[...])
        cq_ref[j] = (cq * (C_QK_DIM ** -0.5 * LOG2E)).astype(BF16)
        ck_ref[j] = ck.astype(BF16)
        store_transposed(cvt_ref.at[j], cv)


def _inproj(x, mod, layer, mod_row, g1, w_in, qg, kg, bd, tables, *, tm, ctx, key_group):
    n = x.shape[0]
    rope = tables is not None
    row = lambda i: (i, 0)
    const2 = lambda i: (0, 0)
    pair3 = lambda i: (0, i, 0)
    tile4 = lambda i: (0, i, 0, 0)
    nc = tm // key_group
    in_specs = [
        pl.BlockSpec((tm, D_MODEL), row),
        pl.BlockSpec((1, 1, N_MOD, D_MODEL), lambda i: (layer, mod_row(i), 0, 0)),
        pl.BlockSpec((1, D_MODEL), const2),
        pl.BlockSpec((D_MODEL, W_IN_WIDTH), const2),
        pl.BlockSpec((1, LANES), const2),
        pl.BlockSpec((1, LANES), const2),
        pl.BlockSpec((LANES, LANES), const2),
    ]
    args = [x, mod, g1, w_in, qg, kg, bd]
    if rope:
        t = tables[0].shape[0]
        tab = lambda i: (i % (t // tm), 0)
        in_specs += [pl.BlockSpec((tm, LANES), tab)] * 4
        args += list(tables)
    out_specs = [
        pl.BlockSpec((4, tm, LANES), pair3),
        pl.BlockSpec((1, tm, LANES), pair3),
        pl.BlockSpec((1, nc, LANES, key_group), tile4),
        pl.BlockSpec((tm, B_WIDTH), row),
        pl.BlockSpec((tm, B_WIDTH), row),
        pl.BlockSpec((2, tm, LANES), pair3),
        pl.BlockSpec((2, tm, LANES), pair3),
        pl.BlockSpec((2, nc, LANES, key_group), tile4),
    ]
    out_shape = [
        jax.ShapeDtypeStruct((4, n, LANES), BF16),
        jax.ShapeDtypeStruct((1, n, LANES), BF16),
        jax.ShapeDtypeStruct((1, n // key_group, LANES, key_group), BF16),
        jax.ShapeDtypeStruct((n, B_WIDTH), BF16),
        jax.ShapeDtypeStruct((n, B_WIDTH), BF16),
        jax.ShapeDtypeStruct((2, n, LANES), BF16),
        jax.ShapeDtypeStruct((2, n, LANES), BF16),
        jax.ShapeDtypeStruct((2, n // key_group, LANES, key_group), BF16),
    ]
    if ctx:
        out_specs += [
            pl.BlockSpec((tm, LANES), row),
            pl.BlockSpec((tm, LANES), row),
            pl.BlockSpec((tm, 2 * LANES), row),
            pl.BlockSpec((tm, 2 * LANES), row),
        ]
        out_shape += [
            jax.ShapeDtypeStruct((n, LANES), F32),
            jax.ShapeDtypeStruct((n, LANES), F32),
            jax.ShapeDtypeStruct((n, 2 * LANES), F32),
            jax.ShapeDtypeStruct((n, 2 * LANES), F32),
        ]
    return pl.pallas_call(
        functools.partial(_inproj_kernel, rope=rope, ctx=ctx, kg=key_group),
        grid=(n // tm,),
        in_specs=in_specs,
        out_specs=out_specs,
        out_shape=out_shape,
        compiler_params=_params(),
        name="inproj_ctx" if ctx else "inproj_lat",
    )(*args)


def _attend(q_masks, halves, k_ref, kj, vt_ref, vj, cache):
    tq = q_masks[0].shape[0]
    nu = len(q_masks)
    kg = vt_ref.shape[-1]
    n_groups = k_ref.shape[1] // kg
    loaders = []
    if cache is not None:
        kc, vc = cache
        loaders.append(lambda: (kc.astype(BF16), vc.T.astype(BF16)))
    for g in range(n_groups):
        loaders.append(lambda g=g: (k_ref[kj, kg * g:kg * (g + 1), :], vt_ref[vj, g]))

    def value_rows(vt, u):
        own = vt[HEAD_ROWS * halves[u]:HEAD_ROWS * (halves[u] + 1), :]
        return jnp.concatenate([own, jnp.ones((ONES_ROWS, vt.shape[1]), BF16)], axis=0)

    ms = [jnp.full((1, tq), -jnp.inf, F32)] * nu
    accs = [jnp.zeros((HEAD_ROWS + ONES_ROWS, tq), F32)] * nu
    k, vt = loaders[0]()
    scores = [_dot_nt(k, q_masks[u]) for u in range(nu)]
    for g in range(len(loaders)):
        cur_scores, cur_vt = scores, vt
        if g + 1 < len(loaders):
            k, vt = loaders[g + 1]()
            scores = [_dot_nt(k, q_masks[u]) for u in range(nu)]
        new_ms, ps = [], []
        for u in range(nu):
            m = jnp.maximum(ms[u], jnp.max(cur_scores[u], axis=0, keepdims=True))
            new_ms.append(m)
            ps.append(jnp.exp2(cur_scores[u] - m).astype(BF16))
        pvs = [_dot(value_rows(cur_vt, u), ps[u]) for u in range(nu)]
        accs = [jnp.exp2(ms[u] - new_ms[u]) * accs[u] + pvs[u] for u in range(nu)]
        ms = new_ms
    return [acc[0:HEAD_ROWS] / acc[HEAD_ROWS:HEAD_ROWS + 1] for acc in accs]


def _lane_mask(q, lo, width):
    lane = lax.broadcasted_iota(jnp.int32, q.shape, 1)
    return jnp.where((lane >= lo) & (lane < lo + width), q, jnp.zeros_like(q))


def _gqa_kernel(*refs, has_cache, nj):
    if has_cache:
        q_ref, k_ref, vt_ref, ck_ref, cv_ref, o_ref = refs
        cache = (ck_ref[0], cv_ref[0])
    else:
        q_ref, k_ref, vt_ref, o_ref = refs
        cache = None
    for j in range(nj):
        q = q_ref[j]
        q_masks = [_lane_mask(q, 0, 64), _lane_mask(q, 64, 64)]
        o0, o1 = _attend(q_masks, (0, 1), k_ref, 0, vt_ref, 0, cache)
        o_ref[:, LANES * j:LANES * (j + 1)] = jnp.concatenate([o0, o1], axis=0).T.astype(BF16)


def _gqa_attention(q, k, vt, cache_k, cache_v, layer, *, n_batch, t_seq, tq, nj):
    n = q.shape[1]
    nq = t_seq // tq
    kg = vt.shape[-1]
    has_cache = cache_k is not None
    in_specs = [
        pl.BlockSpec((nj, tq, LANES), lambda b, i, j: (j, b * nq + i, 0)),
        pl.BlockSpec((1, t_seq, LANES), lambda b, i, j: (0, b, 0)),
        pl.BlockSpec((1, t_seq // kg, LANES, kg), lambda b, i, j: (0, b, 0, 0)),
    ]
    args = [q, k, vt]
    if has_cache:
        past = cache_k.shape[1]
        cspec = pl.BlockSpec((1, past, LANES), lambda b, i, j: (b * DEPTH + layer, 0, 0))
        in_specs += [cspec, cspec]
        args += [cache_k, cache_v]
    return pl.pallas_call(
        functools.partial(_gqa_kernel, has_cache=has_cache, nj=nj),
        grid=(n_batch, nq, 4 // nj),
        in_specs=in_specs,
        out_specs=pl.BlockSpec((tq, LANES * nj), lambda b, i, j: (b * nq + i, j)),
        out_shape=jax.ShapeDtypeStruct((n, 4 * LANES), BF16),
        compiler_params=_params(),
        name="gqa_lat" if has_cache else "gqa_ctx",
    )(*args)


def _diff_kernel(*refs, has_cache, nj, lam_init):
    if has_cache:
        q_ref, k_ref, vt_ref, ck_ref, cv_ref, lam_ref, g_ref, o_ref = refs
    else:
        q_ref, k_ref, vt_ref, lam_ref, g_ref, o_ref = refs
    lv = lam_ref[0]
    lam = (jnp.exp(jnp.sum(lv[0:1] * lv[1:2], axis=-1, keepdims=True))
           - jnp.exp(jnp.sum(lv[2:3] * lv[3:4], axis=-1, keepdims=True)) + lam_init)
    for j in range(nj):
        q = q_ref[j]
        q_masks = [_lane_mask(q, 32 * c, 32) for c in range(4)]
        cache = None
        if has_cache:
            cache = (ck_ref[0, :, LANES * j:LANES * (j + 1)], cv_ref[0, :, LANES * j:LANES * (j + 1)])
        res = _attend(q_masks, (0, 0, 1, 1), k_ref, j, vt_ref, j, cache)
        heads = []
        for i in range(2):
            o = res[2 * i] - lam * res[2 * i + 1]
            ms = jnp.mean(o * o, axis=0, keepdims=True)
            heads.append(o * lax.rsqrt(ms + EPS))
        o = jnp.concatenate(heads, axis=0).T * g_ref[...] * (1.0 - lam_init)
        o_ref[:, LANES * j:LANES * (j + 1)] = o.astype(BF16)


def _diff_attention(cq, ck, cvt, cache_k, cache_v, layer, diff_lambda, subln_g, *,
                    n_batch, t_seq, tq, nj, lam_init):
    n = cq.shape[1]
    nq = t_seq // tq
    kg = cvt.shape[-1]
    has_cache = cache_k is not None
    in_specs = [
        pl.BlockSpec((nj, tq, LANES), lambda b, i, j: (j, b * nq + i, 0)),
        pl.BlockSpec((nj, t_seq, LANES), lambda b, i, j: (j, b, 0)),
        pl.BlockSpec((nj, t_seq // kg, LANES, kg), lambda b, i, j: (j, b, 0, 0)),
    ]
    args = [cq, ck, cvt]
    if has_cache:
        past = cache_k.shape[1]
        cspec = pl.BlockSpec((1, past, LANES * nj), lambda b, i, j: (b * DEPTH + layer, 0, j))
        in_specs += [cspec, cspec]
        args += [cache_k, cache_v]
    in_specs += [
        pl.BlockSpec((1, 4, C_QK_DIM), lambda b, i, j: (layer, 0, 0)),
        pl.BlockSpec((1, LANES), lambda b, i, j: (0, 0)),
    ]
    args += [diff_lambda, subln_g]
    return pl.pallas_call(
        functools.partial(_diff_kernel, has_cache=has_cache, nj=nj, lam_init=lam_init),
        grid=(n_batch, nq, 2 // nj),
        in_specs=in_specs,
        out_specs=pl.BlockSpec((tq, LANES * nj), lambda b, i, j: (b * nq + i, j)),
        out_shape=jax.ShapeDtypeStruct((n, 2 * LANES), BF16),
        compiler_params=_params(),
        name="diff_lat" if has_cache else "diff_ctx",
    )(*args)


HALO = 16


def _route(logits_t, bias_col):
    scores = _sigmoid(logits_t)
    sel = scores + bias_col
    row = lambda a, e: a[e:e + 1, :]
    grp = []
    for g in range(N_EXPERT_GROUPS):
        s0, s1, s2, s3 = (row(sel, 4 * g + k) for k in range(4))
        hi1, lo1 = jnp.maximum(s0, s1), jnp.minimum(s0, s1)
        hi2, lo2 = jnp.maximum(s2, s3), jnp.minimum(s2, s3)
        grp.append(jnp.maximum(hi1, hi2) + jnp.maximum(jnp.minimum(hi1, hi2), jnp.maximum(lo1, lo2)))
    best = grp[0]
    tg = jnp.zeros_like(best, dtype=jnp.int32)
    for g in range(1, N_EXPERT_GROUPS):
        better = grp[g] > best
        best = jnp.where(better, grp[g], best)
        tg = jnp.where(better, g, tg)

    def in_group(a, k):
        out = row(a, k)
        for g in range(1, N_EXPERT_GROUPS):
            out = jnp.where(tg == g, row(a, 4 * g + k), out)
        return out

    vals = [in_group(sel, k) for k in range(EXPERTS_PER_GROUP)]
    scs = [in_group(scores, k) for k in range(EXPERTS_PER_GROUP)]

    def first_argmax(vs):
        bv, bi = vs[0], jnp.zeros_like(tg)
        for k in range(1, len(vs)):
            better = vs[k] > bv
            bv = jnp.where(better, vs[k], bv)
            bi = jnp.where(better, k, bi)
        return bi

    i1 = first_argmax(vals)
    i2 = first_argmax([jnp.where(i1 == k, -jnp.inf, vals[k]) for k in range(EXPERTS_PER_GROUP)])

    def pick(vs, idx):
        out = vs[0]
        for k in range(1, len(vs)):
            out = jnp.where(idx == k, vs[k], out)
        return out

    w1, w2 = pick(scs, i1), pick(scs, i2)
    tot = w1 + w2
    return 4 * tg + i1, 4 * tg + i2, w1 / tot, w2 / tot


def _outproj_kernel(oa_ref, gb_ref, gu_ref, gup_ref, gun_ref, cw_ref, oc_ref, x_ref, mod_ref,
                    w_ref, g2_ref, rwh_ref, rwl_ref, rb_ref,
                    x1_ref, h2t_ref, route_ref, wts_ref, *, tm, t_seq):
    i = pl.program_id(0)
    gu = gu_ref[...].astype(F32)
    rowi = lax.broadcasted_iota(jnp.int32, gu.shape, 0)
    pos = (i * tm + rowi) % t_seq
    prev = pltpu.roll(gu, 1, 0)
    prev = jnp.where(rowi == 0, gup_ref[HALO - 1:HALO, :].astype(F32), prev)
    prev = jnp.where(pos == 0, 0.0, prev)
    nxt = pltpu.roll(gu, tm - 1, 0)
    nxt = jnp.where(rowi == tm - 1, gun_ref[0:1, :].astype(F32), nxt)
    nxt = jnp.where(pos == t_seq - 1, 0.0, nxt)
    cw = cw_ref[0]
    ob = gb_ref[...].astype(F32) * (prev * cw[0:1] + gu * cw[1:2] + nxt * cw[2:3])

    mix = (_dot(oa_ref[...], w_ref[0:512, :])
           + _dot(ob.astype(BF16), w_ref[512:768, :])
           + _dot(oc_ref[...], w_ref[768:1024, :]))
    x1 = x_ref[...] + mod_ref[0, 0, 2:3, :] * mix
    x1_ref[...] = x1
    ms = jnp.mean(x1 * x1, axis=-1, keepdims=True)
    h2 = x1 * lax.rsqrt(ms + EPS) * g2_ref[...]
    h2 = h2 * (1.0 + mod_ref[0, 0, 4:5, :]) + mod_ref[0, 0, 3:4, :]
    _store_token_tiles(h2t_ref, h2)

    hh, hl = _split(h2)
    logits = _dot(hh, rwh_ref[...]) + _dot(hh, rwl_ref[...]) + _dot(hl, rwh_ref[...])
    logits_t = logits.T[0:N_EXPERTS, :]
    e0, e1, w0, w1 = _route(logits_t, rb_ref[...])
    zeros = jnp.zeros_like(w0)
    route_ref[...] = jnp.concatenate(
        [e0.astype(F32), e1.astype(F32), w0, w1, zeros, zeros, zeros, zeros], axis=0)
    srow = lax.broadcasted_iota(jnp.int32, (LANES, tm), 0)
    wts_t = jnp.where(srow == 0, w0, 0.0) + jnp.where(srow == 1, w1, 0.0)
    wts_ref[...] = wts_t.T


def _outproj(oa, gb, gu, conv_w, oc, x, mod, layer, mod_row, w_out, g2, rwh, rwl, rb, *, tm, t_seq):
    n = x.shape[0]
    row = lambda i: (i, 0)
    const2 = lambda i: (0, 0)
    hb = tm // HALO
    last = n // HALO - 1
    return pl.pallas_call(
        functools.partial(_outproj_kernel, tm=tm, t_seq=t_seq),
        grid=(n // tm,),
        in_specs=[
            pl.BlockSpec((tm, 4 * LANES), row),
            pl.BlockSpec((tm, B_WIDTH), row),
            pl.BlockSpec((tm, B_WIDTH), row),
            pl.BlockSpec((HALO, B_WIDTH), lambda i: (jnp.maximum(i * hb - 1, 0), 0)),
            pl.BlockSpec((HALO, B_WIDTH), lambda i: (jnp.minimum((i + 1) * hb, last), 0)),
            pl.BlockSpec((1, 3, B_WIDTH), lambda i: (layer, 0, 0)),
            pl.BlockSpec((tm, 2 * LANES), row),
            pl.BlockSpec((tm, D_MODEL), row),
            pl.BlockSpec((1, 1, N_MOD, D_MODEL), lambda i: (layer, mod_row(i), 0, 0)),
            pl.BlockSpec((D_MODEL, D_MODEL), const2),
            pl.BlockSpec((1, D_MODEL), const2),
            pl.BlockSpec((D_MODEL, LANES), const2),
            pl.BlockSpec((D_MODEL, LANES), const2),
            pl.BlockSpec((N_EXPERTS, 1), const2),
        ],
        out_specs=[
            pl.BlockSpec((tm, D_MODEL), row),
            pl.BlockSpec((tm * TOKEN_TILE_ROWS, LANES), row),
            pl.BlockSpec((8, tm), lambda i: (0, i)),
            pl.BlockSpec((tm, LANES), row),
        ],
        out_shape=[
            jax.ShapeDtypeStruct((n, D_MODEL), F32),
            jax.ShapeDtypeStruct((n * TOKEN_TILE_ROWS, LANES), F32),
            jax.ShapeDtypeStruct((8, n), F32),
            jax.ShapeDtypeStruct((n, LANES), F32),
        ],
        compiler_params=_params(),
        name="outproj",
    )(oa, gb, gu, gu, gu, conv_w, oc, x, mod, w_out, g2, rwh, rwl, rb)


TOKEN_TILE_ROWS = D_MODEL // LANES
EXPERT_TILE = 256
PLAN_BLOCK = 512


def _load_token_tiles(ref, n_tokens, lead=()):
    return jnp.concatenate(
        [ref[lead + (pl.ds(c, n_tokens, stride=TOKEN_TILE_ROWS), slice(None))]
         for c in range(TOKEN_TILE_ROWS)], axis=1)


def _store_token_tiles(ref, x):
    n_tokens = x.shape[0]
    for c in range(TOKEN_TILE_ROWS):
        ref[pl.ds(c, n_tokens, stride=TOKEN_TILE_ROWS), :] = x[:, LANES * c:LANES * (c + 1)]


def _moe_plan_kernel(route_ref, tri_ref, pos_ref, info_ref, *, n, n_tiles):
    eid = lax.broadcasted_iota(jnp.int32, (N_EXPERTS, PLAN_BLOCK), 0)

    def block(b):
        sl = slice(PLAN_BLOCK * b, PLAN_BLOCK * (b + 1))
        e0 = route_ref[0:1, sl].astype(jnp.int32)
        e1 = route_ref[1:2, sl].astype(jnp.int32)
        return e0, e1, jnp.where((eid == e0) | (eid == e1), 1.0, 0.0)

    counts = jnp.zeros((N_EXPERTS, 1), F32)
    for b in range(n // PLAN_BLOCK):
        counts = counts + jnp.sum(block(b)[2], axis=1, keepdims=True)
    ntile = jnp.floor((counts + (EXPERT_TILE - 1)) * (1.0 / EXPERT_TILE))
    sub = lax.broadcasted_iota(jnp.int32, (N_EXPERTS, LANES), 0)
    lane = lax.broadcasted_iota(jnp.int32, (N_EXPERTS, LANES), 1)
    ntile_row = jnp.sum(jnp.where(sub == lane, ntile, 0.0), axis=0, keepdims=True)
    first_tile = jnp.sum(jnp.where(lane < sub, ntile_row, 0.0), axis=1, keepdims=True)
    first_row = first_tile * EXPERT_TILE

    carry = jnp.zeros((N_EXPERTS, 1), F32)
    for b in range(n // PLAN_BLOCK):
        e0, e1, mb = block(b)
        rank = _dot(mb.astype(BF16), tri_ref[...]) + carry
        carry = carry + jnp.sum(mb, axis=1, keepdims=True)
        slot = first_row + rank
        p0 = jnp.sum(jnp.where(eid == e0, slot, 0.0), axis=0, keepdims=True)
        p1 = jnp.sum(jnp.where(eid == e1, slot, 0.0), axis=0, keepdims=True)
        pad = jnp.zeros((6, PLAN_BLOCK), jnp.int32)
        pos_ref[:, PLAN_BLOCK * b:PLAN_BLOCK * (b + 1)] = jnp.concatenate(
            [p0.astype(jnp.int32), p1.astype(jnp.int32), pad], axis=0)

    tile = lax.broadcasted_iota(jnp.int32, (1, LANES), 1).astype(F32)
    total = jnp.sum(ntile, axis=0, keepdims=True)
    expert_of_tile = jnp.sum(jnp.where(first_tile <= tile, 1.0, 0.0), axis=0, keepdims=True) - 1.0
    valid = jnp.where(tile < total, 1.0, 0.0)
    ends_expert = jnp.sum(jnp.where((first_tile + ntile == tile + 1.0) & (ntile > 0.0), 1.0, 0.0),
                          axis=0, keepdims=True)
    needs_zero = jnp.where((ends_expert > 0.0) | (valid == 0.0), 1.0, 0.0)
    zrow = jnp.zeros((5, LANES), jnp.int32)
    info_ref[...] = jnp.concatenate(
        [expert_of_tile.astype(jnp.int32), valid.astype(jnp.int32), needs_zero.astype(jnp.int32),
         zrow], axis=0)


def _moe_plan(route, tri, *, n_tiles):
    n = route.shape[1]
    return pl.pallas_call(
        functools.partial(_moe_plan_kernel, n=n, n_tiles=n_tiles),
        grid=(1,),
        in_specs=[pl.BlockSpec((8, n), lambda i: (0, 0)),
                  pl.BlockSpec((PLAN_BLOCK, PLAN_BLOCK), lambda i: (0, 0))],
        out_specs=[pl.BlockSpec((8, n), lambda i: (0, 0)),
                   pl.BlockSpec((8, LANES), lambda i: (0, 0))],
        out_shape=[jax.ShapeDtypeStruct((8, n), jnp.int32),
                   jax.ShapeDtypeStruct((8, LANES), jnp.int32)],
        compiler_params=_params(),
        name="moe_plan",
    )(route, tri)


def _token_tile(ref, t):
    return ref.at[pl.ds(pl.multiple_of(t * TOKEN_TILE_ROWS, TOKEN_TILE_ROWS), TOKEN_TILE_ROWS)]


def _token_tiles(ref, first, count):
    rows = count * TOKEN_TILE_ROWS
    return ref.at[pl.ds(pl.multiple_of(first * TOKEN_TILE_ROWS, TOKEN_TILE_ROWS), rows)]


def _dispatch_kernel(zero_ref, pos_ref, h_hbm, xs_hbm, zbuf, zsem, sem, *, n_tiles):
    i = pl.program_id(0)

    @pl.when(i == 0)
    def _():
        zbuf[...] = jnp.zeros_like(zbuf)

        def fill_copy(j):
            return pltpu.make_async_copy(zbuf, _token_tiles(xs_hbm, j * EXPERT_TILE, EXPERT_TILE), zsem)

        def fill(j, c):
            @pl.when(zero_ref[j] == 1)
            def _():
                fill_copy(j).start()
            return c

        def drain(j, c):
            @pl.when(zero_ref[j] == 1)
            def _():
                fill_copy(j).wait()
            return c

        lax.fori_loop(0, n_tiles, fill, 0)
        lax.fori_loop(0, n_tiles, drain, 0)

    def scatter(r, c):
        src = _token_tile(h_hbm, i * PLAN_BLOCK + r)
        for s in range(2):
            pltpu.make_async_copy(src, _token_tile(xs_hbm, pos_ref[s, r]), sem).start()
        return c

    lax.fori_loop(0, PLAN_BLOCK, scatter, 0, unroll=8)
    for _ in range(2):
        pltpu.make_async_copy(_token_tiles(h_hbm, 0, PLAN_BLOCK), _token_tiles(xs_hbm, 0, PLAN_BLOCK),
                              sem).wait()


def _dispatch(needs_zero, pos, h2t, *, n_tiles):
    n = h2t.shape[0] // TOKEN_TILE_ROWS
    return pl.pallas_call(
        functools.partial(_dispatch_kernel, n_tiles=n_tiles),
        grid_spec=pltpu.PrefetchScalarGridSpec(
            num_scalar_prefetch=1,
            grid=(n // PLAN_BLOCK,),
            in_specs=[pl.BlockSpec((8, PLAN_BLOCK), lambda i, z: (0, i), memory_space=pltpu.SMEM),
                      pl.BlockSpec(memory_space=pl.ANY)],
            out_specs=pl.BlockSpec(memory_space=pl.ANY),
            scratch_shapes=[pltpu.VMEM((EXPERT_TILE * TOKEN_TILE_ROWS, LANES), F32),
                            pltpu.SemaphoreType.DMA(()), pltpu.SemaphoreType.DMA(())]),
        out_shape=jax.ShapeDtypeStruct((n_tiles * EXPERT_TILE * TOKEN_TILE_ROWS, LANES), F32),
        compiler_params=_params(),
        name="moe_dispatch",
    )(needs_zero, pos, h2t)


def _experts_kernel(expert_ref, valid_ref, x_ref, w1_ref, w3_ref, w2_ref, o_ref):
    j = pl.program_id(0)

    @pl.when(valid_ref[j] == 1)
    def _():
        x = _load_token_tiles(x_ref, EXPERT_TILE).astype(BF16)
        a = _dot(x, w1_ref[0])
        b = _dot(x, w3_ref[0])
        hid = (a * _sigmoid(a)) * b
        _store_token_tiles(o_ref, _dot(hid.astype(BF16), w2_ref[0]))

    @pl.when(valid_ref[j] == 0)
    def _():
        o_ref[...] = jnp.zeros_like(o_ref)


def _experts(expert_of_tile, valid, xs, w1, w3, w2, layer, *, n_tiles):
    rows = EXPERT_TILE * TOKEN_TILE_ROWS
    wmap = lambda j, e, v: (layer * N_EXPERTS + e[j], 0, 0)
    return pl.pallas_call(
        _experts_kernel,
        grid_spec=pltpu.PrefetchScalarGridSpec(
            num_scalar_prefetch=2,
            grid=(n_tiles,),
            in_specs=[pl.BlockSpec((rows, LANES), lambda j, e, v: (j, 0)),
                      pl.BlockSpec((1, D_MODEL, EXPERT_FF), wmap),
                      pl.BlockSpec((1, D_MODEL, EXPERT_FF), wmap),
                      pl.BlockSpec((1, EXPERT_FF, D_MODEL), wmap)],
            out_specs=pl.BlockSpec((rows, LANES), lambda j, e, v: (j, 0))),
        out_shape=jax.ShapeDtypeStruct((n_tiles * rows, LANES), F32),
        compiler_params=_params(),
        name="moe_experts",
    )(expert_of_tile, valid, xs, w1, w3, w2)


def _combine_kernel(pos_ref, ys_hbm, wts_ref, x1_ref, mod_ref, fg_ref, o_ref, buf, sem, *, final):
    def gather(r, c):
        for s in range(2):
            pltpu.make_async_copy(_token_tile(ys_hbm, pos_ref[s, r]), _token_tile(buf.at[s], r),
                                  sem).start()
        return c

    lax.fori_loop(0, PLAN_BLOCK, gather, 0, unroll=8)
    for s in range(2):
        pltpu.make_async_copy(_token_tiles(ys_hbm, 0, PLAN_BLOCK), buf.at[s], sem).wait()
    y0 = _load_token_tiles(buf, PLAN_BLOCK, lead=(0,))
    y1 = _load_token_tiles(buf, PLAN_BLOCK, lead=(1,))
    wts = wts_ref[...]
    moe = wts[:, 0:1] * y0 + wts[:, 1:2] * y1
    x2 = x1_ref[...] + mod_ref[0, 0, 5:6, :] * moe
    if final:
        ms = jnp.mean(x2 * x2, axis=-1, keepdims=True)
        x2 = x2 * lax.rsqrt(ms + EPS) * fg_ref[...]
    o_ref[...] = x2


def _combine(pos, ys, wts, x1, mod, layer, mod_row, fg, *, final):
    n = x1.shape[0]
    row = lambda i: (i, 0)
    return pl.pallas_call(
        functools.partial(_combine_kernel, final=final),
        grid=(n // PLAN_BLOCK,),
        in_specs=[
            pl.BlockSpec((8, PLAN_BLOCK), lambda i: (0, i), memory_space=pltpu.SMEM),
            pl.BlockSpec(memory_space=pl.ANY),
            pl.BlockSpec((PLAN_BLOCK, LANES), row),
            pl.BlockSpec((PLAN_BLOCK, D_MODEL), row),
            pl.BlockSpec((1, 1, N_MOD, D_MODEL), lambda i: (layer, mod_row(i), 0, 0)),
            pl.BlockSpec((1, D_MODEL), lambda i: (0, 0)),
        ],
        out_specs=pl.BlockSpec((PLAN_BLOCK, D_MODEL), row),
        out_shape=jax.ShapeDtypeStruct((n, D_MODEL), F32),
        scratch_shapes=[pltpu.VMEM((2, PLAN_BLOCK * TOKEN_TILE_ROWS, LANES), F32),
                        pltpu.SemaphoreType.DMA(())],
        compiler_params=_params(),
        name="moe_combine",
    )(pos, ys, wts, x1, mod, fg)


def _moe(h2t, route, wts, x1, mod, layer, mod_row, w1, w3, w2, fg, tri, *, final):
    n = x1.shape[0]
    n_tiles = 2 * n // EXPERT_TILE + N_EXPERTS
    pos, info = _moe_plan(route, tri, n_tiles=n_tiles)
    expert_of_tile, valid, needs_zero = info[0, :n_tiles], info[1, :n_tiles], info[2, :n_tiles]
    xs = _dispatch(needs_zero, pos, h2t, n_tiles=n_tiles)
    ys = _experts(expert_of_tile, valid, xs, w1, w3, w2, layer, n_tiles=n_tiles)
    return _combine(pos, ys, wts, x1, mod, layer, mod_row, fg, final=final)


def _rope_tables(n_tokens, rot_dim):
    pos = np.arange(n_tokens)
    r = (pos // GRID_W).astype(np.float64)[:, None]
    col = (pos % GRID_W).astype(np.float64)[:, None]
    n_freq = rot_dim // 4
    inv_freq = ROPE_THETA ** (-np.arange(n_freq, dtype=np.float64) / n_freq)
    ang = np.concatenate([r * inv_freq, col * inv_freq], axis=-1)
    ang = np.repeat(ang, 2, axis=-1)
    sign = np.tile(np.array([-1.0, 1.0]), rot_dim // 2)
    cos = np.tile(np.cos(ang), (1, LANES // rot_dim))
    sin = np.tile(np.sin(ang) * sign, (1, LANES // rot_dim))
    return jnp.asarray(cos, F32), jnp.asarray(sin, F32)


def _block_diag_ones():
    idx = np.arange(LANES) // 64
    return jnp.asarray((idx[:, None] == idx[None, :]).astype(np.float32), BF16)


def _pair_heads_by_kv_group(w, axis):
    shape = w.shape
    lead, tail = shape[:axis], shape[axis + 1:]
    w = w.reshape(lead + (A_KV_HEADS, A_HEADS // A_KV_HEADS, A_HEAD_DIM) + tail)
    w = jnp.swapaxes(w, axis, axis + 1)
    return w.reshape(shape)


def kernel(x_prompt, x_sample, cache_gqa_k, cache_gqa_v, cache_diff_k, cache_diff_v, c, c_ctx, w_in,
           w_out, norm1_g, norm2_g, ada_w, ada_b, gqa_qnorm_g, gqa_knorm_g, conv_w, diff_lambda,
           diff_subln_g, router_w, router_bias, moe_w1, moe_w3, moe_w2, final_g):
    n_ctx_b, t_ctx, _ = x_prompt.shape
    n_lat_b, t_lat, _ = x_sample.shape
    past = cache_gqa_k.shape[2]
    n_q = A_HEADS * A_HEAD_DIM

    cvec = jnp.concatenate([c_ctx[None, :], c, jnp.zeros((8 - 1 - n_lat_b, D_MODEL), F32)], axis=0)
    mod = _adaln(cvec, ada_w, ada_b).reshape(DEPTH, 8, N_MOD, D_MODEL)

    w_in_b = jnp.concatenate(
        [_pair_heads_by_kv_group(w_in[:, :, :n_q], 2), w_in[:, :, n_q:]], axis=2).astype(BF16)
    w_out_b = jnp.concatenate(
        [_pair_heads_by_kv_group(w_out[:, :n_q, :], 1), w_out[:, n_q:, :]], axis=1).astype(BF16)
    w1_b = moe_w1.astype(BF16).reshape(DEPTH * N_EXPERTS, D_MODEL, EXPERT_FF)
    w3_b = moe_w3.astype(BF16).reshape(DEPTH * N_EXPERTS, D_MODEL, EXPERT_FF)
    w2_b = moe_w2.astype(BF16).reshape(DEPTH * N_EXPERTS, EXPERT_FF, D_MODEL)
    rw = jnp.pad(router_w, ((0, 0), (0, LANES - N_EXPERTS)))
    rwh = rw.astype(BF16)
    rwl = (rw - rwh.astype(F32)).astype(BF16)
    rb = router_bias.reshape(N_EXPERTS, 1)
    bd = _block_diag_ones()
    tri = jnp.asarray(np.triu(np.ones((PLAN_BLOCK, PLAN_BLOCK), np.float32), k=1), BF16)
    fg = final_g.reshape(1, D_MODEL)
    tables = _rope_tables(t_lat, A_HEAD_DIM) + _rope_tables(t_lat, C_QK_DIM)

    ck_cache = cache_gqa_k.reshape(n_lat_b * DEPTH, past, A_KV_HEADS * A_HEAD_DIM)
    cv_cache = cache_gqa_v.reshape(n_lat_b * DEPTH, past, A_KV_HEADS * A_HEAD_DIM)
    dk_cache = cache_diff_k.reshape(n_lat_b * DEPTH, past, C_HEADS * 2 * C_QK_DIM)
    dv_cache = cache_diff_v.reshape(n_lat_b * DEPTH, past, C_HEADS * C_V_DIM)

    xp = x_prompt.reshape(n_ctx_b * t_ctx, D_MODEL)
    xs = x_sample.reshape(n_lat_b * t_lat, D_MODEL)
    tm = 512
    tq = 256
    new_state = []
    for l in range(DEPTH):
        lam_init = 0.8 - 0.6 * math.exp(-0.3 * l)
        g1 = norm1_g[l].reshape(1, D_MODEL)
        g2 = norm2_g[l].reshape(1, D_MODEL)
        qg = jnp.tile(gqa_qnorm_g[l], 2).reshape(1, LANES)
        kg = jnp.tile(gqa_knorm_g[l], 2).reshape(1, LANES)
        sg = jnp.tile(diff_subln_g[l], 2).reshape(1, LANES)
        final = l == DEPTH - 1

        def run_stream(x, n_batch, t_seq, mod_row_of, tabs, caches, nj_gqa, nj_diff, key_group):
            ctx = caches is None
            outs = _inproj(x, mod, l, lambda i: mod_row_of(i, tm), g1, w_in_b[l], qg, kg, bd, tabs,
                           tm=tm, ctx=ctx, key_group=key_group)
            q, k, vt, gb, gu, cq, ck, cvt = outs[:8]
            c_gk, c_gv, c_dk, c_dv = caches if caches is not None else (None,) * 4
            oa = _gqa_attention(q, k, vt, c_gk, c_gv, l, n_batch=n_batch, t_seq=t_seq, tq=tq,
                                nj=nj_gqa)
            oc = _diff_attention(cq, ck, cvt, c_dk, c_dv, l, diff_lambda, sg, n_batch=n_batch,
                                 t_seq=t_seq, tq=tq, nj=nj_diff, lam_init=lam_init)
            x1, h2t, route, wts = _outproj(oa, gb, gu, conv_w, oc, x, mod, l,
                                           lambda i: mod_row_of(i, tm), w_out_b[l], g2, rwh, rwl, rb,
                                           tm=tm, t_seq=t_seq)
            x2 = _moe(h2t, route, wts, x1, mod, l, lambda i: mod_row_of(i, PLAN_BLOCK), w1_b, w3_b,
                      w2_b, fg, tri, final=final)
            return x2, outs[8:]

        xp, state = run_stream(xp, n_ctx_b, t_ctx, lambda i, t: 0, None, None, 4, 2, t_ctx)
        new_state.append(state)
        xs, _ = run_stream(xs, n_lat_b, t_lat, lambda i, t: 1 + (i * t) // t_lat, tables,
                           (ck_cache, cv_cache, dk_cache, dv_cache), 1, 1, 512)

    def stack(k, heads, dim):
        return jnp.stack([s[k].reshape(n_ctx_b, t_ctx, heads, dim) for s in new_state], axis=1)

    return (xp.reshape(n_ctx_b, t_ctx, D_MODEL), xs.reshape(n_lat_b, t_lat, D_MODEL),
            stack(0, A_KV_HEADS, A_HEAD_DIM), stack(1, A_KV_HEADS, A_HEAD_DIM),
            stack(2, C_HEADS, 2 * C_QK_DIM), stack(3, C_HEADS, C_V_DIM))
```

```python
import functools
import math

import numpy as np
import jax
import jax.numpy as jnp
from jax import lax
from jax.experimental import pallas as pl
from jax.experimental.pallas import tpu as pltpu

D_MODEL = 1024
DEPTH = 2
GRID_W = 64
A_HEADS = 8
A_KV_HEADS = 2
A_HEAD_DIM = 64
B_WIDTH = 256
C_HEADS = 4
C_QK_DIM = 32
C_V_DIM = 64
N_EXPERTS = 16
EXPERTS_PER_GROUP = 4
N_EXPERT_GROUPS = 4
EXPERT_FF = 512
ROPE_THETA = 10000.0
EPS = 1e-6
N_MOD = 6
W_IN_WIDTH = 2304
LOG2E = math.log2(math.e)

LANES = 128
HEAD_ROWS = 64
ONES_ROWS = 16
VMEM_LIMIT_BYTES = 56 * 1024 * 1024

F32 = jnp.float32
BF16 = jnp.bfloat16

_dot = functools.partial(jnp.dot, preferred_element_type=F32)


def _dot_nt(a, b):
    return lax.dot_general(a, b, (((1,), (1,)), ((), ())), preferred_element_type=F32)


def _split(a):
    hi = a.astype(BF16)
    lo = (a - hi.astype(F32)).astype(BF16)
    return hi, lo


def _dot3(a, b):
    ah, al = _split(a)
    bh, bl = _split(b)
    return _dot(ah, bh) + _dot(ah, bl) + _dot(al, bh)


def _sigmoid(x):
    return 1.0 / (1.0 + jnp.exp(-x))


def _seg_rms(x, bd, g):
    hi, lo = _split(x * x)
    ss = _dot(hi, bd) + _dot(lo, bd)
    return x * lax.rsqrt(ss * (1.0 / 64.0) + EPS) * g


def _rope(x, cos, sin_signed):
    lane = lax.broadcasted_iota(jnp.int32, x.shape, 1)
    even = (lane & 1) == 0
    partner = jnp.where(even, pltpu.roll(x, LANES - 1, 1), pltpu.roll(x, 1, 1))
    return x * cos + partner * sin_signed


def _params():
    return pltpu.CompilerParams(vmem_limit_bytes=VMEM_LIMIT_BYTES)


def _adaln_kernel(c_ref, w_ref, b_ref, o_ref):
    c = c_ref[...]
    s = c * _sigmoid(c)
    o_ref[0] = _dot3(s, w_ref[0]) + b_ref[0]


def _adaln(cvec, ada_w, ada_b):
    tn = 1536
    nt = (N_MOD * D_MODEL) // tn
    return pl.pallas_call(
        _adaln_kernel,
        grid=(DEPTH, nt),
        in_specs=[
            pl.BlockSpec((8, D_MODEL), lambda l, n: (0, 0)),
            pl.BlockSpec((1, D_MODEL, tn), lambda l, n: (l, 0, n)),
            pl.BlockSpec((1, 1, tn), lambda l, n: (l, 0, n)),
        ],
        out_specs=pl.BlockSpec((1, 8, tn), lambda l, n: (l, 0, n)),
        out_shape=jax.ShapeDtypeStruct((DEPTH, 8, N_MOD * D_MODEL), F32),
        compiler_params=_params(),
        name="adaln",
    )(cvec, ada_w, ada_b.reshape(DEPTH, 1, N_MOD * D_MODEL))


def _inproj_kernel(*refs, rope, ctx, kg):
    it = iter(refs)
    x_ref, mod_ref, g1_ref, w_ref, qg_ref, kg_ref, bd_ref = (next(it) for _ in range(7))
    if rope:
        cos_a, sin_a, cos_c, sin_c = (next(it) for _ in range(4))
    q_ref, k_ref, vt_ref, gb_ref, gu_ref, cq_ref, ck_ref, cvt_ref = (next(it) for _ in range(8))
    if ctx:
        k32_ref, v32_ref, ck32_ref, cv32_ref = (next(it) for _ in range(4))

    x = x_ref[...]
    ms = jnp.mean(x * x, axis=-1, keepdims=True)
    y = x * lax.rsqrt(ms + EPS) * g1_ref[...]
    h = y * (1.0 + mod_ref[0, 0, 1:2, :]) + mod_ref[0, 0, 0:1, :]
    hb = h.astype(BF16)
    bd = bd_ref[...]

    def seg(lo, width):
        return _dot(hb, w_ref[:, lo:lo + width])

    def store_transposed(dst, a):
        at = a.T.astype(BF16)
        for c in range(a.shape[0] // kg):
            dst[c] = at[:, kg * c:kg * (c + 1)]

    for j in range(4):
        q = _seg_rms(seg(LANES * j, LANES), bd, qg_ref[...])
        if rope:
            q = _rope(q, cos_a[...], sin_a[...])
        q_ref[j] = (q * (A_HEAD_DIM ** -0.5 * LOG2E)).astype(BF16)

    k = _seg_rms(seg(512, LANES), bd, kg_ref[...])
    if rope:
        k = _rope(k, cos_a[...], sin_a[...])
    v = seg(640, LANES)
    if ctx:
        k32_ref[...] = k
        v32_ref[...] = v
    k_ref[0] = k.astype(BF16)
    store_transposed(vt_ref.at[0], v)

    gb_ref[...] = seg(768, B_WIDTH).astype(BF16)
    gu_ref[...] = (seg(1024, B_WIDTH) * seg(1280, B_WIDTH)).astype(BF16)

    for j in range(2):
        cq = seg(1536 + LANES * j, LANES)
        ck = seg(1792 + LANES * j, LANES)
        cv = seg(2048 + LANES * j, LANES)
        if ctx:
            ck32_ref[:, LANES * j:LANES * (j + 1)] = ck
            cv32_ref[:, LANES * j:LANES * (j + 1)] = cv
        if rope:
            cq = _rope(cq, cos_c[...], sin_c[...])
            ck = _rope(ck, cos_c[...], sin_c[...])
        cq_ref[j] = (cq * (C_QK_DIM ** -0.5 * LOG2E)).astype(BF16)
        ck_ref[j] = ck.astype(BF16)
        store_transposed(cvt_ref.at[j], cv)


def _inproj(x, mod, layer, mod_row, g1, w_in, qg, kg, bd, tables, *, tm, ctx, key_group):
    n = x.shape[0]
    rope = tables is not None
    row = lambda i: (i, 0)
    const2 = lambda i: (0, 0)
    pair3 = lambda i: (0, i, 0)
    tile4 = lambda i: (0, i, 0, 0)
    nc = tm // key_group
    in_specs = [
        pl.BlockSpec((tm, D_MODEL), row),
        pl.BlockSpec((1, 1, N_MOD, D_MODEL), lambda i: (layer, mod_row(i), 0, 0)),
        pl.BlockSpec((1, D_MODEL), const2),
        pl.BlockSpec((D_MODEL, W_IN_WIDTH), const2),
        pl.BlockSpec((1, LANES), const2),
        pl.BlockSpec((1, LANES), const2),
        pl.BlockSpec((LANES, LANES), const2),
    ]
    args = [x, mod, g1, w_in, qg, kg, bd]
    if rope:
        t = tables[0].shape[0]
        tab = lambda i: (i % (t // tm), 0)
        in_specs += [pl.BlockSpec((tm, LANES), tab)] * 4
        args += list(tables)
    out_specs = [
        pl.BlockSpec((4, tm, LANES), pair3),
        pl.BlockSpec((1, tm, LANES), pair3),
        pl.BlockSpec((1, nc, LANES, key_group), tile4),
        pl.BlockSpec((tm, B_WIDTH), row),
        pl.BlockSpec((tm, B_WIDTH), row),
        pl.BlockSpec((2, tm, LANES), pair3),
        pl.BlockSpec((2, tm, LANES), pair3),
        pl.BlockSpec((2, nc, LANES, key_group), tile4),
    ]
    out_shape = [
        jax.ShapeDtypeStruct((4, n, LANES), BF16),
        jax.ShapeDtypeStruct((1, n, LANES), BF16),
        jax.ShapeDtypeStruct((1, n // key_group, LANES, key_group), BF16),
        jax.ShapeDtypeStruct((n, B_WIDTH), BF16),
        jax.ShapeDtypeStruct((n, B_WIDTH), BF16),
        jax.ShapeDtypeStruct((2, n, LANES), BF16),
        jax.ShapeDtypeStruct((2, n, LANES), BF16),
        jax.ShapeDtypeStruct((2, n // key_group, LANES, key_group), BF16),
    ]
    if ctx:
        out_specs += [
            pl.BlockSpec((tm, LANES), row),
            pl.BlockSpec((tm, LANES), row),
            pl.BlockSpec((tm, 2 * LANES), row),
            pl.BlockSpec((tm, 2 * LANES), row),
        ]
        out_shape += [
            jax.ShapeDtypeStruct((n, LANES), F32),
            jax.ShapeDtypeStruct((n, LANES), F32),
            jax.ShapeDtypeStruct((n, 2 * LANES), F32),
            jax.ShapeDtypeStruct((n, 2 * LANES), F32),
        ]
    return pl.pallas_call(
        functools.partial(_inproj_kernel, rope=rope, ctx=ctx, kg=key_group),
        grid=(n // tm,),
        in_specs=in_specs,
        out_specs=out_specs,
        out_shape=out_shape,
        compiler_params=_params(),
        name="inproj_ctx" if ctx else "inproj_lat",
    )(*args)


def _attend(q_masks, halves, k_ref, kj, vt_ref, vj, cache):
    tq = q_masks[0].shape[0]
    nu = len(q_masks)
    kg = vt_ref.shape[-1]
    n_groups = k_ref.shape[1] // kg
    loaders = []
    if cache is not None:
        kc, vc = cache
        loaders.append(lambda: (kc.astype(BF16), vc.T.astype(BF16)))
    for g in range(n_groups):
        loaders.append(lambda g=g: (k_ref[kj, kg * g:kg * (g + 1), :], vt_ref[vj, g]))

    def value_rows(vt, u):
        own = vt[HEAD_ROWS * halves[u]:HEAD_ROWS * (halves[u] + 1), :]
        return jnp.concatenate([own, jnp.ones((ONES_ROWS, vt.shape[1]), BF16)], axis=0)

    ms = [jnp.full((1, tq), -jnp.inf, F32)] * nu
    accs = [jnp.zeros((HEAD_ROWS + ONES_ROWS, tq), F32)] * nu
    k, vt = loaders[0]()
    scores = [_dot_nt(k, q_masks[u]) for u in range(nu)]
    for g in range(len(loaders)):
        cur_scores, cur_vt = scores, vt
        if g + 1 < len(loaders):
            k, vt = loaders[g + 1]()
            scores = [_dot_nt(k, q_masks[u]) for u in range(nu)]
        new_ms, ps = [], []
        for u in range(nu):
            m = jnp.maximum(ms[u], jnp.max(cur_scores[u], axis=0, keepdims=True))
            new_ms.append(m)
            ps.append(jnp.exp2(cur_scores[u] - m).astype(BF16))
        pvs = [_dot(value_rows(cur_vt, u), ps[u]) for u in range(nu)]
        accs = [jnp.exp2(ms[u] - new_ms[u]) * accs[u] + pvs[u] for u in range(nu)]
        ms = new_ms
    return [acc[0:HEAD_ROWS] / acc[HEAD_ROWS:HEAD_ROWS + 1] for acc in accs]


def _lane_mask(q, lo, width):
    lane = lax.broadcasted_iota(jnp.int32, q.shape, 1)
    return jnp.where((lane >= lo) & (lane < lo + width), q, jnp.zeros_like(q))


def _gqa_kernel(*refs, has_cache, nj):
    if has_cache:
        q_ref, k_ref, vt_ref, ck_ref, cv_ref, o_ref = refs
        cache = (ck_ref[0], cv_ref[0])
    else:
        q_ref, k_ref, vt_ref, o_ref = refs
        cache = None
    for j in range(nj):
        q = q_ref[j]
        q_masks = [_lane_mask(q, 0, 64), _lane_mask(q, 64, 64)]
        o0, o1 = _attend(q_masks, (0, 1), k_ref, 0, vt_ref, 0, cache)
        o_ref[:, LANES * j:LANES * (j + 1)] = jnp.concatenate([o0, o1], axis=0).T.astype(BF16)


def _gqa_attention(q, k, vt, cache_k, cache_v, layer, *, n_batch, t_seq, tq, nj):
    n = q.shape[1]
    nq = t_seq // tq
    kg = vt.shape[-1]
    has_cache = cache_k is not None
    in_specs = [
        pl.BlockSpec((nj, tq, LANES), lambda b, i, j: (j, b * nq + i, 0)),
        pl.BlockSpec((1, t_seq, LANES), lambda b, i, j: (0, b, 0)),
        pl.BlockSpec((1, t_seq // kg, LANES, kg), lambda b, i, j: (0, b, 0, 0)),
    ]
    args = [q, k, vt]
    if has_cache:
        past = cache_k.shape[1]
        cspec = pl.BlockSpec((1, past, LANES), lambda b, i, j: (b * DEPTH + layer, 0, 0))
        in_specs += [cspec, cspec]
        args += [cache_k, cache_v]
    return pl.pallas_call(
        functools.partial(_gqa_kernel, has_cache=has_cache, nj=nj),
        grid=(n_batch, nq, 4 // nj),
        in_specs=in_specs,
        out_specs=pl.BlockSpec((tq, LANES * nj), lambda b, i, j: (b * nq + i, j)),
        out_shape=jax.ShapeDtypeStruct((n, 4 * LANES), BF16),
        compiler_params=_params(),
        name="gqa_lat" if has_cache else "gqa_ctx",
    )(*args)


def _diff_kernel(*refs, has_cache, nj, lam_init):
    if has_cache:
        q_ref, k_ref, vt_ref, ck_ref, cv_ref, lam_ref, g_ref, o_ref = refs
    else:
        q_ref, k_ref, vt_ref, lam_ref, g_ref, o_ref = refs
    lv = lam_ref[0]
    lam = (jnp.exp(jnp.sum(lv[0:1] * lv[1:2], axis=-1, keepdims=True))
           - jnp.exp(jnp.sum(lv[2:3] * lv[3:4], axis=-1, keepdims=True)) + lam_init)
    for j in range(nj):
        q = q_ref[j]
        q_masks = [_lane_mask(q, 32 * c, 32) for c in range(4)]
        cache = None
        if has_cache:
            cache = (ck_ref[0, :, LANES * j:LANES * (j + 1)], cv_ref[0, :, LANES * j:LANES * (j + 1)])
        res = _attend(q_masks, (0, 0, 1, 1), k_ref, j, vt_ref, j, cache)
        heads = []
        for i in range(2):
            o = res[2 * i] - lam * res[2 * i + 1]
            ms = jnp.mean(o * o, axis=0, keepdims=True)
            heads.append(o * lax.rsqrt(ms + EPS))
        o = jnp.concatenate(heads, axis=0).T * g_ref[...] * (1.0 - lam_init)
        o_ref[:, LANES * j:LANES * (j + 1)] = o.astype(BF16)


def _diff_attention(cq, ck, cvt, cache_k, cache_v, layer, diff_lambda, subln_g, *,
                    n_batch, t_seq, tq, nj, lam_init):
    n = cq.shape[1]
    nq = t_seq // tq
    kg = cvt.shape[-1]
    has_cache = cache_k is not None
    in_specs = [
        pl.BlockSpec((nj, tq, LANES), lambda b, i, j: (j, b * nq + i, 0)),
        pl.BlockSpec((nj, t_seq, LANES), lambda b, i, j: (j, b, 0)),
        pl.BlockSpec((nj, t_seq // kg, LANES, kg), lambda b, i, j: (j, b, 0, 0)),
    ]
    args = [cq, ck, cvt]
    if has_cache:
        past = cache_k.shape[1]
        cspec = pl.BlockSpec((1, past, LANES * nj), lambda b, i, j: (b * DEPTH + layer, 0, j))
        in_specs += [cspec, cspec]
        args += [cache_k, cache_v]
    in_specs += [
        pl.BlockSpec((1, 4, C_QK_DIM), lambda b, i, j: (layer, 0, 0)),
        pl.BlockSpec((1, LANES), lambda b, i, j: (0, 0)),
    ]
    args += [diff_lambda, subln_g]
    return pl.pallas_call(
        functools.partial(_diff_kernel, has_cache=has_cache, nj=nj, lam_init=lam_init),
        grid=(n_batch, nq, 2 // nj),
        in_specs=in_specs,
        out_specs=pl.BlockSpec((tq, LANES * nj), lambda b, i, j: (b * nq + i, j)),
        out_shape=jax.ShapeDtypeStruct((n, 2 * LANES), BF16),
        compiler_params=_params(),
        name="diff_lat" if has_cache else "diff_ctx",
    )(*args)


HALO = 16


def _route(logits_t, bias_col):
    scores = _sigmoid(logits_t)
    sel = scores + bias_col
    row = lambda a, e: a[e:e + 1, :]
    grp = []
    for g in range(N_EXPERT_GROUPS):
        s0, s1, s2, s3 = (row(sel, 4 * g + k) for k in range(4))
        hi1, lo1 = jnp.maximum(s0, s1), jnp.minimum(s0, s1)
        hi2, lo2 = jnp.maximum(s2, s3), jnp.minimum(s2, s3)
        grp.append(jnp.maximum(hi1, hi2) + jnp.maximum(jnp.minimum(hi1, hi2), jnp.maximum(lo1, lo2)))
    best = grp[0]
    tg = jnp.zeros_like(best, dtype=jnp.int32)
    for g in range(1, N_EXPERT_GROUPS):
        better = grp[g] > best
        best = jnp.where(better, grp[g], best)
        tg = jnp.where(better, g, tg)

    def in_group(a, k):
        out = row(a, k)
        for g in range(1, N_EXPERT_GROUPS):
            out = jnp.where(tg == g, row(a, 4 * g + k), out)
        return out

    vals = [in_group(sel, k) for k in range(EXPERTS_PER_GROUP)]
    scs = [in_group(scores, k) for k in range(EXPERTS_PER_GROUP)]

    def first_argmax(vs):
        bv, bi = vs[0], jnp.zeros_like(tg)
        for k in range(1, len(vs)):
            better = vs[k] > bv
            bv = jnp.where(better, vs[k], bv)
            bi = jnp.where(better, k, bi)
        return bi

    i1 = first_argmax(vals)
    i2 = first_argmax([jnp.where(i1 == k, -jnp.inf, vals[k]) for k in range(EXPERTS_PER_GROUP)])

    def pick(vs, idx):
        out = vs[0]
        for k in range(1, len(vs)):
            out = jnp.where(idx == k, vs[k], out)
        return out

    w1, w2 = pick(scs, i1), pick(scs, i2)
    tot = w1 + w2
    return 4 * tg + i1, 4 * tg + i2, w1 / tot, w2 / tot


def _outproj_kernel(oa_ref, gb_ref, gu_ref, gup_ref, gun_ref, cw_ref, oc_ref, x_ref, mod_ref,
                    w_ref, g2_ref, rwh_ref, rwl_ref, rb_ref,
                    x1_ref, h2t_ref, route_ref, wts_ref, *, tm, t_seq):
    i = pl.program_id(0)
    gu = gu_ref[...].astype(F32)
    rowi = lax.broadcasted_iota(jnp.int32, gu.shape, 0)
    pos = (i * tm + rowi) % t_seq
    prev = pltpu.roll(gu, 1, 0)
    prev = jnp.where(rowi == 0, gup_ref[HALO - 1:HALO, :].astype(F32), prev)
    prev = jnp.where(pos == 0, 0.0, prev)
    nxt = pltpu.roll(gu, tm - 1, 0)
    nxt = jnp.where(rowi == tm - 1, gun_ref[0:1, :].astype(F32), nxt)
    nxt = jnp.where(pos == t_seq - 1, 0.0, nxt)
    cw = cw_ref[0]
    ob = gb_ref[...].astype(F32) * (prev * cw[0:1] + gu * cw[1:2] + nxt * cw[2:3])

    mix = (_dot(oa_ref[...], w_ref[0:512, :])
           + _dot(ob.astype(BF16), w_ref[512:768, :])
           + _dot(oc_ref[...], w_ref[768:1024, :]))
    x1 = x_ref[...] + mod_ref[0, 0, 2:3, :] * mix
    x1_ref[...] = x1
    ms = jnp.mean(x1 * x1, axis=-1, keepdims=True)
    h2 = x1 * lax.rsqrt(ms + EPS) * g2_ref[...]
    h2 = h2 * (1.0 + mod_ref[0, 0, 4:5, :]) + mod_ref[0, 0, 3:4, :]
    _store_token_tiles(h2t_ref, h2)

    hh, hl = _split(h2)
    logits = _dot(hh, rwh_ref[...]) + _dot(hh, rwl_ref[...]) + _dot(hl, rwh_ref[...])
    logits_t = logits.T[0:N_EXPERTS, :]
    e0, e1, w0, w1 = _route(logits_t, rb_ref[...])
    zeros = jnp.zeros_like(w0)
    route_ref[...] = jnp.concatenate(
        [e0.astype(F32), e1.astype(F32), w0, w1, zeros, zeros, zeros, zeros], axis=0)
    srow = lax.broadcasted_iota(jnp.int32, (LANES, tm), 0)
    wts_t = jnp.where(srow == 0, w0, 0.0) + jnp.where(srow == 1, w1, 0.0)
    wts_ref[...] = wts_t.T


def _outproj(oa, gb, gu, conv_w, oc, x, mod, layer, mod_row, w_out, g2, rwh, rwl, rb, *, tm, t_seq):
    n = x.shape[0]
    row = lambda i: (i, 0)
    const2 = lambda i: (0, 0)
    hb = tm // HALO
    last = n // HALO - 1
    return pl.pallas_call(
        functools.partial(_outproj_kernel, tm=tm, t_seq=t_seq),
        grid=(n // tm,),
        in_specs=[
            pl.BlockSpec((tm, 4 * LANES), row),
            pl.BlockSpec((tm, B_WIDTH), row),
            pl.BlockSpec((tm, B_WIDTH), row),
            pl.BlockSpec((HALO, B_WIDTH), lambda i: (jnp.maximum(i * hb - 1, 0), 0)),
            pl.BlockSpec((HALO, B_WIDTH), lambda i: (jnp.minimum((i + 1) * hb, last), 0)),
            pl.BlockSpec((1, 3, B_WIDTH), lambda i: (layer, 0, 0)),
            pl.BlockSpec((tm, 2 * LANES), row),
            pl.BlockSpec((tm, D_MODEL), row),
            pl.BlockSpec((1, 1, N_MOD, D_MODEL), lambda i: (layer, mod_row(i), 0, 0)),
            pl.BlockSpec((D_MODEL, D_MODEL), const2),
            pl.BlockSpec((1, D_MODEL), const2),
            pl.BlockSpec((D_MODEL, LANES), const2),
            pl.BlockSpec((D_MODEL, LANES), const2),
            pl.BlockSpec((N_EXPERTS, 1), const2),
        ],
        out_specs=[
            pl.BlockSpec((tm, D_MODEL), row),
            pl.BlockSpec((tm * TOKEN_TILE_ROWS, LANES), row),
            pl.BlockSpec((8, tm), lambda i: (0, i)),
            pl.BlockSpec((tm, LANES), row),
        ],
        out_shape=[
            jax.ShapeDtypeStruct((n, D_MODEL), F32),
            jax.ShapeDtypeStruct((n * TOKEN_TILE_ROWS, LANES), F32),
            jax.ShapeDtypeStruct((8, n), F32),
            jax.ShapeDtypeStruct((n, LANES), F32),
        ],
        compiler_params=_params(),
        name="outproj",
    )(oa, gb, gu, gu, gu, conv_w, oc, x, mod, w_out, g2, rwh, rwl, rb)


TOKEN_TILE_ROWS = D_MODEL // LANES
EXPERT_TILE = 256
PLAN_BLOCK = 512


def _load_token_tiles(ref, n_tokens, lead=()):
    return jnp.concatenate(
        [ref[lead + (pl.ds(c, n_tokens, stride=TOKEN_TILE_ROWS), slice(None))]
         for c in range(TOKEN_TILE_ROWS)], axis=1)


def _store_token_tiles(ref, x):
    n_tokens = x.shape[0]
    for c in range(TOKEN_TILE_ROWS):
        ref[pl.ds(c, n_tokens, stride=TOKEN_TILE_ROWS), :] = x[:, LANES * c:LANES * (c + 1)]


def _moe_plan_kernel(route_ref, tri_ref, pos_ref, info_ref, *, n, n_tiles):
    eid = lax.broadcasted_iota(jnp.int32, (N_EXPERTS, PLAN_BLOCK), 0)

    def block(b):
        sl = slice(PLAN_BLOCK * b, PLAN_BLOCK * (b + 1))
        e0 = route_ref[0:1, sl].astype(jnp.int32)
        e1 = route_ref[1:2, sl].astype(jnp.int32)
        return e0, e1, jnp.where((eid == e0) | (eid == e1), 1.0, 0.0)

    counts = jnp.zeros((N_EXPERTS, 1), F32)
    for b in range(n // PLAN_BLOCK):
        counts = counts + jnp.sum(block(b)[2], axis=1, keepdims=True)
    ntile = jnp.floor((counts + (EXPERT_TILE - 1)) * (1.0 / EXPERT_TILE))
    sub = lax.broadcasted_iota(jnp.int32, (N_EXPERTS, LANES), 0)
    lane = lax.broadcasted_iota(jnp.int32, (N_EXPERTS, LANES), 1)
    ntile_row = jnp.sum(jnp.where(sub == lane, ntile, 0.0), axis=0, keepdims=True)
    first_tile = jnp.sum(jnp.where(lane < sub, ntile_row, 0.0), axis=1, keepdims=True)
    first_row = first_tile * EXPERT_TILE

    carry = jnp.zeros((N_EXPERTS, 1), F32)
    for b in range(n // PLAN_BLOCK):
        e0, e1, mb = block(b)
        rank = _dot(mb.astype(BF16), tri_ref[...]) + carry
        carry = carry + jnp.sum(mb, axis=1, keepdims=True)
        slot = first_row + rank
        p0 = jnp.sum(jnp.where(eid == e0, slot, 0.0), axis=0, keepdims=True)
        p1 = jnp.sum(jnp.where(eid == e1, slot, 0.0), axis=0, keepdims=True)
        pad = jnp.zeros((6, PLAN_BLOCK), jnp.int32)
        pos_ref[:, PLAN_BLOCK * b:PLAN_BLOCK * (b + 1)] = jnp.concatenate(
            [p0.astype(jnp.int32), p1.astype(jnp.int32), pad], axis=0)

    tile = lax.broadcasted_iota(jnp.int32, (1, LANES), 1).astype(F32)
    total = jnp.sum(ntile, axis=0, keepdims=True)
    expert_of_tile = jnp.sum(jnp.where(first_tile <= tile, 1.0, 0.0), axis=0, keepdims=True) - 1.0
    valid = jnp.where(tile < total, 1.0, 0.0)
    ends_expert = jnp.sum(jnp.where((first_tile + ntile == tile + 1.0) & (ntile > 0.0), 1.0, 0.0),
                          axis=0, keepdims=True)
    needs_zero = jnp.where((ends_expert > 0.0) | (valid == 0.0), 1.0, 0.0)
    zrow = jnp.zeros((5, LANES), jnp.int32)
    info_ref[...] = jnp.concatenate(
        [expert_of_tile.astype(jnp.int32), valid.astype(jnp.int32), needs_zero.astype(jnp.int32),
         zrow], axis=0)


def _moe_plan(route, tri, *, n_tiles):
    n = route.shape[1]
    return pl.pallas_call(
        functools.partial(_moe_plan_kernel, n=n, n_tiles=n_tiles),
        grid=(1,),
        in_specs=[pl.BlockSpec((8, n), lambda i: (0, 0)),
                  pl.BlockSpec((PLAN_BLOCK, PLAN_BLOCK), lambda i: (0, 0))],
        out_specs=[pl.BlockSpec((8, n), lambda i: (0, 0)),
                   pl.BlockSpec((8, LANES), lambda i: (0, 0))],
        out_shape=[jax.ShapeDtypeStruct((8, n), jnp.int32),
                   jax.ShapeDtypeStruct((8, LANES), jnp.int32)],
        compiler_params=_params(),
        name="moe_plan",
    )(route, tri)


def _token_tile(ref, t):
    return ref.at[pl.ds(pl.multiple_of(t * TOKEN_TILE_ROWS, TOKEN_TILE_ROWS), TOKEN_TILE_ROWS)]


def _token_tiles(ref, first, count):
    rows = count * TOKEN_TILE_ROWS
    return ref.at[pl.ds(pl.multiple_of(first * TOKEN_TILE_ROWS, TOKEN_TILE_ROWS), rows)]


def _dispatch_kernel(zero_ref, pos_ref, h_ref, xs_hbm, zbuf, zsem, sem, *, n_tiles):
    i = pl.program_id(0)

    @pl.when(i == 0)
    def _():
        zbuf[...] = jnp.zeros_like(zbuf)

        def fill_copy(j):
            return pltpu.make_async_copy(zbuf, _token_tiles(xs_hbm, j * EXPERT_TILE, EXPERT_TILE), zsem)

        def fill(j, c):
            @pl.when(zero_ref[j] == 1)
            def _():
                fill_copy(j).start()
            return c

        def drain(j, c):
            @pl.when(zero_ref[j] == 1)
            def _():
                fill_copy(j).wait()
            return c

        lax.fori_loop(0, n_tiles, fill, 0)
        lax.fori_loop(0, n_tiles, drain, 0)

    def scatter(r, c):
        src = _token_tile(h_ref, r)
        for s in range(2):
            pltpu.make_async_copy(src, _token_tile(xs_hbm, pos_ref[s, r]), sem).start()
        return c

    lax.fori_loop(0, PLAN_BLOCK, scatter, 0, unroll=8)
    for _ in range(2):
        pltpu.make_async_copy(h_ref, _token_tiles(xs_hbm, 0, PLAN_BLOCK), sem).wait()


def _dispatch(needs_zero, pos, h2t, *, n_tiles):
    n = h2t.shape[0] // TOKEN_TILE_ROWS
    return pl.pallas_call(
        functools.partial(_dispatch_kernel, n_tiles=n_tiles),
        grid_spec=pltpu.PrefetchScalarGridSpec(
            num_scalar_prefetch=1,
            grid=(n // PLAN_BLOCK,),
            in_specs=[pl.BlockSpec((8, PLAN_BLOCK), lambda i, z: (0, i), memory_space=pltpu.SMEM),
                      pl.BlockSpec((PLAN_BLOCK * TOKEN_TILE_ROWS, LANES), lambda i, z: (i, 0))],
            out_specs=pl.BlockSpec(memory_space=pl.ANY),
            scratch_shapes=[pltpu.VMEM((EXPERT_TILE * TOKEN_TILE_ROWS, LANES), F32),
                            pltpu.SemaphoreType.DMA(()), pltpu.SemaphoreType.DMA(())]),
        out_shape=jax.ShapeDtypeStruct((n_tiles * EXPERT_TILE * TOKEN_TILE_ROWS, LANES), F32),
        compiler_params=_params(),
        name="moe_dispatch",
    )(needs_zero, pos, h2t)


def _experts_kernel(expert_ref, valid_ref, x_ref, w1_ref, w3_ref, w2_ref, o_ref):
    j = pl.program_id(0)

    @pl.when(valid_ref[j] == 1)
    def _():
        x = _load_token_tiles(x_ref, EXPERT_TILE).astype(BF16)
        a = _dot(x, w1_ref[0])
        b = _dot(x, w3_ref[0])
        hid = (a * _sigmoid(a)) * b
        _store_token_tiles(o_ref, _dot(hid.astype(BF16), w2_ref[0]))

    @pl.when(valid_ref[j] == 0)
    def _():
        o_ref[...] = jnp.zeros_like(o_ref)


def _experts(expert_of_tile, valid, xs, w1, w3, w2, layer, *, n_tiles):
    rows = EXPERT_TILE * TOKEN_TILE_ROWS
    wmap = lambda j, e, v: (layer * N_EXPERTS + e[j], 0, 0)
    return pl.pallas_call(
        _experts_kernel,
        grid_spec=pltpu.PrefetchScalarGridSpec(
            num_scalar_prefetch=2,
            grid=(n_tiles,),
            in_specs=[pl.BlockSpec((rows, LANES), lambda j, e, v: (j, 0)),
                      pl.BlockSpec((1, D_MODEL, EXPERT_FF), wmap),
                      pl.BlockSpec((1, D_MODEL, EXPERT_FF), wmap),
                      pl.BlockSpec((1, EXPERT_FF, D_MODEL), wmap)],
            out_specs=pl.BlockSpec((rows, LANES), lambda j, e, v: (j, 0))),
        out_shape=jax.ShapeDtypeStruct((n_tiles * rows, LANES), F32),
        compiler_params=_params(),
        name="moe_experts",
    )(expert_of_tile, valid, xs, w1, w3, w2)


def _combine_kernel(pos_ref, ys_hbm, wts_ref, x1_ref, mod_ref, fg_ref, o_ref, buf, sem, *, final):
    def gather(r, c):
        for s in range(2):
            pltpu.make_async_copy(_token_tile(ys_hbm, pos_ref[s, r]), _token_tile(buf.at[s], r),
                                  sem).start()
        return c

    lax.fori_loop(0, PLAN_BLOCK, gather, 0, unroll=8)
    for s in range(2):
        pltpu.make_async_copy(_token_tiles(ys_hbm, 0, PLAN_BLOCK), buf.at[s], sem).wait()
    y0 = _load_token_tiles(buf, PLAN_BLOCK, lead=(0,))
    y1 = _load_token_tiles(buf, PLAN_BLOCK, lead=(1,))
    wts = wts_ref[...]
    moe = wts[:, 0:1] * y0 + wts[:, 1:2] * y1
    x2 = x1_ref[...] + mod_ref[0, 0, 5:6, :] * moe
    if final:
        ms = jnp.mean(x2 * x2, axis=-1, keepdims=True)
        x2 = x2 * lax.rsqrt(ms + EPS) * fg_ref[...]
    o_ref[...] = x2


def _combine(pos, ys, wts, x1, mod, layer, mod_row, fg, *, final):
    n = x1.shape[0]
    row = lambda i: (i, 0)
    return pl.pallas_call(
        functools.partial(_combine_kernel, final=final),
        grid=(n // PLAN_BLOCK,),
        in_specs=[
            pl.BlockSpec((8, PLAN_BLOCK), lambda i: (0, i), memory_space=pltpu.SMEM),
            pl.BlockSpec(memory_space=pl.ANY),
            pl.BlockSpec((PLAN_BLOCK, LANES), row),
            pl.BlockSpec((PLAN_BLOCK, D_MODEL), row),
            pl.BlockSpec((1, 1, N_MOD, D_MODEL), lambda i: (layer, mod_row(i), 0, 0)),
            pl.BlockSpec((1, D_MODEL), lambda i: (0, 0)),
        ],
        out_specs=pl.BlockSpec((PLAN_BLOCK, D_MODEL), row),
        out_shape=jax.ShapeDtypeStruct((n, D_MODEL), F32),
        scratch_shapes=[pltpu.VMEM((2, PLAN_BLOCK * TOKEN_TILE_ROWS, LANES), F32),
                        pltpu.SemaphoreType.DMA(())],
        compiler_params=_params(),
        name="moe_combine",
    )(pos, ys, wts, x1, mod, fg)


def _moe(h2t, route, wts, x1, mod, layer, mod_row, w1, w3, w2, fg, tri, *, final):
    n = x1.shape[0]
    n_tiles = 2 * n // EXPERT_TILE + N_EXPERTS
    pos, info = _moe_plan(route, tri, n_tiles=n_tiles)
    expert_of_tile, valid, needs_zero = info[0, :n_tiles], info[1, :n_tiles], info[2, :n_tiles]
    xs = _dispatch(needs_zero, pos, h2t, n_tiles=n_tiles)
    ys = _experts(expert_of_tile, valid, xs, w1, w3, w2, layer, n_tiles=n_tiles)
    return _combine(pos, ys, wts, x1, mod, layer, mod_row, fg, final=final)


def _rope_tables(n_tokens, rot_dim):
    pos = np.arange(n_tokens)
    r = (pos // GRID_W).astype(np.float64)[:, None]
    col = (pos % GRID_W).astype(np.float64)[:, None]
    n_freq = rot_dim // 4
    inv_freq = ROPE_THETA ** (-np.arange(n_freq, dtype=np.float64) / n_freq)
    ang = np.concatenate([r * inv_freq, col * inv_freq], axis=-1)
    ang = np.repeat(ang, 2, axis=-1)
    sign = np.tile(np.array([-1.0, 1.0]), rot_dim // 2)
    cos = np.tile(np.cos(ang), (1, LANES // rot_dim))
    sin = np.tile(np.sin(ang) * sign, (1, LANES // rot_dim))
    return jnp.asarray(cos, F32), jnp.asarray(sin, F32)


def _block_diag_ones():
    idx = np.arange(LANES) // 64
    return jnp.asarray((idx[:, None] == idx[None, :]).astype(np.float32), BF16)


def _pair_heads_by_kv_group(w, axis):
    shape = w.shape
    lead, tail = shape[:axis], shape[axis + 1:]
    w = w.reshape(lead + (A_KV_HEADS, A_HEADS // A_KV_HEADS, A_HEAD_DIM) + tail)
    w = jnp.swapaxes(w, axis, axis + 1)
    return w.reshape(shape)


def kernel(x_prompt, x_sample, cache_gqa_k, cache_gqa_v, cache_diff_k, cache_diff_v, c, c_ctx, w_in,
           w_out, norm1_g, norm2_g, ada_w, ada_b, gqa_qnorm_g, gqa_knorm_g, conv_w, diff_lambda,
           diff_subln_g, router_w, router_bias, moe_w1, moe_w3, moe_w2, final_g):
    n_ctx_b, t_ctx, _ = x_prompt.shape
    n_lat_b, t_lat, _ = x_sample.shape
    past = cache_gqa_k.shape[2]
    n_q = A_HEADS * A_HEAD_DIM

    cvec = jnp.concatenate([c_ctx[None, :], c, jnp.zeros((8 - 1 - n_lat_b, D_MODEL), F32)], axis=0)
    mod = _adaln(cvec, ada_w, ada_b).reshape(DEPTH, 8, N_MOD, D_MODEL)

    w_in_b = jnp.concatenate(
        [_pair_heads_by_kv_group(w_in[:, :, :n_q], 2), w_in[:, :, n_q:]], axis=2).astype(BF16)
    w_out_b = jnp.concatenate(
        [_pair_heads_by_kv_group(w_out[:, :n_q, :], 1), w_out[:, n_q:, :]], axis=1).astype(BF16)
    w1_b = moe_w1.astype(BF16).reshape(DEPTH * N_EXPERTS, D_MODEL, EXPERT_FF)
    w3_b = moe_w3.astype(BF16).reshape(DEPTH * N_EXPERTS, D_MODEL, EXPERT_FF)
    w2_b = moe_w2.astype(BF16).reshape(DEPTH * N_EXPERTS, EXPERT_FF, D_MODEL)
    rw = jnp.pad(router_w, ((0, 0), (0, LANES - N_EXPERTS)))
    rwh = rw.astype(BF16)
    rwl = (rw - rwh.astype(F32)).astype(BF16)
    rb = router_bias.reshape(N_EXPERTS, 1)
    bd = _block_diag_ones()
    tri = jnp.asarray(np.triu(np.ones((PLAN_BLOCK, PLAN_BLOCK), np.float32), k=1), BF16)
    fg = final_g.reshape(1, D_MODEL)
    tables = _rope_tables(t_lat, A_HEAD_DIM) + _rope_tables(t_lat, C_QK_DIM)

    ck_cache = cache_gqa_k.reshape(n_lat_b * DEPTH, past, A_KV_HEADS * A_HEAD_DIM)
    cv_cache = cache_gqa_v.reshape(n_lat_b * DEPTH, past, A_KV_HEADS * A_HEAD_DIM)
    dk_cache = cache_diff_k.reshape(n_lat_b * DEPTH, past, C_HEADS * 2 * C_QK_DIM)
    dv_cache = cache_diff_v.reshape(n_lat_b * DEPTH, past, C_HEADS * C_V_DIM)

    xp = x_prompt.reshape(n_ctx_b * t_ctx, D_MODEL)
    xs = x_sample.reshape(n_lat_b * t_lat, D_MODEL)
    tm = 512
    tq = 256
    new_state = []
    for l in range(DEPTH):
        lam_init = 0.8 - 0.6 * math.exp(-0.3 * l)
        g1 = norm1_g[l].reshape(1, D_MODEL)
        g2 = norm2_g[l].reshape(1, D_MODEL)
        qg = jnp.tile(gqa_qnorm_g[l], 2).reshape(1, LANES)
        kg = jnp.tile(gqa_knorm_g[l], 2).reshape(1, LANES)
        sg = jnp.tile(diff_subln_g[l], 2).reshape(1, LANES)
        final = l == DEPTH - 1

        def run_stream(x, n_batch, t_seq, mod_row_of, tabs, caches, nj_gqa, nj_diff, key_group):
            ctx = caches is None
            outs = _inproj(x, mod, l, lambda i: mod_row_of(i, tm), g1, w_in_b[l], qg, kg, bd, tabs,
                           tm=tm, ctx=ctx, key_group=key_group)
            q, k, vt, gb, gu, cq, ck, cvt = outs[:8]
            c_gk, c_gv, c_dk, c_dv = caches if caches is not None else (None,) * 4
            oa = _gqa_attention(q, k, vt, c_gk, c_gv, l, n_batch=n_batch, t_seq=t_seq, tq=tq,
                                nj=nj_gqa)
            oc = _diff_attention(cq, ck, cvt, c_dk, c_dv, l, diff_lambda, sg, n_batch=n_batch,
                                 t_seq=t_seq, tq=tq, nj=nj_diff, lam_init=lam_init)
            x1, h2t, route, wts = _outproj(oa, gb, gu, conv_w, oc, x, mod, l,
                                           lambda i: mod_row_of(i, tm), w_out_b[l], g2, rwh, rwl, rb,
                                           tm=tm, t_seq=t_seq)
            x2 = _moe(h2t, route, wts, x1, mod, l, lambda i: mod_row_of(i, PLAN_BLOCK), w1_b, w3_b,
                      w2_b, fg, tri, final=final)
            return x2, outs[8:]

        xp, state = run_stream(xp, n_ctx_b, t_ctx, lambda i, t: 0, None, None, 4, 2, t_ctx)
        new_state.append(state)
        xs, _ = run_stream(xs, n_lat_b, t_lat, lambda i, t: 1 + (i * t) // t_lat, tables,
                           (ck_cache, cv_cache, dk_cache, dv_cache), 1, 1, 512)

    def stack(k, heads, dim):
        return jnp.stack([s[k].reshape(n_ctx_b, t_ctx, heads, dim) for s in new_state], axis=1)

    return (xp.reshape(n_ctx_b, t_ctx, D_MODEL), xs.reshape(n_lat_b, t_lat, D_MODEL),
            stack(0, A_KV_HEADS, A_HEAD_DIM), stack(1, A_KV_HEADS, A_HEAD_DIM),
            stack(2, C_HEADS, 2 * C_QK_DIM), stack(3, C_HEADS, C_V_DIM))
```

```python
import functools
import math

import numpy as np
import jax
import jax.numpy as jnp
from jax import lax
from jax.experimental import pallas as pl
from jax.experimental.pallas import tpu as pltpu

D_MODEL = 1024
DEPTH = 2
GRID_W = 64
A_HEADS = 8
A_KV_HEADS = 2
A_HEAD_DIM = 64
B_WIDTH = 256
C_HEADS = 4
C_QK_DIM = 32
C_V_DIM = 64
N_EXPERTS = 16
EXPERTS_PER_GROUP = 4
N_EXPERT_GROUPS = 4
EXPERT_FF = 512
ROPE_THETA = 10000.0
EPS = 1e-6
N_MOD = 6
W_IN_WIDTH = 2304
LOG2E = math.log2(math.e)

LANES = 128
HEAD_ROWS = 64
ONES_ROWS = 16
VMEM_LIMIT_BYTES = 56 * 1024 * 1024

F32 = jnp.float32
BF16 = jnp.bfloat16

_dot = functools.partial(jnp.dot, preferred_element_type=F32)


def _dot_nt(a, b):
    return lax.dot_general(a, b, (((1,), (1,)), ((), ())), preferred_element_type=F32)


def _split(a):
    hi = a.astype(BF16)
    lo = (a - hi.astype(F32)).astype(BF16)
    return hi, lo


def _dot3(a, b):
    ah, al = _split(a)
    bh, bl = _split(b)
    return _dot(ah, bh) + _dot(ah, bl) + _dot(al, bh)


def _sigmoid(x):
    return 1.0 / (1.0 + jnp.exp(-x))


def _seg_rms(x, bd, g):
    ss = _dot((x * x).astype(BF16), bd)
    return x * lax.rsqrt(ss * (1.0 / 64.0) + EPS) * g


def _rope(x, cos, sin_signed):
    lane = lax.broadcasted_iota(jnp.int32, x.shape, 1)
    even = (lane & 1) == 0
    partner = jnp.where(even, pltpu.roll(x, LANES - 1, 1), pltpu.roll(x, 1, 1))
    return x * cos + partner * sin_signed


def _params():
    return pltpu.CompilerParams(vmem_limit_bytes=VMEM_LIMIT_BYTES)


def _adaln_kernel(c_ref, w_ref, b_ref, o_ref):
    c = c_ref[...]
    s = c * _sigmoid(c)
    o_ref[0] = _dot3(s, w_ref[0]) + b_ref[0]


def _adaln(cvec, ada_w, ada_b):
    tn = 1536
    nt = (N_MOD * D_MODEL) // tn
    return pl.pallas_call(
        _adaln_kernel,
        grid=(DEPTH, nt),
        in_specs=[
            pl.BlockSpec((8, D_MODEL), lambda l, n: (0, 0)),
            pl.BlockSpec((1, D_MODEL, tn), lambda l, n: (l, 0, n)),
            pl.BlockSpec((1, 1, tn), lambda l, n: (l, 0, n)),
        ],
        out_specs=pl.BlockSpec((1, 8, tn), lambda l, n: (l, 0, n)),
        out_shape=jax.ShapeDtypeStruct((DEPTH, 8, N_MOD * D_MODEL), F32),
        compiler_params=_params(),
        name="adaln",
    )(cvec, ada_w, ada_b.reshape(DEPTH, 1, N_MOD * D_MODEL))


def _inproj_kernel(*refs, rope, ctx, kg):
    it = iter(refs)
    x_ref, mod_ref, g1_ref, w_ref, qg_ref, kg_ref, bd_ref = (next(it) for _ in range(7))
    if rope:
        cos_a, sin_a, cos_c, sin_c = (next(it) for _ in range(4))
    q_ref, k_ref, vt_ref, gb_ref, gu_ref, cq_ref, ck_ref, cvt_ref = (next(it) for _ in range(8))
    if ctx:
        k32_ref, v32_ref, ck32_ref, cv32_ref = (next(it) for _ in range(4))

    x = x_ref[...]
    ms = jnp.mean(x * x, axis=-1, keepdims=True)
    y = x * lax.rsqrt(ms + EPS) * g1_ref[...]
    h = y * (1.0 + mod_ref[0, 0, 1:2, :]) + mod_ref[0, 0, 0:1, :]
    hb = h.astype(BF16)
    bd = bd_ref[...]

    def seg(lo, width):
        return _dot(hb, w_ref[:, lo:lo + width])

    def store_transposed(dst, a):
        at = a.T.astype(BF16)
        for c in range(a.shape[0] // kg):
            dst[c] = at[:, kg * c:kg * (c + 1)]

    halves = lambda a: (a[:, :LANES], a[:, LANES:])

    for jj in range(2):
        qq = _seg_rms(seg(2 * LANES * jj, 2 * LANES), bd, qg_ref[...])
        for j, q in zip((2 * jj, 2 * jj + 1), halves(qq)):
            if rope:
                q = _rope(q, cos_a[...], sin_a[...])
            q_ref[j] = (q * (A_HEAD_DIM ** -0.5 * LOG2E)).astype(BF16)

    kv = seg(512, 2 * LANES)
    k, v = halves(_seg_rms(kv, bd, kg_ref[...]))[0], kv[:, LANES:]
    if rope:
        k = _rope(k, cos_a[...], sin_a[...])
    if ctx:
        k32_ref[...] = k
        v32_ref[...] = v
    k_ref[0] = k.astype(BF16)
    store_transposed(vt_ref.at[0], v)

    gb_ref[...] = seg(768, B_WIDTH).astype(BF16)
    gu_ref[...] = (seg(1024, B_WIDTH) * seg(1280, B_WIDTH)).astype(BF16)

    cq2, ck2, cv2 = seg(1536, 2 * LANES), seg(1792, 2 * LANES), seg(2048, 2 * LANES)
    if ctx:
        ck32_ref[...] = ck2
        cv32_ref[...] = cv2
    for j, (cq, ck, cv) in enumerate(zip(halves(cq2), halves(ck2), halves(cv2))):
        if rope:
            cq = _rope(cq, cos_c[...], sin_c[...])
            ck = _rope(ck, cos_c[...], sin_c[...])
        cq_ref[j] = (cq * (C_QK_DIM ** -0.5 * LOG2E)).astype(BF16)
        ck_ref[j] = ck.astype(BF16)
        store_transposed(cvt_ref.at[j], cv)


def _inproj(x, mod, layer, mod_row, g1, w_in, qg, kg, bd, tables, *, tm, ctx, key_group):
    n = x.shape[0]
    rope = tables is not None
    row = lambda i: (i, 0)
    const2 = lambda i: (0, 0)
    pair3 = lambda i: (0, i, 0)
    tile4 = lambda i: (0, i, 0, 0)
    nc = tm // key_group
    in_specs = [
        pl.BlockSpec((tm, D_MODEL), row),
        pl.BlockSpec((1, 1, N_MOD, D_MODEL), lambda i: (layer, mod_row(i), 0, 0)),
        pl.BlockSpec((1, D_MODEL), const2),
        pl.BlockSpec((D_MODEL, W_IN_WIDTH), const2),
        pl.BlockSpec((1, 2 * LANES), const2),
        pl.BlockSpec((1, 2 * LANES), const2),
        pl.BlockSpec((2 * LANES, 2 * LANES), const2),
    ]
    args = [x, mod, g1, w_in, qg, kg, bd]
    if rope:
        t = tables[0].shape[0]
        tab = lambda i: (i % (t // tm), 0)
        in_specs += [pl.BlockSpec((tm, LANES), tab)] * 4
        args += list(tables)
    out_specs = [
        pl.BlockSpec((4, tm, LANES), pair3),
        pl.BlockSpec((1, tm, LANES), pair3),
        pl.BlockSpec((1, nc, LANES, key_group), tile4),
        pl.BlockSpec((tm, B_WIDTH), row),
        pl.BlockSpec((tm, B_WIDTH), row),
        pl.BlockSpec((2, tm, LANES), pair3),
        pl.BlockSpec((2, tm, LANES), pair3),
        pl.BlockSpec((2, nc, LANES, key_group), tile4),
    ]
    out_shape = [
        jax.ShapeDtypeStruct((4, n, LANES), BF16),
        jax.ShapeDtypeStruct((1, n, LANES), BF16),
        jax.ShapeDtypeStruct((1, n // key_group, LANES, key_group), BF16),
        jax.ShapeDtypeStruct((n, B_WIDTH), BF16),
        jax.ShapeDtypeStruct((n, B_WIDTH), BF16),
        jax.ShapeDtypeStruct((2, n, LANES), BF16),
        jax.ShapeDtypeStruct((2, n, LANES), BF16),
        jax.ShapeDtypeStruct((2, n // key_group, LANES, key_group), BF16),
    ]
    if ctx:
        out_specs += [
            pl.BlockSpec((tm, LANES), row),
            pl.BlockSpec((tm, LANES), row),
            pl.BlockSpec((tm, 2 * LANES), row),
            pl.BlockSpec((tm, 2 * LANES), row),
        ]
        out_shape += [
            jax.ShapeDtypeStruct((n, LANES), F32),
            jax.ShapeDtypeStruct((n, LANES), F32),
            jax.ShapeDtypeStruct((n, 2 * LANES), F32),
            jax.ShapeDtypeStruct((n, 2 * LANES), F32),
        ]
    return pl.pallas_call(
        functools.partial(_inproj_kernel, rope=rope, ctx=ctx, kg=key_group),
        grid=(n // tm,),
        in_specs=in_specs,
        out_specs=out_specs,
        out_shape=out_shape,
        compiler_params=_params(),
        name="inproj_ctx" if ctx else "inproj_lat",
    )(*args)


def _attend(q_masks, halves, k_ref, kj, vt_ref, vj, cache):
    tq = q_masks[0].shape[0]
    nu = len(q_masks)
    kg = vt_ref.shape[-1]
    n_groups = k_ref.shape[1] // kg
    loaders = []
    if cache is not None:
        kc, vc = cache
        loaders.append(lambda: (kc.astype(BF16), vc.T.astype(BF16)))
    for g in range(n_groups):
        loaders.append(lambda g=g: (k_ref[kj, kg * g:kg * (g + 1), :], vt_ref[vj, g]))

    def value_rows(vt, u):
        own = vt[HEAD_ROWS * halves[u]:HEAD_ROWS * (halves[u] + 1), :]
        return jnp.concatenate([own, jnp.ones((ONES_ROWS, vt.shape[1]), BF16)], axis=0)

    ms = [jnp.full((1, tq), -jnp.inf, F32)] * nu
    accs = [jnp.zeros((HEAD_ROWS + ONES_ROWS, tq), F32)] * nu
    k, vt = loaders[0]()
    scores = [_dot_nt(k, q_masks[u]) for u in range(nu)]
    for g in range(len(loaders)):
        cur_scores, cur_vt = scores, vt
        if g + 1 < len(loaders):
            k, vt = loaders[g + 1]()
            scores = [_dot_nt(k, q_masks[u]) for u in range(nu)]
        new_ms, ps = [], []
        for u in range(nu):
            m = jnp.maximum(ms[u], jnp.max(cur_scores[u], axis=0, keepdims=True))
            new_ms.append(m)
            ps.append(jnp.exp2(cur_scores[u] - m).astype(BF16))
        pvs = [_dot(value_rows(cur_vt, u), ps[u]) for u in range(nu)]
        accs = [jnp.exp2(ms[u] - new_ms[u]) * accs[u] + pvs[u] for u in range(nu)]
        ms = new_ms
    return [acc[0:HEAD_ROWS] / acc[HEAD_ROWS:HEAD_ROWS + 1] for acc in accs]


def _lane_mask(q, lo, width):
    lane = lax.broadcasted_iota(jnp.int32, q.shape, 1)
    return jnp.where((lane >= lo) & (lane < lo + width), q, jnp.zeros_like(q))


def _gqa_kernel(*refs, has_cache, nj):
    if has_cache:
        q_ref, k_ref, vt_ref, ck_ref, cv_ref, o_ref = refs
        cache = (ck_ref[0], cv_ref[0])
    else:
        q_ref, k_ref, vt_ref, o_ref = refs
        cache = None
    for j in range(nj):
        q = q_ref[j]
        q_masks = [_lane_mask(q, 0, 64), _lane_mask(q, 64, 64)]
        o0, o1 = _attend(q_masks, (0, 1), k_ref, 0, vt_ref, 0, cache)
        o_ref[:, LANES * j:LANES * (j + 1)] = jnp.concatenate([o0, o1], axis=0).T.astype(BF16)


def _gqa_attention(q, k, vt, cache_k, cache_v, layer, *, n_batch, t_seq, tq, nj):
    n = q.shape[1]
    nq = t_seq // tq
    kg = vt.shape[-1]
    has_cache = cache_k is not None
    in_specs = [
        pl.BlockSpec((nj, tq, LANES), lambda b, i, j: (j, b * nq + i, 0)),
        pl.BlockSpec((1, t_seq, LANES), lambda b, i, j: (0, b, 0)),
        pl.BlockSpec((1, t_seq // kg, LANES, kg), lambda b, i, j: (0, b, 0, 0)),
    ]
    args = [q, k, vt]
    if has_cache:
        past = cache_k.shape[1]
        cspec = pl.BlockSpec((1, past, LANES), lambda b, i, j: (b * DEPTH + layer, 0, 0))
        in_specs += [cspec, cspec]
        args += [cache_k, cache_v]
    return pl.pallas_call(
        functools.partial(_gqa_kernel, has_cache=has_cache, nj=nj),
        grid=(n_batch, nq, 4 // nj),
        in_specs=in_specs,
        out_specs=pl.BlockSpec((tq, LANES * nj), lambda b, i, j: (b * nq + i, j)),
        out_shape=jax.ShapeDtypeStruct((n, 4 * LANES), BF16),
        compiler_params=_params(),
        name="gqa_lat" if has_cache else "gqa_ctx",
    )(*args)


def _diff_kernel(*refs, has_cache, nj, lam_init):
    if has_cache:
        q_ref, k_ref, vt_ref, ck_ref, cv_ref, lam_ref, g_ref, o_ref = refs
    else:
        q_ref, k_ref, vt_ref, lam_ref, g_ref, o_ref = refs
    lv = lam_ref[0]
    lam = (jnp.exp(jnp.sum(lv[0:1] * lv[1:2], axis=-1, keepdims=True))
           - jnp.exp(jnp.sum(lv[2:3] * lv[3:4], axis=-1, keepdims=True)) + lam_init)
    for j in range(nj):
        q = q_ref[j]
        q_masks = [_lane_mask(q, 32 * c, 32) for c in range(4)]
        cache = None
        if has_cache:
            cache = (ck_ref[0, :, LANES * j:LANES * (j + 1)], cv_ref[0, :, LANES * j:LANES * (j + 1)])
        res = _attend(q_masks, (0, 0, 1, 1), k_ref, j, vt_ref, j, cache)
        heads = []
        for i in range(2):
            o = res[2 * i] - lam * res[2 * i + 1]
            ms = jnp.mean(o * o, axis=0, keepdims=True)
            heads.append(o * lax.rsqrt(ms + EPS))
        o = jnp.concatenate(heads, axis=0).T * g_ref[...] * (1.0 - lam_init)
        o_ref[:, LANES * j:LANES * (j + 1)] = o.astype(BF16)


def _diff_attention(cq, ck, cvt, cache_k, cache_v, layer, diff_lambda, subln_g, *,
                    n_batch, t_seq, tq, nj, lam_init):
    n = cq.shape[1]
    nq = t_seq // tq
    kg = cvt.shape[-1]
    has_cache = cache_k is not None
    in_specs = [
        pl.BlockSpec((nj, tq, LANES), lambda b, i, j: (j, b * nq + i, 0)),
        pl.BlockSpec((nj, t_seq, LANES), lambda b, i, j: (j, b, 0)),
        pl.BlockSpec((nj, t_seq // kg, LANES, kg), lambda b, i, j: (j, b, 0, 0)),
    ]
    args = [cq, ck, cvt]
    if has_cache:
        past = cache_k.shape[1]
        cspec = pl.BlockSpec((1, past, LANES * nj), lambda b, i, j: (b * DEPTH + layer, 0, j))
        in_specs += [cspec, cspec]
        args += [cache_k, cache_v]
    in_specs += [
        pl.BlockSpec((1, 4, C_QK_DIM), lambda b, i, j: (layer, 0, 0)),
        pl.BlockSpec((1, LANES), lambda b, i, j: (0, 0)),
    ]
    args += [diff_lambda, subln_g]
    return pl.pallas_call(
        functools.partial(_diff_kernel, has_cache=has_cache, nj=nj, lam_init=lam_init),
        grid=(n_batch, nq, 2 // nj),
        in_specs=in_specs,
        out_specs=pl.BlockSpec((tq, LANES * nj), lambda b, i, j: (b * nq + i, j)),
        out_shape=jax.ShapeDtypeStruct((n, 2 * LANES), BF16),
        compiler_params=_params(),
        name="diff_lat" if has_cache else "diff_ctx",
    )(*args)


HALO = 16


def _route(logits_t, bias_col):
    scores = _sigmoid(logits_t)
    sel = scores + bias_col
    row = lambda a, e: a[e:e + 1, :]
    grp = []
    for g in range(N_EXPERT_GROUPS):
        s0, s1, s2, s3 = (row(sel, 4 * g + k) for k in range(4))
        hi1, lo1 = jnp.maximum(s0, s1), jnp.minimum(s0, s1)
        hi2, lo2 = jnp.maximum(s2, s3), jnp.minimum(s2, s3)
        grp.append(jnp.maximum(hi1, hi2) + jnp.maximum(jnp.minimum(hi1, hi2), jnp.maximum(lo1, lo2)))
    best = grp[0]
    tg = jnp.zeros_like(best, dtype=jnp.int32)
    for g in range(1, N_EXPERT_GROUPS):
        better = grp[g] > best
        best = jnp.where(better, grp[g], best)
        tg = jnp.where(better, g, tg)

    def in_group(a, k):
        out = row(a, k)
        for g in range(1, N_EXPERT_GROUPS):
            out = jnp.where(tg == g, row(a, 4 * g + k), out)
        return out

    vals = [in_group(sel, k) for k in range(EXPERTS_PER_GROUP)]
    scs = [in_group(scores, k) for k in range(EXPERTS_PER_GROUP)]

    def first_argmax(vs):
        bv, bi = vs[0], jnp.zeros_like(tg)
        for k in range(1, len(vs)):
            better = vs[k] > bv
            bv = jnp.where(better, vs[k], bv)
            bi = jnp.where(better, k, bi)
        return bi

    i1 = first_argmax(vals)
    i2 = first_argmax([jnp.where(i1 == k, -jnp.inf, vals[k]) for k in range(EXPERTS_PER_GROUP)])

    def pick(vs, idx):
        out = vs[0]
        for k in range(1, len(vs)):
            out = jnp.where(idx == k, vs[k], out)
        return out

    w1, w2 = pick(scs, i1), pick(scs, i2)
    tot = w1 + w2
    return 4 * tg + i1, 4 * tg + i2, w1 / tot, w2 / tot


def _outproj_kernel(oa_ref, gb_ref, gu_ref, gup_ref, gun_ref, cw_ref, oc_ref, x_ref, mod_ref,
                    w_ref, g2_ref, rwt_ref, rb_ref,
                    x1_ref, h2t_ref, route_ref, wts_ref, *, tm, t_seq):
    i = pl.program_id(0)
    gu = gu_ref[...].astype(F32)
    rowi = lax.broadcasted_iota(jnp.int32, gu.shape, 0)
    pos = (i * tm + rowi) % t_seq
    prev = pltpu.roll(gu, 1, 0)
    prev = jnp.where(rowi == 0, gup_ref[HALO - 1:HALO, :].astype(F32), prev)
    prev = jnp.where(pos == 0, 0.0, prev)
    nxt = pltpu.roll(gu, tm - 1, 0)
    nxt = jnp.where(rowi == tm - 1, gun_ref[0:1, :].astype(F32), nxt)
    nxt = jnp.where(pos == t_seq - 1, 0.0, nxt)
    cw = cw_ref[0]
    ob = gb_ref[...].astype(F32) * (prev * cw[0:1] + gu * cw[1:2] + nxt * cw[2:3])

    mix = (_dot(oa_ref[...], w_ref[0:512, :])
           + _dot(ob.astype(BF16), w_ref[512:768, :])
           + _dot(oc_ref[...], w_ref[768:1024, :]))
    x1 = x_ref[...] + mod_ref[0, 0, 2:3, :] * mix
    x1_ref[...] = x1
    ms = jnp.mean(x1 * x1, axis=-1, keepdims=True)
    h2 = x1 * lax.rsqrt(ms + EPS) * g2_ref[...]
    h2 = h2 * (1.0 + mod_ref[0, 0, 4:5, :]) + mod_ref[0, 0, 3:4, :]
    _store_token_tiles(h2t_ref, h2)

    hh, hl = _split(h2)
    rwt = rwt_ref[...]
    both = _dot_nt(rwt, hh)
    logits_t = both[0:N_EXPERTS] + both[N_EXPERTS:2 * N_EXPERTS] + _dot_nt(rwt[0:N_EXPERTS], hl)
    e0, e1, w0, w1 = _route(logits_t, rb_ref[...])
    zeros = jnp.zeros_like(w0)
    route_ref[...] = jnp.concatenate(
        [e0.astype(F32), e1.astype(F32), w0, w1, zeros, zeros, zeros, zeros], axis=0)
    srow = lax.broadcasted_iota(jnp.int32, (LANES, tm), 0)
    wts_t = jnp.where(srow == 0, w0, 0.0) + jnp.where(srow == 1, w1, 0.0)
    wts_ref[...] = wts_t.T


def _outproj(oa, gb, gu, conv_w, oc, x, mod, layer, mod_row, w_out, g2, rwt, rb, *, tm, t_seq):
    n = x.shape[0]
    row = lambda i: (i, 0)
    const2 = lambda i: (0, 0)
    hb = tm // HALO
    last = n // HALO - 1
    return pl.pallas_call(
        functools.partial(_outproj_kernel, tm=tm, t_seq=t_seq),
        grid=(n // tm,),
        in_specs=[
            pl.BlockSpec((tm, 4 * LANES), row),
            pl.BlockSpec((tm, B_WIDTH), row),
            pl.BlockSpec((tm, B_WIDTH), row),
            pl.BlockSpec((HALO, B_WIDTH), lambda i: (jnp.maximum(i * hb - 1, 0), 0)),
            pl.BlockSpec((HALO, B_WIDTH), lambda i: (jnp.minimum((i + 1) * hb, last), 0)),
            pl.BlockSpec((1, 3, B_WIDTH), lambda i: (layer, 0, 0)),
            pl.BlockSpec((tm, 2 * LANES), row),
            pl.BlockSpec((tm, D_MODEL), row),
            pl.BlockSpec((1, 1, N_MOD, D_MODEL), lambda i: (layer, mod_row(i), 0, 0)),
            pl.BlockSpec((D_MODEL, D_MODEL), const2),
            pl.BlockSpec((1, D_MODEL), const2),
            pl.BlockSpec((2 * N_EXPERTS, D_MODEL), const2),
            pl.BlockSpec((N_EXPERTS, 1), const2),
        ],
        out_specs=[
            pl.BlockSpec((tm, D_MODEL), row),
            pl.BlockSpec((tm * TOKEN_TILE_ROWS, LANES), row),
            pl.BlockSpec((8, tm), lambda i: (0, i)),
            pl.BlockSpec((tm, LANES), row),
        ],
        out_shape=[
            jax.ShapeDtypeStruct((n, D_MODEL), F32),
            jax.ShapeDtypeStruct((n * TOKEN_TILE_ROWS, LANES), F32),
            jax.ShapeDtypeStruct((8, n), F32),
            jax.ShapeDtypeStruct((n, LANES), F32),
        ],
        compiler_params=_params(),
        name="outproj",
    )(oa, gb, gu, gu, gu, conv_w, oc, x, mod, w_out, g2, rwt, rb)


TOKEN_TILE_ROWS = D_MODEL // LANES
EXPERT_TILE = 256
PLAN_BLOCK = 512


def _load_token_tiles(ref, n_tokens, lead=()):
    return jnp.concatenate(
        [ref[lead + (pl.ds(c, n_tokens, stride=TOKEN_TILE_ROWS), slice(None))]
         for c in range(TOKEN_TILE_ROWS)], axis=1)


def _store_token_tiles(ref, x):
    n_tokens = x.shape[0]
    for c in range(TOKEN_TILE_ROWS):
        ref[pl.ds(c, n_tokens, stride=TOKEN_TILE_ROWS), :] = x[:, LANES * c:LANES * (c + 1)]


def _moe_plan_kernel(route_ref, tri_ref, pos_ref, info_ref, *, n, n_tiles):
    eid = lax.broadcasted_iota(jnp.int32, (N_EXPERTS, PLAN_BLOCK), 0)

    def block(b):
        sl = slice(PLAN_BLOCK * b, PLAN_BLOCK * (b + 1))
        e0 = route_ref[0:1, sl].astype(jnp.int32)
        e1 = route_ref[1:2, sl].astype(jnp.int32)
        return e0, e1, jnp.where((eid == e0) | (eid == e1), 1.0, 0.0)

    counts = jnp.zeros((N_EXPERTS, 1), F32)
    for b in range(n // PLAN_BLOCK):
        counts = counts + jnp.sum(block(b)[2], axis=1, keepdims=True)
    ntile = jnp.floor((counts + (EXPERT_TILE - 1)) * (1.0 / EXPERT_TILE))
    sub = lax.broadcasted_iota(jnp.int32, (N_EXPERTS, LANES), 0)
    lane = lax.broadcasted_iota(jnp.int32, (N_EXPERTS, LANES), 1)
    ntile_row = jnp.sum(jnp.where(sub == lane, ntile, 0.0), axis=0, keepdims=True)
    first_tile = jnp.sum(jnp.where(lane < sub, ntile_row, 0.0), axis=1, keepdims=True)
    first_row = first_tile * EXPERT_TILE

    carry = jnp.zeros((N_EXPERTS, 1), F32)
    for b in range(n // PLAN_BLOCK):
        e0, e1, mb = block(b)
        rank = _dot(mb.astype(BF16), tri_ref[...]) + carry
        carry = carry + jnp.sum(mb, axis=1, keepdims=True)
        slot = first_row + rank
        p0 = jnp.sum(jnp.where(eid == e0, slot, 0.0), axis=0, keepdims=True)
        p1 = jnp.sum(jnp.where(eid == e1, slot, 0.0), axis=0, keepdims=True)
        pad = jnp.zeros((6, PLAN_BLOCK), jnp.int32)
        pos_ref[:, PLAN_BLOCK * b:PLAN_BLOCK * (b + 1)] = jnp.concatenate(
            [p0.astype(jnp.int32), p1.astype(jnp.int32), pad], axis=0)

    tile = lax.broadcasted_iota(jnp.int32, (1, LANES), 1).astype(F32)
    total = jnp.sum(ntile, axis=0, keepdims=True)
    expert_of_tile = jnp.sum(jnp.where(first_tile <= tile, 1.0, 0.0), axis=0, keepdims=True) - 1.0
    valid = jnp.where(tile < total, 1.0, 0.0)
    ends_expert = jnp.sum(jnp.where((first_tile + ntile == tile + 1.0) & (ntile > 0.0), 1.0, 0.0),
                          axis=0, keepdims=True)
    needs_zero = jnp.where((ends_expert > 0.0) | (valid == 0.0), 1.0, 0.0)
    zrow = jnp.zeros((5, LANES), jnp.int32)
    info_ref[...] = jnp.concatenate(
        [expert_of_tile.astype(jnp.int32), valid.astype(jnp.int32), needs_zero.astype(jnp.int32),
         zrow], axis=0)


def _moe_plan(route, tri, *, n_tiles):
    n = route.shape[1]
    return pl.pallas_call(
        functools.partial(_moe_plan_kernel, n=n, n_tiles=n_tiles),
        grid=(1,),
        in_specs=[pl.BlockSpec((8, n), lambda i: (0, 0)),
                  pl.BlockSpec((PLAN_BLOCK, PLAN_BLOCK), lambda i: (0, 0))],
        out_specs=[pl.BlockSpec((8, n), lambda i: (0, 0)),
                   pl.BlockSpec((8, LANES), lambda i: (0, 0))],
        out_shape=[jax.ShapeDtypeStruct((8, n), jnp.int32),
                   jax.ShapeDtypeStruct((8, LANES), jnp.int32)],
        compiler_params=_params(),
        name="moe_plan",
    )(route, tri)


def _token_tile(ref, t):
    return ref.at[pl.ds(pl.multiple_of(t * TOKEN_TILE_ROWS, TOKEN_TILE_ROWS), TOKEN_TILE_ROWS)]


def _token_tiles(ref, first, count):
    rows = count * TOKEN_TILE_ROWS
    return ref.at[pl.ds(pl.multiple_of(first * TOKEN_TILE_ROWS, TOKEN_TILE_ROWS), rows)]


def _dispatch_kernel(zero_ref, pos_ref, h_ref, xs_hbm, zbuf, zsem, sem, *, n_tiles):
    i = pl.program_id(0)

    @pl.when(i == 0)
    def _():
        zbuf[...] = jnp.zeros_like(zbuf)

        def fill_copy(j):
            return pltpu.make_async_copy(zbuf, _token_tiles(xs_hbm, j * EXPERT_TILE, EXPERT_TILE), zsem)

        def fill(j, c):
            @pl.when(zero_ref[j] == 1)
            def _():
                fill_copy(j).start()
            return c

        def drain(j, c):
            @pl.when(zero_ref[j] == 1)
            def _():
                fill_copy(j).wait()
            return c

        lax.fori_loop(0, n_tiles, fill, 0)
        lax.fori_loop(0, n_tiles, drain, 0)

    def scatter(r, c):
        src = _token_tile(h_ref, r)
        for s in range(2):
            pltpu.make_async_copy(src, _token_tile(xs_hbm, pos_ref[s, r]), sem).start()
        return c

    lax.fori_loop(0, PLAN_BLOCK, scatter, 0, unroll=8)
    for _ in range(2):
        pltpu.make_async_copy(h_ref, _token_tiles(xs_hbm, 0, PLAN_BLOCK), sem).wait()


def _dispatch(needs_zero, pos, h2t, *, n_tiles):
    n = h2t.shape[0] // TOKEN_TILE_ROWS
    return pl.pallas_call(
        functools.partial(_dispatch_kernel, n_tiles=n_tiles),
        grid_spec=pltpu.PrefetchScalarGridSpec(
            num_scalar_prefetch=1,
            grid=(n // PLAN_BLOCK,),
            in_specs=[pl.BlockSpec((8, PLAN_BLOCK), lambda i, z: (0, i), memory_space=pltpu.SMEM),
                      pl.BlockSpec((PLAN_BLOCK * TOKEN_TILE_ROWS, LANES), lambda i, z: (i, 0))],
            out_specs=pl.BlockSpec(memory_space=pl.ANY),
            scratch_shapes=[pltpu.VMEM((EXPERT_TILE * TOKEN_TILE_ROWS, LANES), F32),
                            pltpu.SemaphoreType.DMA(()), pltpu.SemaphoreType.DMA(())]),
        out_shape=jax.ShapeDtypeStruct((n_tiles * EXPERT_TILE * TOKEN_TILE_ROWS, LANES), F32),
        compiler_params=_params(),
        name="moe_dispatch",
    )(needs_zero, pos, h2t)


def _experts_kernel(expert_ref, valid_ref, x_ref, w1_ref, w3_ref, w2_ref, o_ref):
    j = pl.program_id(0)

    @pl.when(valid_ref[j] == 1)
    def _():
        x = _load_token_tiles(x_ref, EXPERT_TILE).astype(BF16)
        a = _dot(x, w1_ref[0].astype(BF16))
        b = _dot(x, w3_ref[0].astype(BF16))
        hid = (a * _sigmoid(a)) * b
        _store_token_tiles(o_ref, _dot(hid.astype(BF16), w2_ref[0].astype(BF16)))

    @pl.when(valid_ref[j] == 0)
    def _():
        o_ref[...] = jnp.zeros_like(o_ref)


def _experts(expert_of_tile, valid, xs, w1, w3, w2, layer, *, n_tiles):
    rows = EXPERT_TILE * TOKEN_TILE_ROWS
    wmap = lambda j, e, v: (layer * N_EXPERTS + e[j], 0, 0)
    return pl.pallas_call(
        _experts_kernel,
        grid_spec=pltpu.PrefetchScalarGridSpec(
            num_scalar_prefetch=2,
            grid=(n_tiles,),
            in_specs=[pl.BlockSpec((rows, LANES), lambda j, e, v: (j, 0)),
                      pl.BlockSpec((1, D_MODEL, EXPERT_FF), wmap),
                      pl.BlockSpec((1, D_MODEL, EXPERT_FF), wmap),
                      pl.BlockSpec((1, EXPERT_FF, D_MODEL), wmap)],
            out_specs=pl.BlockSpec((rows, LANES), lambda j, e, v: (j, 0))),
        out_shape=jax.ShapeDtypeStruct((n_tiles * rows, LANES), F32),
        compiler_params=_params(),
        name="moe_experts",
    )(expert_of_tile, valid, xs, w1, w3, w2)


def _combine_kernel(pos_ref, ys_hbm, wts_ref, x1_ref, mod_ref, fg_ref, o_ref, buf, sem, *, final):
    def gather(r, c):
        for s in range(2):
            pltpu.make_async_copy(_token_tile(ys_hbm, pos_ref[s, r]), _token_tile(buf.at[s], r),
                                  sem).start()
        return c

    lax.fori_loop(0, PLAN_BLOCK, gather, 0, unroll=8)
    for s in range(2):
        pltpu.make_async_copy(_token_tiles(ys_hbm, 0, PLAN_BLOCK), buf.at[s], sem).wait()
    y0 = _load_token_tiles(buf, PLAN_BLOCK, lead=(0,))
    y1 = _load_token_tiles(buf, PLAN_BLOCK, lead=(1,))
    wts = wts_ref[...]
    moe = wts[:, 0:1] * y0 + wts[:, 1:2] * y1
    x2 = x1_ref[...] + mod_ref[0, 0, 5:6, :] * moe
    if final:
        ms = jnp.mean(x2 * x2, axis=-1, keepdims=True)
        x2 = x2 * lax.rsqrt(ms + EPS) * fg_ref[...]
    o_ref[...] = x2


def _combine(pos, ys, wts, x1, mod, layer, mod_row, fg, *, final):
    n = x1.shape[0]
    row = lambda i: (i, 0)
    return pl.pallas_call(
        functools.partial(_combine_kernel, final=final),
        grid=(n // PLAN_BLOCK,),
        in_specs=[
            pl.BlockSpec((8, PLAN_BLOCK), lambda i: (0, i), memory_space=pltpu.SMEM),
            pl.BlockSpec(memory_space=pl.ANY),
            pl.BlockSpec((PLAN_BLOCK, LANES), row),
            pl.BlockSpec((PLAN_BLOCK, D_MODEL), row),
            pl.BlockSpec((1, 1, N_MOD, D_MODEL), lambda i: (layer, mod_row(i), 0, 0)),
            pl.BlockSpec((1, D_MODEL), lambda i: (0, 0)),
        ],
        out_specs=pl.BlockSpec((PLAN_BLOCK, D_MODEL), row),
        out_shape=jax.ShapeDtypeStruct((n, D_MODEL), F32),
        scratch_shapes=[pltpu.VMEM((2, PLAN_BLOCK * TOKEN_TILE_ROWS, LANES), F32),
                        pltpu.SemaphoreType.DMA(())],
        compiler_params=_params(),
        name="moe_combine",
    )(pos, ys, wts, x1, mod, fg)


def _moe(h2t, route, wts, x1, mod, layer, mod_row, w1, w3, w2, fg, tri, *, final):
    n = x1.shape[0]
    n_tiles = 2 * n // EXPERT_TILE + N_EXPERTS
    pos, info = _moe_plan(route, tri, n_tiles=n_tiles)
    expert_of_tile, valid, needs_zero = info[0, :n_tiles], info[1, :n_tiles], info[2, :n_tiles]
    xs = _dispatch(needs_zero, pos, h2t, n_tiles=n_tiles)
    ys = _experts(expert_of_tile, valid, xs, w1, w3, w2, layer, n_tiles=n_tiles)
    return _combine(pos, ys, wts, x1, mod, layer, mod_row, fg, final=final)


def _rope_tables(n_tokens, rot_dim):
    pos = np.arange(n_tokens)
    r = (pos // GRID_W).astype(np.float64)[:, None]
    col = (pos % GRID_W).astype(np.float64)[:, None]
    n_freq = rot_dim // 4
    inv_freq = ROPE_THETA ** (-np.arange(n_freq, dtype=np.float64) / n_freq)
    ang = np.concatenate([r * inv_freq, col * inv_freq], axis=-1)
    ang = np.repeat(ang, 2, axis=-1)
    sign = np.tile(np.array([-1.0, 1.0]), rot_dim // 2)
    cos = np.tile(np.cos(ang), (1, LANES // rot_dim))
    sin = np.tile(np.sin(ang) * sign, (1, LANES // rot_dim))
    return jnp.asarray(cos, F32), jnp.asarray(sin, F32)


def _block_diag_ones():
    idx = np.arange(2 * LANES) // 64
    return jnp.asarray((idx[:, None] == idx[None, :]).astype(np.float32), BF16)


def _pair_heads_by_kv_group(w, axis):
    shape = w.shape
    lead, tail = shape[:axis], shape[axis + 1:]
    w = w.reshape(lead + (A_KV_HEADS, A_HEADS // A_KV_HEADS, A_HEAD_DIM) + tail)
    w = jnp.swapaxes(w, axis, axis + 1)
    return w.reshape(shape)


def kernel(x_prompt, x_sample, cache_gqa_k, cache_gqa_v, cache_diff_k, cache_diff_v, c, c_ctx, w_in,
           w_out, norm1_g, norm2_g, ada_w, ada_b, gqa_qnorm_g, gqa_knorm_g, conv_w, diff_lambda,
           diff_subln_g, router_w, router_bias, moe_w1, moe_w3, moe_w2, final_g):
    n_ctx_b, t_ctx, _ = x_prompt.shape
    n_lat_b, t_lat, _ = x_sample.shape
    past = cache_gqa_k.shape[2]
    n_q = A_HEADS * A_HEAD_DIM

    cvec = jnp.concatenate([c_ctx[None, :], c, jnp.zeros((8 - 1 - n_lat_b, D_MODEL), F32)], axis=0)
    mod = _adaln(cvec, ada_w, ada_b).reshape(DEPTH, 8, N_MOD, D_MODEL)

    w_in_b = jnp.concatenate(
        [_pair_heads_by_kv_group(w_in[:, :, :n_q], 2), w_in[:, :, n_q:]], axis=2).astype(BF16)
    w_out_b = jnp.concatenate(
        [_pair_heads_by_kv_group(w_out[:, :n_q, :], 1), w_out[:, n_q:, :]], axis=1).astype(BF16)
    w1_b = moe_w1.reshape(DEPTH * N_EXPERTS, D_MODEL, EXPERT_FF)
    w3_b = moe_w3.reshape(DEPTH * N_EXPERTS, D_MODEL, EXPERT_FF)
    w2_b = moe_w2.reshape(DEPTH * N_EXPERTS, EXPERT_FF, D_MODEL)
    rw_t = router_w.T
    rw_hi = rw_t.astype(BF16)
    rwt = jnp.concatenate([rw_hi, (rw_t - rw_hi.astype(F32)).astype(BF16)], axis=0)
    rb = router_bias.reshape(N_EXPERTS, 1)
    bd = _block_diag_ones()
    tri = jnp.asarray(np.triu(np.ones((PLAN_BLOCK, PLAN_BLOCK), np.float32), k=1), BF16)
    fg = final_g.reshape(1, D_MODEL)
    tables = _rope_tables(t_lat, A_HEAD_DIM) + _rope_tables(t_lat, C_QK_DIM)

    ck_cache = cache_gqa_k.reshape(n_lat_b * DEPTH, past, A_KV_HEADS * A_HEAD_DIM)
    cv_cache = cache_gqa_v.reshape(n_lat_b * DEPTH, past, A_KV_HEADS * A_HEAD_DIM)
    dk_cache = cache_diff_k.reshape(n_lat_b * DEPTH, past, C_HEADS * 2 * C_QK_DIM)
    dv_cache = cache_diff_v.reshape(n_lat_b * DEPTH, past, C_HEADS * C_V_DIM)

    xp = x_prompt.reshape(n_ctx_b * t_ctx, D_MODEL)
    xs = x_sample.reshape(n_lat_b * t_lat, D_MODEL)
    tm = 512
    tq = 256
    new_state = []
    for l in range(DEPTH):
        lam_init = 0.8 - 0.6 * math.exp(-0.3 * l)
        g1 = norm1_g[l].reshape(1, D_MODEL)
        g2 = norm2_g[l].reshape(1, D_MODEL)
        qg = jnp.tile(gqa_qnorm_g[l], 4).reshape(1, 2 * LANES)
        kg = jnp.tile(gqa_knorm_g[l], 4).reshape(1, 2 * LANES)
        sg = jnp.tile(diff_subln_g[l], 2).reshape(1, LANES)
        final = l == DEPTH - 1

        def run_stream(x, n_batch, t_seq, mod_row_of, tabs, caches, nj_gqa, nj_diff, key_group):
            ctx = caches is None
            outs = _inproj(x, mod, l, lambda i: mod_row_of(i, tm), g1, w_in_b[l], qg, kg, bd, tabs,
                           tm=tm, ctx=ctx, key_group=key_group)
            q, k, vt, gb, gu, cq, ck, cvt = outs[:8]
            c_gk, c_gv, c_dk, c_dv = caches if caches is not None else (None,) * 4
            oa = _gqa_attention(q, k, vt, c_gk, c_gv, l, n_batch=n_batch, t_seq=t_seq, tq=tq,
                                nj=nj_gqa)
            oc = _diff_attention(cq, ck, cvt, c_dk, c_dv, l, diff_lambda, sg, n_batch=n_batch,
                                 t_seq=t_seq, tq=tq, nj=nj_diff, lam_init=lam_init)
            x1, h2t, route, wts = _outproj(oa, gb, gu, conv_w, oc, x, mod, l,
                                           lambda i: mod_row_of(i, tm), w_out_b[l], g2, rwt, rb,
                                           tm=tm, t_seq=t_seq)
            x2 = _moe(h2t, route, wts, x1, mod, l, lambda i: mod_row_of(i, PLAN_BLOCK), w1_b, w3_b,
                      w2_b, fg, tri, final=final)
            return x2, outs[8:]

        xp, state = run_stream(xp, n_ctx_b, t_ctx, lambda i, t: 0, None, None, 4, 2, t_ctx)
        new_state.append(state)
        xs, _ = run_stream(xs, n_lat_b, t_lat, lambda i, t: 1 + (i * t) // t_lat, tables,
                           (ck_cache, cv_cache, dk_cache, dv_cache), 1, 1, 512)

    def stack(k, heads, dim):
        return jnp.stack([s[k].reshape(n_ctx_b, t_ctx, heads, dim) for s in new_state], axis=1)

    return (xp.reshape(n_ctx_b, t_ctx, D_MODEL), xs.reshape(n_lat_b, t_lat, D_MODEL),
            stack(0, A_KV_HEADS, A_HEAD_DIM), stack(1, A_KV_HEADS, A_HEAD_DIM),
            stack(2, C_HEADS, 2 * C_QK_DIM), stack(3, C_HEADS, C_V_DIM))
```

```python
import functools
import math

import numpy as np
import jax
import jax.numpy as jnp
from jax import lax
from jax.experimental import pallas as pl
from jax.experimental.pallas import tpu as pltpu

D_MODEL = 1024
DEPTH = 2
GRID_W = 64
A_HEADS = 8
A_KV_HEADS = 2
A_HEAD_DIM = 64
B_WIDTH = 256
C_HEADS = 4
C_QK_DIM = 32
C_V_DIM = 64
N_EXPERTS = 16
EXPERTS_PER_GROUP = 4
N_EXPERT_GROUPS = 4
EXPERT_FF = 512
ROPE_THETA = 10000.0
EPS = 1e-6
N_MOD = 6
W_IN_WIDTH = 2304
LOG2E = math.log2(math.e)

LANES = 128
HEAD_ROWS = 64
ONES_ROWS = 16
VMEM_LIMIT_BYTES = 56 * 1024 * 1024

F32 = jnp.float32
BF16 = jnp.bfloat16

_dot = functools.partial(jnp.dot, preferred_element_type=F32)


def _dot_nt(a, b):
    return lax.dot_general(a, b, (((1,), (1,)), ((), ())), preferred_element_type=F32)


def _split(a):
    hi = a.astype(BF16)
    lo = (a - hi.astype(F32)).astype(BF16)
    return hi, lo


def _dot3(a, b):
    ah, al = _split(a)
    bh, bl = _split(b)
    return _dot(ah, bh) + _dot(ah, bl) + _dot(al, bh)


def _sigmoid(x):
    return 1.0 / (1.0 + jnp.exp(-x))


def _seg_rms(x, bd, g):
    ss = _dot((x * x).astype(BF16), bd)
    return x * lax.rsqrt(ss * (1.0 / 64.0) + EPS) * g


def _rope(x, cos, sin_signed):
    lane = lax.broadcasted_iota(jnp.int32, x.shape, 1)
    even = (lane & 1) == 0
    partner = jnp.where(even, pltpu.roll(x, LANES - 1, 1), pltpu.roll(x, 1, 1))
    return x * cos + partner * sin_signed


def _params():
    return pltpu.CompilerParams(vmem_limit_bytes=VMEM_LIMIT_BYTES)


def _adaln_kernel(c_ref, w_ref, b_ref, o_ref):
    c = c_ref[...]
    s = c * _sigmoid(c)
    o_ref[0] = _dot3(s, w_ref[0]) + b_ref[0]


def _adaln(cvec, ada_w, ada_b):
    tn = 1536
    nt = (N_MOD * D_MODEL) // tn
    return pl.pallas_call(
        _adaln_kernel,
        grid=(DEPTH, nt),
        in_specs=[
            pl.BlockSpec((8, D_MODEL), lambda l, n: (0, 0)),
            pl.BlockSpec((1, D_MODEL, tn), lambda l, n: (l, 0, n)),
            pl.BlockSpec((1, 1, tn), lambda l, n: (l, 0, n)),
        ],
        out_specs=pl.BlockSpec((1, 8, tn), lambda l, n: (l, 0, n)),
        out_shape=jax.ShapeDtypeStruct((DEPTH, 8, N_MOD * D_MODEL), F32),
        compiler_params=_params(),
        name="adaln",
    )(cvec, ada_w, ada_b.reshape(DEPTH, 1, N_MOD * D_MODEL))


def _inproj_kernel(*refs, rope, ctx, kg):
    it = iter(refs)
    x_ref, mod_ref, g1_ref, w_ref, qg_ref, kg_ref, bd_ref = (next(it) for _ in range(7))
    if rope:
        cos_a, sin_a, cos_c, sin_c = (next(it) for _ in range(4))
    q_ref, k_ref, vt_ref, gb_ref, gu_ref, cq_ref, ck_ref, cvt_ref = (next(it) for _ in range(8))
    if ctx:
        k32_ref, v32_ref, ck32_ref, cv32_ref = (next(it) for _ in range(4))

    x = x_ref[...]
    ms = jnp.mean(x * x, axis=-1, keepdims=True)
    y = x * lax.rsqrt(ms + EPS) * g1_ref[...]
    h = y * (1.0 + mod_ref[0, 0, 1:2, :]) + mod_ref[0, 0, 0:1, :]
    hb = h.astype(BF16)
    bd = bd_ref[...]

    def seg(lo, width):
        return _dot(hb, w_ref[:, lo:lo + width])

    def store_transposed(dst, a):
        at = a.T.astype(BF16)
        for c in range(a.shape[0] // kg):
            dst[c] = at[:, kg * c:kg * (c + 1)]

    halves = lambda a: (a[:, :LANES], a[:, LANES:])

    for jj in range(2):
        qq = _seg_rms(seg(2 * LANES * jj, 2 * LANES), bd, qg_ref[...])
        for j, q in zip((2 * jj, 2 * jj + 1), halves(qq)):
            if rope:
                q = _rope(q, cos_a[...], sin_a[...])
            q_ref[j] = (q * (A_HEAD_DIM ** -0.5 * LOG2E)).astype(BF16)

    kv = seg(512, 2 * LANES)
    k, v = halves(_seg_rms(kv, bd, kg_ref[...]))[0], kv[:, LANES:]
    if rope:
        k = _rope(k, cos_a[...], sin_a[...])
    if ctx:
        k32_ref[...] = k
        v32_ref[...] = v
    k_ref[0] = k.astype(BF16)
    store_transposed(vt_ref.at[0], v)

    gb_ref[...] = seg(768, B_WIDTH).astype(BF16)
    gu_ref[...] = (seg(1024, B_WIDTH) * seg(1280, B_WIDTH)).astype(BF16)

    cq2, ck2, cv2 = seg(1536, 2 * LANES), seg(1792, 2 * LANES), seg(2048, 2 * LANES)
    if ctx:
        ck32_ref[...] = ck2
        cv32_ref[...] = cv2
    for j, (cq, ck, cv) in enumerate(zip(halves(cq2), halves(ck2), halves(cv2))):
        if rope:
            cq = _rope(cq, cos_c[...], sin_c[...])
            ck = _rope(ck, cos_c[...], sin_c[...])
        cq_ref[j] = (cq * (C_QK_DIM ** -0.5 * LOG2E)).astype(BF16)
        ck_ref[j] = ck.astype(BF16)
        store_transposed(cvt_ref.at[j], cv)


def _inproj(x, mod, layer, mod_row, g1, w_in, qg, kg, bd, tables, *, tm, ctx, key_group):
    n = x.shape[0]
    rope = tables is not None
    row = lambda i: (i, 0)
    const2 = lambda i: (0, 0)
    pair3 = lambda i: (0, i, 0)
    tile4 = lambda i: (0, i, 0, 0)
    nc = tm // key_group
    in_specs = [
        pl.BlockSpec((tm, D_MODEL), row),
        pl.BlockSpec((1, 1, N_MOD, D_MODEL), lambda i: (layer, mod_row(i), 0, 0)),
        pl.BlockSpec((1, D_MODEL), const2),
        pl.BlockSpec((D_MODEL, W_IN_WIDTH), const2),
        pl.BlockSpec((1, 2 * LANES), const2),
        pl.BlockSpec((1, 2 * LANES), const2),
        pl.BlockSpec((2 * LANES, 2 * LANES), const2),
    ]
    args = [x, mod, g1, w_in, qg, kg, bd]
    if rope:
        t = tables[0].shape[0]
        tab = lambda i: (i % (t // tm), 0)
        in_specs += [pl.BlockSpec((tm, LANES), tab)] * 4
        args += list(tables)
    out_specs = [
        pl.BlockSpec((4, tm, LANES), pair3),
        pl.BlockSpec((1, tm, LANES), pair3),
        pl.BlockSpec((1, nc, LANES, key_group), tile4),
        pl.BlockSpec((tm, B_WIDTH), row),
        pl.BlockSpec((tm, B_WIDTH), row),
        pl.BlockSpec((2, tm, LANES), pair3),
        pl.BlockSpec((2, tm, LANES), pair3),
        pl.BlockSpec((2, nc, LANES, key_group), tile4),
    ]
    out_shape = [
        jax.ShapeDtypeStruct((4, n, LANES), BF16),
        jax.ShapeDtypeStruct((1, n, LANES), BF16),
        jax.ShapeDtypeStruct((1, n // key_group, LANES, key_group), BF16),
        jax.ShapeDtypeStruct((n, B_WIDTH), BF16),
        jax.ShapeDtypeStruct((n, B_WIDTH), BF16),
        jax.ShapeDtypeStruct((2, n, LANES), BF16),
        jax.ShapeDtypeStruct((2, n, LANES), BF16),
        jax.ShapeDtypeStruct((2, n // key_group, LANES, key_group), BF16),
    ]
    if ctx:
        out_specs += [
            pl.BlockSpec((tm, LANES), row),
            pl.BlockSpec((tm, LANES), row),
            pl.BlockSpec((tm, 2 * LANES), row),
            pl.BlockSpec((tm, 2 * LANES), row),
        ]
        out_shape += [
            jax.ShapeDtypeStruct((n, LANES), F32),
            jax.ShapeDtypeStruct((n, LANES), F32),
            jax.ShapeDtypeStruct((n, 2 * LANES), F32),
            jax.ShapeDtypeStruct((n, 2 * LANES), F32),
        ]
    return pl.pallas_call(
        functools.partial(_inproj_kernel, rope=rope, ctx=ctx, kg=key_group),
        grid=(n // tm,),
        in_specs=in_specs,
        out_specs=out_specs,
        out_shape=out_shape,
        compiler_params=_params(),
        name="inproj_ctx" if ctx else "inproj_lat",
    )(*args)


def _attend(units, k_ref, vt_ref, caches):
    tq = units[0][0].shape[0]
    nu = len(units)
    kg = vt_ref.shape[-1]
    n_groups = k_ref.shape[1] // kg
    blocks = sorted({j for _, _, j in units})
    n_stages = n_groups + (0 if caches is None else 1)

    def load(stage):
        if caches is not None:
            if stage == 0:
                return {j: (caches[j][0].astype(BF16), caches[j][1].T.astype(BF16)) for j in blocks}
            stage -= 1
        return {j: (k_ref[j, kg * stage:kg * (stage + 1), :], vt_ref[j, stage]) for j in blocks}

    def value_rows(vt, half):
        own = vt[HEAD_ROWS * half:HEAD_ROWS * (half + 1), :]
        return jnp.concatenate([own, jnp.ones((ONES_ROWS, vt.shape[1]), BF16)], axis=0)

    def score(kv):
        return [_dot_nt(kv[j][0], q) for q, _, j in units]

    ms = [jnp.full((1, tq), -jnp.inf, F32)] * nu
    accs = [jnp.zeros((HEAD_ROWS + ONES_ROWS, tq), F32)] * nu
    kv = load(0)
    scores = score(kv)
    for s in range(n_stages):
        cur_scores, cur_kv = scores, kv
        if s + 1 < n_stages:
            kv = load(s + 1)
            scores = score(kv)
        new_ms, ps = [], []
        for u in range(nu):
            m = jnp.maximum(ms[u], jnp.max(cur_scores[u], axis=0, keepdims=True))
            new_ms.append(m)
            ps.append(jnp.exp2(cur_scores[u] - m).astype(BF16))
        pvs = [_dot(value_rows(cur_kv[j][1], half), ps[u]) for u, (_, half, j) in enumerate(units)]
        accs = [jnp.exp2(ms[u] - new_ms[u]) * accs[u] + pvs[u] for u in range(nu)]
        ms = new_ms
    return [acc[0:HEAD_ROWS] / acc[HEAD_ROWS:HEAD_ROWS + 1] for acc in accs]


def _lane_mask(q, lo, width):
    lane = lax.broadcasted_iota(jnp.int32, q.shape, 1)
    return jnp.where((lane >= lo) & (lane < lo + width), q, jnp.zeros_like(q))


def _gqa_kernel(*refs, has_cache, nj):
    if has_cache:
        q_ref, k_ref, vt_ref, ck_ref, cv_ref, o_ref = refs
        caches = {0: (ck_ref[0], cv_ref[0])}
    else:
        q_ref, k_ref, vt_ref, o_ref = refs
        caches = None
    units = [(_lane_mask(q_ref[j], 64 * u, 64), u, 0) for j in range(nj) for u in range(2)]
    outs = _attend(units, k_ref, vt_ref, caches)
    for j in range(nj):
        ot = jnp.concatenate(outs[2 * j:2 * j + 2], axis=0)
        o_ref[:, LANES * j:LANES * (j + 1)] = ot.T.astype(BF16)


def _gqa_attention(q, k, vt, cache_k, cache_v, layer, *, n_batch, t_seq, tq, nj):
    n = q.shape[1]
    nq = t_seq // tq
    kg = vt.shape[-1]
    has_cache = cache_k is not None
    in_specs = [
        pl.BlockSpec((nj, tq, LANES), lambda b, i, j: (j, b * nq + i, 0)),
        pl.BlockSpec((1, t_seq, LANES), lambda b, i, j: (0, b, 0)),
        pl.BlockSpec((1, t_seq // kg, LANES, kg), lambda b, i, j: (0, b, 0, 0)),
    ]
    args = [q, k, vt]
    if has_cache:
        past = cache_k.shape[1]
        cspec = pl.BlockSpec((1, past, LANES), lambda b, i, j: (b * DEPTH + layer, 0, 0))
        in_specs += [cspec, cspec]
        args += [cache_k, cache_v]
    return pl.pallas_call(
        functools.partial(_gqa_kernel, has_cache=has_cache, nj=nj),
        grid=(n_batch, nq, 4 // nj),
        in_specs=in_specs,
        out_specs=pl.BlockSpec((tq, LANES * nj), lambda b, i, j: (b * nq + i, j)),
        out_shape=jax.ShapeDtypeStruct((n, 4 * LANES), BF16),
        compiler_params=_params(),
        name="gqa_lat" if has_cache else "gqa_ctx",
    )(*args)


def _diff_kernel(*refs, has_cache, nj, lam_init):
    if has_cache:
        q_ref, k_ref, vt_ref, ck_ref, cv_ref, lam_ref, g_ref, o_ref = refs
    else:
        q_ref, k_ref, vt_ref, lam_ref, g_ref, o_ref = refs
    lv = lam_ref[0]
    lam = (jnp.exp(jnp.sum(lv[0:1] * lv[1:2], axis=-1, keepdims=True))
           - jnp.exp(jnp.sum(lv[2:3] * lv[3:4], axis=-1, keepdims=True)) + lam_init)
    units = [(_lane_mask(q_ref[j], 32 * c, 32), c // 2, j) for j in range(nj) for c in range(4)]
    caches = None
    if has_cache:
        caches = {j: (ck_ref[0, :, LANES * j:LANES * (j + 1)], cv_ref[0, :, LANES * j:LANES * (j + 1)])
                  for j in range(nj)}
    outs = _attend(units, k_ref, vt_ref, caches)
    for j in range(nj):
        res = outs[4 * j:4 * j + 4]
        heads = []
        for i in range(2):
            o = res[2 * i] - lam * res[2 * i + 1]
            ms = jnp.mean(o * o, axis=0, keepdims=True)
            heads.append(o * lax.rsqrt(ms + EPS))
        o = jnp.concatenate(heads, axis=0).T * g_ref[...] * (1.0 - lam_init)
        o_ref[:, LANES * j:LANES * (j + 1)] = o.astype(BF16)


def _diff_attention(cq, ck, cvt, cache_k, cache_v, layer, diff_lambda, subln_g, *,
                    n_batch, t_seq, tq, nj, lam_init):
    n = cq.shape[1]
    nq = t_seq // tq
    kg = cvt.shape[-1]
    has_cache = cache_k is not None
    in_specs = [
        pl.BlockSpec((nj, tq, LANES), lambda b, i, j: (j, b * nq + i, 0)),
        pl.BlockSpec((nj, t_seq, LANES), lambda b, i, j: (j, b, 0)),
        pl.BlockSpec((nj, t_seq // kg, LANES, kg), lambda b, i, j: (j, b, 0, 0)),
    ]
    args = [cq, ck, cvt]
    if has_cache:
        past = cache_k.shape[1]
        cspec = pl.BlockSpec((1, past, LANES * nj), lambda b, i, j: (b * DEPTH + layer, 0, j))
        in_specs += [cspec, cspec]
        args += [cache_k, cache_v]
    in_specs += [
        pl.BlockSpec((1, 4, C_QK_DIM), lambda b, i, j: (layer, 0, 0)),
        pl.BlockSpec((1, LANES), lambda b, i, j: (0, 0)),
    ]
    args += [diff_lambda, subln_g]
    return pl.pallas_call(
        functools.partial(_diff_kernel, has_cache=has_cache, nj=nj, lam_init=lam_init),
        grid=(n_batch, nq, 2 // nj),
        in_specs=in_specs,
        out_specs=pl.BlockSpec((tq, LANES * nj), lambda b, i, j: (b * nq + i, j)),
        out_shape=jax.ShapeDtypeStruct((n, 2 * LANES), BF16),
        compiler_params=_params(),
        name="diff_lat" if has_cache else "diff_ctx",
    )(*args)


HALO = 16


def _route(logits_t, bias_col):
    scores = _sigmoid(logits_t)
    sel = scores + bias_col
    row = lambda a, e: a[e:e + 1, :]
    grp = []
    for g in range(N_EXPERT_GROUPS):
        s0, s1, s2, s3 = (row(sel, 4 * g + k) for k in range(4))
        hi1, lo1 = jnp.maximum(s0, s1), jnp.minimum(s0, s1)
        hi2, lo2 = jnp.maximum(s2, s3), jnp.minimum(s2, s3)
        grp.append(jnp.maximum(hi1, hi2) + jnp.maximum(jnp.minimum(hi1, hi2), jnp.maximum(lo1, lo2)))
    best = grp[0]
    tg = jnp.zeros_like(best, dtype=jnp.int32)
    for g in range(1, N_EXPERT_GROUPS):
        better = grp[g] > best
        best = jnp.where(better, grp[g], best)
        tg = jnp.where(better, g, tg)

    def in_group(a, k):
        out = row(a, k)
        for g in range(1, N_EXPERT_GROUPS):
            out = jnp.where(tg == g, row(a, 4 * g + k), out)
        return out

    vals = [in_group(sel, k) for k in range(EXPERTS_PER_GROUP)]
    scs = [in_group(scores, k) for k in range(EXPERTS_PER_GROUP)]

    def first_argmax(vs):
        bv, bi = vs[0], jnp.zeros_like(tg)
        for k in range(1, len(vs)):
            better = vs[k] > bv
            bv = jnp.where(better, vs[k], bv)
            bi = jnp.where(better, k, bi)
        return bi

    i1 = first_argmax(vals)
    i2 = first_argmax([jnp.where(i1 == k, -jnp.inf, vals[k]) for k in range(EXPERTS_PER_GROUP)])

    def pick(vs, idx):
        out = vs[0]
        for k in range(1, len(vs)):
            out = jnp.where(idx == k, vs[k], out)
        return out

    w1, w2 = pick(scs, i1), pick(scs, i2)
    tot = w1 + w2
    return 4 * tg + i1, 4 * tg + i2, w1 / tot, w2 / tot


def _outproj_kernel(oa_ref, gb_ref, gu_ref, gup_ref, gun_ref, cw_ref, oc_ref, x_ref, mod_ref,
                    w_ref, g2_ref, rwt_ref, rb_ref,
                    x1_ref, h2t_ref, route_ref, wts_ref, *, tm, t_seq):
    i = pl.program_id(0)
    gu = gu_ref[...].astype(F32)
    rowi = lax.broadcasted_iota(jnp.int32, gu.shape, 0)
    pos = (i * tm + rowi) % t_seq
    prev = pltpu.roll(gu, 1, 0)
    prev = jnp.where(rowi == 0, gup_ref[HALO - 1:HALO, :].astype(F32), prev)
    prev = jnp.where(pos == 0, 0.0, prev)
    nxt = pltpu.roll(gu, tm - 1, 0)
    nxt = jnp.where(rowi == tm - 1, gun_ref[0:1, :].astype(F32), nxt)
    nxt = jnp.where(pos == t_seq - 1, 0.0, nxt)
    cw = cw_ref[0]
    ob = gb_ref[...].astype(F32) * (prev * cw[0:1] + gu * cw[1:2] + nxt * cw[2:3])

    mix = (_dot(oa_ref[...], w_ref[0:512, :])
           + _dot(ob.astype(BF16), w_ref[512:768, :])
           + _dot(oc_ref[...], w_ref[768:1024, :]))
    x1 = x_ref[...] + mod_ref[0, 0, 2:3, :] * mix
    x1_ref[...] = x1
    ms = jnp.mean(x1 * x1, axis=-1, keepdims=True)
    h2 = x1 * lax.rsqrt(ms + EPS) * g2_ref[...]
    h2 = h2 * (1.0 + mod_ref[0, 0, 4:5, :]) + mod_ref[0, 0, 3:4, :]
    _store_token_tiles(h2t_ref, h2)

    hh, hl = _split(h2)
    rwt = rwt_ref[...]
    both = _dot_nt(rwt, hh)
    logits_t = both[0:N_EXPERTS] + both[N_EXPERTS:2 * N_EXPERTS] + _dot_nt(rwt[0:N_EXPERTS], hl)
    e0, e1, w0, w1 = _route(logits_t, rb_ref[...])
    zeros = jnp.zeros_like(w0)
    route_ref[...] = jnp.concatenate(
        [e0.astype(F32), e1.astype(F32), w0, w1, zeros, zeros, zeros, zeros], axis=0)
    srow = lax.broadcasted_iota(jnp.int32, (LANES, tm), 0)
    wts_t = jnp.where(srow == 0, w0, 0.0) + jnp.where(srow == 1, w1, 0.0)
    wts_ref[...] = wts_t.T


def _outproj(oa, gb, gu, conv_w, oc, x, mod, layer, mod_row, w_out, g2, rwt, rb, *, tm, t_seq):
    n = x.shape[0]
    row = lambda i: (i, 0)
    const2 = lambda i: (0, 0)
    hb = tm // HALO
    last = n // HALO - 1
    return pl.pallas_call(
        functools.partial(_outproj_kernel, tm=tm, t_seq=t_seq),
        grid=(n // tm,),
        in_specs=[
            pl.BlockSpec((tm, 4 * LANES), row),
            pl.BlockSpec((tm, B_WIDTH), row),
            pl.BlockSpec((tm, B_WIDTH), row),
            pl.BlockSpec((HALO, B_WIDTH), lambda i: (jnp.maximum(i * hb - 1, 0), 0)),
            pl.BlockSpec((HALO, B_WIDTH), lambda i: (jnp.minimum((i + 1) * hb, last), 0)),
            pl.BlockSpec((1, 3, B_WIDTH), lambda i: (layer, 0, 0)),
            pl.BlockSpec((tm, 2 * LANES), row),
            pl.BlockSpec((tm, D_MODEL), row),
            pl.BlockSpec((1, 1, N_MOD, D_MODEL), lambda i: (layer, mod_row(i), 0, 0)),
            pl.BlockSpec((D_MODEL, D_MODEL), const2),
            pl.BlockSpec((1, D_MODEL), const2),
            pl.BlockSpec((2 * N_EXPERTS, D_MODEL), const2),
            pl.BlockSpec((N_EXPERTS, 1), const2),
        ],
        out_specs=[
            pl.BlockSpec((tm, D_MODEL), row),
            pl.BlockSpec((tm * TOKEN_TILE_ROWS, LANES), row),
            pl.BlockSpec((8, tm), lambda i: (0, i)),
            pl.BlockSpec((tm, LANES), row),
        ],
        out_shape=[
            jax.ShapeDtypeStruct((n, D_MODEL), F32),
            jax.ShapeDtypeStruct((n * TOKEN_TILE_ROWS, LANES), F32),
            jax.ShapeDtypeStruct((8, n), F32),
            jax.ShapeDtypeStruct((n, LANES), F32),
        ],
        compiler_params=_params(),
        name="outproj",
    )(oa, gb, gu, gu, gu, conv_w, oc, x, mod, w_out, g2, rwt, rb)


TOKEN_TILE_ROWS = D_MODEL // LANES
EXPERT_TILE = 256
PLAN_BLOCK = 512


def _load_token_tiles(ref, n_tokens, lead=()):
    return jnp.concatenate(
        [ref[lead + (pl.ds(c, n_tokens, stride=TOKEN_TILE_ROWS), slice(None))]
         for c in range(TOKEN_TILE_ROWS)], axis=1)


def _store_token_tiles(ref, x):
    n_tokens = x.shape[0]
    for c in range(TOKEN_TILE_ROWS):
        ref[pl.ds(c, n_tokens, stride=TOKEN_TILE_ROWS), :] = x[:, LANES * c:LANES * (c + 1)]


def _moe_plan_kernel(route_ref, tri_ref, pos_ref, info_ref, *, n, n_tiles):
    eid = lax.broadcasted_iota(jnp.int32, (N_EXPERTS, PLAN_BLOCK), 0)

    def block(b):
        sl = slice(PLAN_BLOCK * b, PLAN_BLOCK * (b + 1))
        e0 = route_ref[0:1, sl].astype(jnp.int32)
        e1 = route_ref[1:2, sl].astype(jnp.int32)
        return e0, e1, jnp.where((eid == e0) | (eid == e1), 1.0, 0.0)

    counts = jnp.zeros((N_EXPERTS, 1), F32)
    for b in range(n // PLAN_BLOCK):
        counts = counts + jnp.sum(block(b)[2], axis=1, keepdims=True)
    ntile = jnp.floor((counts + (EXPERT_TILE - 1)) * (1.0 / EXPERT_TILE))
    sub = lax.broadcasted_iota(jnp.int32, (N_EXPERTS, LANES), 0)
    lane = lax.broadcasted_iota(jnp.int32, (N_EXPERTS, LANES), 1)
    ntile_row = jnp.sum(jnp.where(sub == lane, ntile, 0.0), axis=0, keepdims=True)
    first_tile = jnp.sum(jnp.where(lane < sub, ntile_row, 0.0), axis=1, keepdims=True)
    first_row = first_tile * EXPERT_TILE

    carry = jnp.zeros((N_EXPERTS, 1), F32)
    for b in range(n // PLAN_BLOCK):
        e0, e1, mb = block(b)
        rank = _dot(mb.astype(BF16), tri_ref[...]) + carry
        carry = carry + jnp.sum(mb, axis=1, keepdims=True)
        slot = first_row + rank
        p0 = jnp.sum(jnp.where(eid == e0, slot, 0.0), axis=0, keepdims=True)
        p1 = jnp.sum(jnp.where(eid == e1, slot, 0.0), axis=0, keepdims=True)
        pad = jnp.zeros((6, PLAN_BLOCK), jnp.int32)
        pos_ref[:, PLAN_BLOCK * b:PLAN_BLOCK * (b + 1)] = jnp.concatenate(
            [p0.astype(jnp.int32), p1.astype(jnp.int32), pad], axis=0)

    tile = lax.broadcasted_iota(jnp.int32, (1, LANES), 1).astype(F32)
    total = jnp.sum(ntile, axis=0, keepdims=True)
    expert_of_tile = jnp.sum(jnp.where(first_tile <= tile, 1.0, 0.0), axis=0, keepdims=True) - 1.0
    valid = jnp.where(tile < total, 1.0, 0.0)
    ends_expert = jnp.sum(jnp.where((first_tile + ntile == tile + 1.0) & (ntile > 0.0), 1.0, 0.0),
                          axis=0, keepdims=True)
    needs_zero = jnp.where((ends_expert > 0.0) | (valid == 0.0), 1.0, 0.0)
    zrow = jnp.zeros((5, LANES), jnp.int32)
    info_ref[...] = jnp.concatenate(
        [expert_of_tile.astype(jnp.int32), valid.astype(jnp.int32), needs_zero.astype(jnp.int32),
         zrow], axis=0)


def _moe_plan(route, tri, *, n_tiles):
    n = route.shape[1]
    return pl.pallas_call(
        functools.partial(_moe_plan_kernel, n=n, n_tiles=n_tiles),
        grid=(1,),
        in_specs=[pl.BlockSpec((8, n), lambda i: (0, 0)),
                  pl.BlockSpec((PLAN_BLOCK, PLAN_BLOCK), lambda i: (0, 0))],
        out_specs=[pl.BlockSpec((8, n), lambda i: (0, 0)),
                   pl.BlockSpec((8, LANES), lambda i: (0, 0))],
        out_shape=[jax.ShapeDtypeStruct((8, n), jnp.int32),
                   jax.ShapeDtypeStruct((8, LANES), jnp.int32)],
        compiler_params=_params(),
        name="moe_plan",
    )(route, tri)


def _token_tile(ref, t):
    return ref.at[pl.ds(pl.multiple_of(t * TOKEN_TILE_ROWS, TOKEN_TILE_ROWS), TOKEN_TILE_ROWS)]


def _token_tiles(ref, first, count):
    rows = count * TOKEN_TILE_ROWS
    return ref.at[pl.ds(pl.multiple_of(first * TOKEN_TILE_ROWS, TOKEN_TILE_ROWS), rows)]


def _dispatch_kernel(zero_ref, pos_ref, h_ref, xs_hbm, zbuf, zsem, sem, *, n_tiles):
    i = pl.program_id(0)

    @pl.when(i == 0)
    def _():
        zbuf[...] = jnp.zeros_like(zbuf)

        def fill_copy(j):
            return pltpu.make_async_copy(zbuf, _token_tiles(xs_hbm, j * EXPERT_TILE, EXPERT_TILE), zsem)

        def fill(j, c):
            @pl.when(zero_ref[j] == 1)
            def _():
                fill_copy(j).start()
            return c

        def drain(j, c):
            @pl.when(zero_ref[j] == 1)
            def _():
                fill_copy(j).wait()
            return c

        lax.fori_loop(0, n_tiles, fill, 0)
        lax.fori_loop(0, n_tiles, drain, 0)

    for r in range(PLAN_BLOCK):
        src = h_ref.at[TOKEN_TILE_ROWS * r:TOKEN_TILE_ROWS * (r + 1)]
        for s in range(2):
            pltpu.make_async_copy(src, _token_tile(xs_hbm, pos_ref[s, r]), sem).start()
    for _ in range(2):
        pltpu.make_async_copy(h_ref, _token_tiles(xs_hbm, 0, PLAN_BLOCK), sem).wait()


def _dispatch(needs_zero, pos, h2t, *, n_tiles):
    n = h2t.shape[0] // TOKEN_TILE_ROWS
    return pl.pallas_call(
        functools.partial(_dispatch_kernel, n_tiles=n_tiles),
        grid_spec=pltpu.PrefetchScalarGridSpec(
            num_scalar_prefetch=1,
            grid=(n // PLAN_BLOCK,),
            in_specs=[pl.BlockSpec((8, PLAN_BLOCK), lambda i, z: (0, i), memory_space=pltpu.SMEM),
                      pl.BlockSpec((PLAN_BLOCK * TOKEN_TILE_ROWS, LANES), lambda i, z: (i, 0))],
            out_specs=pl.BlockSpec(memory_space=pl.ANY),
            scratch_shapes=[pltpu.VMEM((EXPERT_TILE * TOKEN_TILE_ROWS, LANES), F32),
                            pltpu.SemaphoreType.DMA(()), pltpu.SemaphoreType.DMA(())]),
        out_shape=jax.ShapeDtypeStruct((n_tiles * EXPERT_TILE * TOKEN_TILE_ROWS, LANES), F32),
        compiler_params=_params(),
        name="moe_dispatch",
    )(needs_zero, pos, h2t)


def _experts_kernel(expert_ref, valid_ref, x_ref, w1_ref, w3_ref, w2_ref, o_ref):
    j = pl.program_id(0)

    @pl.when(valid_ref[j] == 1)
    def _():
        x = _load_token_tiles(x_ref, EXPERT_TILE).astype(BF16)
        a = _dot(x, w1_ref[0].astype(BF16))
        b = _dot(x, w3_ref[0].astype(BF16))
        hid = (a * _sigmoid(a)) * b
        _store_token_tiles(o_ref, _dot(hid.astype(BF16), w2_ref[0].astype(BF16)))

    @pl.when(valid_ref[j] == 0)
    def _():
        o_ref[...] = jnp.zeros_like(o_ref)


def _experts(expert_of_tile, valid, xs, w1, w3, w2, layer, *, n_tiles):
    rows = EXPERT_TILE * TOKEN_TILE_ROWS
    wmap = lambda j, e, v: (layer * N_EXPERTS + e[j], 0, 0)
    return pl.pallas_call(
        _experts_kernel,
        grid_spec=pltpu.PrefetchScalarGridSpec(
            num_scalar_prefetch=2,
            grid=(n_tiles,),
            in_specs=[pl.BlockSpec((rows, LANES), lambda j, e, v: (j, 0)),
                      pl.BlockSpec((1, D_MODEL, EXPERT_FF), wmap),
                      pl.BlockSpec((1, D_MODEL, EXPERT_FF), wmap),
                      pl.BlockSpec((1, EXPERT_FF, D_MODEL), wmap)],
            out_specs=pl.BlockSpec((rows, LANES), lambda j, e, v: (j, 0))),
        out_shape=jax.ShapeDtypeStruct((n_tiles * rows, LANES), F32),
        compiler_params=_params(),
        name="moe_experts",
    )(expert_of_tile, valid, xs, w1, w3, w2)


def _combine_kernel(pos_ref, next_pos_ref, ys_hbm, wts_ref, x1_ref, mod_ref, fg_ref, o_ref, buf, sem,
                    *, final):
    i = pl.program_id(0)
    last = pl.num_programs(0) - 1
    slot = i % 2

    def gather(p_ref, sl):
        for r in range(PLAN_BLOCK):
            for s in range(2):
                dst = buf.at[sl, s, TOKEN_TILE_ROWS * r:TOKEN_TILE_ROWS * (r + 1)]
                pltpu.make_async_copy(_token_tile(ys_hbm, p_ref[s, r]), dst, sem.at[sl]).start()

    def drain(sl):
        for s in range(2):
            pltpu.make_async_copy(_token_tiles(ys_hbm, 0, PLAN_BLOCK), buf.at[sl, s], sem.at[sl]).wait()

    @pl.when(i == 0)
    def _():
        gather(pos_ref, 0)

    gather(next_pos_ref, 1 - slot)
    drain(slot)
    y0 = _load_token_tiles(buf, PLAN_BLOCK, lead=(slot, 0))
    y1 = _load_token_tiles(buf, PLAN_BLOCK, lead=(slot, 1))
    wts = wts_ref[...]
    moe = wts[:, 0:1] * y0 + wts[:, 1:2] * y1
    x2 = x1_ref[...] + mod_ref[0, 0, 5:6, :] * moe
    if final:
        ms = jnp.mean(x2 * x2, axis=-1, keepdims=True)
        x2 = x2 * lax.rsqrt(ms + EPS) * fg_ref[...]
    o_ref[...] = x2

    @pl.when(i == last)
    def _():
        drain(1 - slot)


def _combine(pos, ys, wts, x1, mod, layer, mod_row, fg, *, final):
    n = x1.shape[0]
    nb = n // PLAN_BLOCK
    row = lambda i: (i, 0)
    return pl.pallas_call(
        functools.partial(_combine_kernel, final=final),
        grid=(nb,),
        in_specs=[
            pl.BlockSpec((8, PLAN_BLOCK), lambda i: (0, i), memory_space=pltpu.SMEM),
            pl.BlockSpec((8, PLAN_BLOCK), lambda i: (0, jnp.minimum(i + 1, nb - 1)),
                         memory_space=pltpu.SMEM),
            pl.BlockSpec(memory_space=pl.ANY),
            pl.BlockSpec((PLAN_BLOCK, LANES), row),
            pl.BlockSpec((PLAN_BLOCK, D_MODEL), row),
            pl.BlockSpec((1, 1, N_MOD, D_MODEL), lambda i: (layer, mod_row(i), 0, 0)),
            pl.BlockSpec((1, D_MODEL), lambda i: (0, 0)),
        ],
        out_specs=pl.BlockSpec((PLAN_BLOCK, D_MODEL), row),
        out_shape=jax.ShapeDtypeStruct((n, D_MODEL), F32),
        scratch_shapes=[pltpu.VMEM((2, 2, PLAN_BLOCK * TOKEN_TILE_ROWS, LANES), F32),
                        pltpu.SemaphoreType.DMA((2,))],
        compiler_params=pltpu.CompilerParams(vmem_limit_bytes=VMEM_LIMIT_BYTES,
                                             dimension_semantics=("arbitrary",)),
        name="moe_combine",
    )(pos, pos, ys, wts, x1, mod, fg)


def _moe(h2t, route, wts, x1, mod, layer, mod_row, w1, w3, w2, fg, tri, *, final):
    n = x1.shape[0]
    n_tiles = 2 * n // EXPERT_TILE + N_EXPERTS
    pos, info = _moe_plan(route, tri, n_tiles=n_tiles)
    expert_of_tile, valid, needs_zero = info[0, :n_tiles], info[1, :n_tiles], info[2, :n_tiles]
    xs = _dispatch(needs_zero, pos, h2t, n_tiles=n_tiles)
    ys = _experts(expert_of_tile, valid, xs, w1, w3, w2, layer, n_tiles=n_tiles)
    return _combine(pos, ys, wts, x1, mod, layer, mod_row, fg, final=final)


def _rope_tables(n_tokens, rot_dim):
    pos = np.arange(n_tokens)
    r = (pos // GRID_W).astype(np.float64)[:, None]
    col = (pos % GRID_W).astype(np.float64)[:, None]
    n_freq = rot_dim // 4
    inv_freq = ROPE_THETA ** (-np.arange(n_freq, dtype=np.float64) / n_freq)
    ang = np.concatenate([r * inv_freq, col * inv_freq], axis=-1)
    ang = np.repeat(ang, 2, axis=-1)
    sign = np.tile(np.array([-1.0, 1.0]), rot_dim // 2)
    cos = np.tile(np.cos(ang), (1, LANES // rot_dim))
    sin = np.tile(np.sin(ang) * sign, (1, LANES // rot_dim))
    return jnp.asarray(cos, F32), jnp.asarray(sin, F32)


def _block_diag_ones():
    idx = np.arange(2 * LANES) // 64
    return jnp.asarray((idx[:, None] == idx[None, :]).astype(np.float32), BF16)


def _pair_heads_by_kv_group(w, axis):
    shape = w.shape
    lead, tail = shape[:axis], shape[axis + 1:]
    w = w.reshape(lead + (A_KV_HEADS, A_HEADS // A_KV_HEADS, A_HEAD_DIM) + tail)
    w = jnp.swapaxes(w, axis, axis + 1)
    return w.reshape(shape)


def kernel(x_prompt, x_sample, cache_gqa_k, cache_gqa_v, cache_diff_k, cache_diff_v, c, c_ctx, w_in,
           w_out, norm1_g, norm2_g, ada_w, ada_b, gqa_qnorm_g, gqa_knorm_g, conv_w, diff_lambda,
           diff_subln_g, router_w, router_bias, moe_w1, moe_w3, moe_w2, final_g):
    n_ctx_b, t_ctx, _ = x_prompt.shape
    n_lat_b, t_lat, _ = x_sample.shape
    past = cache_gqa_k.shape[2]
    n_q = A_HEADS * A_HEAD_DIM

    cvec = jnp.concatenate([c_ctx[None, :], c, jnp.zeros((8 - 1 - n_lat_b, D_MODEL), F32)], axis=0)
    mod = _adaln(cvec, ada_w, ada_b).reshape(DEPTH, 8, N_MOD, D_MODEL)

    w_in_b = jnp.concatenate(
        [_pair_heads_by_kv_group(w_in[:, :, :n_q], 2), w_in[:, :, n_q:]], axis=2).astype(BF16)
    w_out_b = jnp.concatenate(
        [_pair_heads_by_kv_group(w_out[:, :n_q, :], 1), w_out[:, n_q:, :]], axis=1).astype(BF16)
    w1_b = moe_w1.reshape(DEPTH * N_EXPERTS, D_MODEL, EXPERT_FF)
    w3_b = moe_w3.reshape(DEPTH * N_EXPERTS, D_MODEL, EXPERT_FF)
    w2_b = moe_w2.reshape(DEPTH * N_EXPERTS, EXPERT_FF, D_MODEL)
    rw_t = router_w.T
    rw_hi = rw_t.astype(BF16)
    rwt = jnp.concatenate([rw_hi, (rw_t - rw_hi.astype(F32)).astype(BF16)], axis=0)
    rb = router_bias.reshape(N_EXPERTS, 1)
    bd = _block_diag_ones()
    tri = jnp.asarray(np.triu(np.ones((PLAN_BLOCK, PLAN_BLOCK), np.float32), k=1), BF16)
    fg = final_g.reshape(1, D_MODEL)
    tables = _rope_tables(t_lat, A_HEAD_DIM) + _rope_tables(t_lat, C_QK_DIM)

    ck_cache = cache_gqa_k.reshape(n_lat_b * DEPTH, past, A_KV_HEADS * A_HEAD_DIM)
    cv_cache = cache_gqa_v.reshape(n_lat_b * DEPTH, past, A_KV_HEADS * A_HEAD_DIM)
    dk_cache = cache_diff_k.reshape(n_lat_b * DEPTH, past, C_HEADS * 2 * C_QK_DIM)
    dv_cache = cache_diff_v.reshape(n_lat_b * DEPTH, past, C_HEADS * C_V_DIM)

    xp = x_prompt.reshape(n_ctx_b * t_ctx, D_MODEL)
    xs = x_sample.reshape(n_lat_b * t_lat, D_MODEL)
    tm = 512
    tq = 256
    new_state = []
    for l in range(DEPTH):
        lam_init = 0.8 - 0.6 * math.exp(-0.3 * l)
        g1 = norm1_g[l].reshape(1, D_MODEL)
        g2 = norm2_g[l].reshape(1, D_MODEL)
        qg = jnp.tile(gqa_qnorm_g[l], 4).reshape(1, 2 * LANES)
        kg = jnp.tile(gqa_knorm_g[l], 4).reshape(1, 2 * LANES)
        sg = jnp.tile(diff_subln_g[l], 2).reshape(1, LANES)
        final = l == DEPTH - 1

        def run_stream(x, n_batch, t_seq, mod_row_of, tabs, caches, nj_gqa, nj_diff, key_group):
            ctx = caches is None
            outs = _inproj(x, mod, l, lambda i: mod_row_of(i, tm), g1, w_in_b[l], qg, kg, bd, tabs,
                           tm=tm, ctx=ctx, key_group=key_group)
            q, k, vt, gb, gu, cq, ck, cvt = outs[:8]
            c_gk, c_gv, c_dk, c_dv = caches if caches is not None else (None,) * 4
            oa = _gqa_attention(q, k, vt, c_gk, c_gv, l, n_batch=n_batch, t_seq=t_seq, tq=tq,
                                nj=nj_gqa)
            oc = _diff_attention(cq, ck, cvt, c_dk, c_dv, l, diff_lambda, sg, n_batch=n_batch,
                                 t_seq=t_seq, tq=tq, nj=nj_diff, lam_init=lam_init)
            x1, h2t, route, wts = _outproj(oa, gb, gu, conv_w, oc, x, mod, l,
                                           lambda i: mod_row_of(i, tm), w_out_b[l], g2, rwt, rb,
                                           tm=tm, t_seq=t_seq)
            x2 = _moe(h2t, route, wts, x1, mod, l, lambda i: mod_row_of(i, PLAN_BLOCK), w1_b, w3_b,
                      w2_b, fg, tri, final=final)
            return x2, outs[8:]

        xp, state = run_stream(xp, n_ctx_b, t_ctx, lambda i, t: 0, None, None, 4, 2, t_ctx)
        new_state.append(state)
        xs, _ = run_stream(xs, n_lat_b, t_lat, lambda i, t: 1 + (i * t) // t_lat, tables,
                           (ck_cache, cv_cache, dk_cache, dv_cache), 4, 2, 256)

    def stack(k, heads, dim):
        return jnp.stack([s[k].reshape(n_ctx_b, t_ctx, heads, dim) for s in new_state], axis=1)

    return (xp.reshape(n_ctx_b, t_ctx, D_MODEL), xs.reshape(n_lat_b, t_lat, D_MODEL),
            stack(0, A_KV_HEADS, A_HEAD_DIM), stack(1, A_KV_HEADS, A_HEAD_DIM),
            stack(2, C_HEADS, 2 * C_QK_DIM), stack(3, C_HEADS, C_V_DIM))
```

```python
import functools
import math

import numpy as np
import jax
import jax.numpy as jnp
from jax import lax
from jax.experimental import pallas as pl
from jax.experimental.pallas import tpu as pltpu

D_MODEL = 1024
DEPTH = 2
GRID_W = 64
A_HEADS = 8
A_KV_HEADS = 2
A_HEAD_DIM = 64
B_WIDTH = 256
C_HEADS = 4
C_QK_DIM = 32
C_V_DIM = 64
N_EXPERTS = 16
EXPERTS_PER_GROUP = 4
N_EXPERT_GROUPS = 4
EXPERT_FF = 512
ROPE_THETA = 10000.0
EPS = 1e-6
N_MOD = 6
W_IN_WIDTH = 2304
LOG2E = math.log2(math.e)

LANES = 128
HEAD_ROWS = 64
ONES_ROWS = 16
VMEM_LIMIT_BYTES = 56 * 1024 * 1024

F32 = jnp.float32
BF16 = jnp.bfloat16

_dot = functools.partial(jnp.dot, preferred_element_type=F32)


def _dot_nt(a, b):
    return lax.dot_general(a, b, (((1,), (1,)), ((), ())), preferred_element_type=F32)


def _split(a):
    hi = a.astype(BF16)
    lo = (a - hi.astype(F32)).astype(BF16)
    return hi, lo


def _dot3(a, b):
    ah, al = _split(a)
    bh, bl = _split(b)
    return _dot(ah, bh) + _dot(ah, bl) + _dot(al, bh)


def _sigmoid(x):
    return 1.0 / (1.0 + jnp.exp(-x))


def _seg_rms(x, bd, g):
    ss = _dot((x * x).astype(BF16), bd)
    return x * lax.rsqrt(ss * (1.0 / 64.0) + EPS) * g


def _rope(x, cos, sin_signed):
    lane = lax.broadcasted_iota(jnp.int32, x.shape, 1)
    even = (lane & 1) == 0
    partner = jnp.where(even, pltpu.roll(x, LANES - 1, 1), pltpu.roll(x, 1, 1))
    return x * cos + partner * sin_signed


def _params():
    return pltpu.CompilerParams(vmem_limit_bytes=VMEM_LIMIT_BYTES)


def _adaln_kernel(c_ref, w_ref, b_ref, o_ref):
    c = c_ref[...]
    s = c * _sigmoid(c)
    o_ref[0] = _dot3(s, w_ref[0]) + b_ref[0]


def _adaln(cvec, ada_w, ada_b):
    tn = 1536
    nt = (N_MOD * D_MODEL) // tn
    return pl.pallas_call(
        _adaln_kernel,
        grid=(DEPTH, nt),
        in_specs=[
            pl.BlockSpec((8, D_MODEL), lambda l, n: (0, 0)),
            pl.BlockSpec((1, D_MODEL, tn), lambda l, n: (l, 0, n)),
            pl.BlockSpec((1, 1, tn), lambda l, n: (l, 0, n)),
        ],
        out_specs=pl.BlockSpec((1, 8, tn), lambda l, n: (l, 0, n)),
        out_shape=jax.ShapeDtypeStruct((DEPTH, 8, N_MOD * D_MODEL), F32),
        compiler_params=_params(),
        name="adaln",
    )(cvec, ada_w, ada_b.reshape(DEPTH, 1, N_MOD * D_MODEL))


def _inproj_kernel(*refs, rope, ctx, kg, n_state_in):
    it = iter(refs)
    x_ref, mod_ref, g1_ref, w_ref, qg_ref, kg_ref, bd_ref = (next(it) for _ in range(7))
    if rope:
        cos_a, sin_a, cos_c, sin_c = (next(it) for _ in range(4))
    for _ in range(n_state_in):
        next(it)
    q_ref, k_ref, vt_ref, gb_ref, gu_ref, cq_ref, ck_ref, cvt_ref = (next(it) for _ in range(8))
    if ctx:
        k32_ref, v32_ref, ck32_ref, cv32_ref = (next(it) for _ in range(4))

    def store_state(ref, a):
        ref[...] = a.reshape(ref.shape)

    x = x_ref[...]
    ms = jnp.mean(x * x, axis=-1, keepdims=True)
    y = x * lax.rsqrt(ms + EPS) * g1_ref[...]
    h = y * (1.0 + mod_ref[0, 0, 1:2, :]) + mod_ref[0, 0, 0:1, :]
    hb = h.astype(BF16)
    bd = bd_ref[...]

    def seg(lo, width):
        return _dot(hb, w_ref[:, lo:lo + width])

    def store_transposed(dst, a):
        at = a.T.astype(BF16)
        for c in range(a.shape[0] // kg):
            dst[c] = at[:, kg * c:kg * (c + 1)]

    halves = lambda a: (a[:, :LANES], a[:, LANES:])

    for jj in range(2):
        qq = _seg_rms(seg(2 * LANES * jj, 2 * LANES), bd, qg_ref[...])
        for j, q in zip((2 * jj, 2 * jj + 1), halves(qq)):
            if rope:
                q = _rope(q, cos_a[...], sin_a[...])
            q_ref[j] = (q * (A_HEAD_DIM ** -0.5 * LOG2E)).astype(BF16)

    kv = seg(512, 2 * LANES)
    k, v = halves(_seg_rms(kv, bd, kg_ref[...]))[0], kv[:, LANES:]
    if rope:
        k = _rope(k, cos_a[...], sin_a[...])
    if ctx:
        store_state(k32_ref, k)
        store_state(v32_ref, v)
    k_ref[0] = k.astype(BF16)
    store_transposed(vt_ref.at[0], v)

    gb_ref[...] = seg(768, B_WIDTH).astype(BF16)
    gu_ref[...] = (seg(1024, B_WIDTH) * seg(1280, B_WIDTH)).astype(BF16)

    cq2, ck2, cv2 = seg(1536, 2 * LANES), seg(1792, 2 * LANES), seg(2048, 2 * LANES)
    if ctx:
        store_state(ck32_ref, ck2)
        store_state(cv32_ref, cv2)
    for j, (cq, ck, cv) in enumerate(zip(halves(cq2), halves(ck2), halves(cv2))):
        if rope:
            cq = _rope(cq, cos_c[...], sin_c[...])
            ck = _rope(ck, cos_c[...], sin_c[...])
        cq_ref[j] = (cq * (C_QK_DIM ** -0.5 * LOG2E)).astype(BF16)
        ck_ref[j] = ck.astype(BF16)
        store_transposed(cvt_ref.at[j], cv)


def _inproj(x, mod, layer, mod_row, g1, w_in, qg, kg, bd, tables, *, tm, ctx, key_group, t_seq,
            state_in):
    n = x.shape[0]
    rope = tables is not None
    row = lambda i: (i, 0)
    const2 = lambda i: (0, 0)
    pair3 = lambda i: (0, i, 0)
    tile4 = lambda i: (0, i, 0, 0)
    nc = tm // key_group
    in_specs = [
        pl.BlockSpec((tm, D_MODEL), row),
        pl.BlockSpec((1, 1, N_MOD, D_MODEL), lambda i: (layer, mod_row(i), 0, 0)),
        pl.BlockSpec((1, D_MODEL), const2),
        pl.BlockSpec((D_MODEL, W_IN_WIDTH), const2),
        pl.BlockSpec((1, 2 * LANES), const2),
        pl.BlockSpec((1, 2 * LANES), const2),
        pl.BlockSpec((2 * LANES, 2 * LANES), const2),
    ]
    args = [x, mod, g1, w_in, qg, kg, bd]
    if rope:
        t = tables[0].shape[0]
        tab = lambda i: (i % (t // tm), 0)
        in_specs += [pl.BlockSpec((tm, LANES), tab)] * 4
        args += list(tables)
    out_specs = [
        pl.BlockSpec((4, tm, LANES), pair3),
        pl.BlockSpec((1, tm, LANES), pair3),
        pl.BlockSpec((1, nc, LANES, key_group), tile4),
        pl.BlockSpec((tm, B_WIDTH), row),
        pl.BlockSpec((tm, B_WIDTH), row),
        pl.BlockSpec((2, tm, LANES), pair3),
        pl.BlockSpec((2, tm, LANES), pair3),
        pl.BlockSpec((2, nc, LANES, key_group), tile4),
    ]
    out_shape = [
        jax.ShapeDtypeStruct((4, n, LANES), BF16),
        jax.ShapeDtypeStruct((1, n, LANES), BF16),
        jax.ShapeDtypeStruct((1, n // key_group, LANES, key_group), BF16),
        jax.ShapeDtypeStruct((n, B_WIDTH), BF16),
        jax.ShapeDtypeStruct((n, B_WIDTH), BF16),
        jax.ShapeDtypeStruct((2, n, LANES), BF16),
        jax.ShapeDtypeStruct((2, n, LANES), BF16),
        jax.ShapeDtypeStruct((2, n // key_group, LANES, key_group), BF16),
    ]
    aliases = {}
    n_state_in = 0
    if ctx:
        seqs = tm // t_seq
        for width in (LANES, LANES, 2 * LANES, 2 * LANES):
            out_specs.append(pl.BlockSpec((seqs, 1, t_seq, width), lambda i: (i, layer, 0, 0)))
            out_shape.append(jax.ShapeDtypeStruct((n // t_seq, DEPTH, t_seq, width), F32))
        if state_in is not None:
            n_state_in = len(state_in)
            for s, a in enumerate(state_in):
                aliases[len(args)] = len(out_shape) - n_state_in + s
                in_specs.append(pl.BlockSpec(memory_space=pl.ANY))
                args.append(a)
    return pl.pallas_call(
        functools.partial(_inproj_kernel, rope=rope, ctx=ctx, kg=key_group, n_state_in=n_state_in),
        grid=(n // tm,),
        in_specs=in_specs,
        out_specs=out_specs,
        out_shape=out_shape,
        input_output_aliases=aliases,
        compiler_params=_params(),
        name="inproj_ctx" if ctx else "inproj_lat",
    )(*args)


def _attend(units, k_ref, vt_ref, caches):
    tq = units[0][0].shape[0]
    nu = len(units)
    kg = vt_ref.shape[-1]
    n_groups = k_ref.shape[1] // kg
    blocks = sorted({j for _, _, j in units})
    n_stages = n_groups + (0 if caches is None else 1)

    def load(stage):
        if caches is not None:
            if stage == 0:
                return {j: (caches[j][0].astype(BF16), caches[j][1].T.astype(BF16)) for j in blocks}
            stage -= 1
        return {j: (k_ref[j, kg * stage:kg * (stage + 1), :], vt_ref[j, stage]) for j in blocks}

    def value_rows(vt, half):
        own = vt[HEAD_ROWS * half:HEAD_ROWS * (half + 1), :]
        return jnp.concatenate([own, jnp.ones((ONES_ROWS, vt.shape[1]), BF16)], axis=0)

    def score(kv):
        return [_dot_nt(kv[j][0], q) for q, _, j in units]

    ms = [jnp.full((1, tq), -jnp.inf, F32)] * nu
    accs = [jnp.zeros((HEAD_ROWS + ONES_ROWS, tq), F32)] * nu
    kv = load(0)
    scores = score(kv)
    for s in range(n_stages):
        cur_scores, cur_kv = scores, kv
        if s + 1 < n_stages:
            kv = load(s + 1)
            scores = score(kv)
        new_ms, ps = [], []
        for u in range(nu):
            m = jnp.maximum(ms[u], jnp.max(cur_scores[u], axis=0, keepdims=True))
            new_ms.append(m)
            ps.append(jnp.exp2(cur_scores[u] - m).astype(BF16))
        pvs = [_dot(value_rows(cur_kv[j][1], half), ps[u]) for u, (_, half, j) in enumerate(units)]
        accs = [jnp.exp2(ms[u] - new_ms[u]) * accs[u] + pvs[u] for u in range(nu)]
        ms = new_ms
    return [acc[0:HEAD_ROWS] / acc[HEAD_ROWS:HEAD_ROWS + 1] for acc in accs]


def _lane_mask(q, lo, width):
    lane = lax.broadcasted_iota(jnp.int32, q.shape, 1)
    return jnp.where((lane >= lo) & (lane < lo + width), q, jnp.zeros_like(q))


def _gqa_kernel(*refs, has_cache, nj):
    if has_cache:
        q_ref, k_ref, vt_ref, ck_ref, cv_ref, o_ref = refs
        caches = {0: (ck_ref[0], cv_ref[0])}
    else:
        q_ref, k_ref, vt_ref, o_ref = refs
        caches = None
    units = [(_lane_mask(q_ref[j], 64 * u, 64), u, 0) for j in range(nj) for u in range(2)]
    outs = _attend(units, k_ref, vt_ref, caches)
    for j in range(nj):
        ot = jnp.concatenate(outs[2 * j:2 * j + 2], axis=0)
        o_ref[:, LANES * j:LANES * (j + 1)] = ot.T.astype(BF16)


def _gqa_attention(q, k, vt, cache_k, cache_v, layer, *, n_batch, t_seq, tq, nj):
    n = q.shape[1]
    nq = t_seq // tq
    kg = vt.shape[-1]
    has_cache = cache_k is not None
    in_specs = [
        pl.BlockSpec((nj, tq, LANES), lambda b, i, j: (j, b * nq + i, 0)),
        pl.BlockSpec((1, t_seq, LANES), lambda b, i, j: (0, b, 0)),
        pl.BlockSpec((1, t_seq // kg, LANES, kg), lambda b, i, j: (0, b, 0, 0)),
    ]
    args = [q, k, vt]
    if has_cache:
        past = cache_k.shape[1]
        cspec = pl.BlockSpec((1, past, LANES), lambda b, i, j: (b * DEPTH + layer, 0, 0))
        in_specs += [cspec, cspec]
        args += [cache_k, cache_v]
    return pl.pallas_call(
        functools.partial(_gqa_kernel, has_cache=has_cache, nj=nj),
        grid=(n_batch, nq, 4 // nj),
        in_specs=in_specs,
        out_specs=pl.BlockSpec((tq, LANES * nj), lambda b, i, j: (b * nq + i, j)),
        out_shape=jax.ShapeDtypeStruct((n, 4 * LANES), BF16),
        compiler_params=_params(),
        name="gqa_lat" if has_cache else "gqa_ctx",
    )(*args)


def _diff_kernel(*refs, has_cache, nj, lam_init):
    if has_cache:
        q_ref, k_ref, vt_ref, ck_ref, cv_ref, lam_ref, g_ref, o_ref = refs
    else:
        q_ref, k_ref, vt_ref, lam_ref, g_ref, o_ref = refs
    lv = lam_ref[0]
    lam = (jnp.exp(jnp.sum(lv[0:1] * lv[1:2], axis=-1, keepdims=True))
           - jnp.exp(jnp.sum(lv[2:3] * lv[3:4], axis=-1, keepdims=True)) + lam_init)
    units = [(_lane_mask(q_ref[j], 32 * c, 32), c // 2, j) for j in range(nj) for c in range(4)]
    caches = None
    if has_cache:
        caches = {j: (ck_ref[0, :, LANES * j:LANES * (j + 1)], cv_ref[0, :, LANES * j:LANES * (j + 1)])
                  for j in range(nj)}
    outs = _attend(units, k_ref, vt_ref, caches)
    for j in range(nj):
        res = outs[4 * j:4 * j + 4]
        heads = []
        for i in range(2):
            o = res[2 * i] - lam * res[2 * i + 1]
            ms = jnp.mean(o * o, axis=0, keepdims=True)
            heads.append(o * lax.rsqrt(ms + EPS))
        o = jnp.concatenate(heads, axis=0).T * g_ref[...] * (1.0 - lam_init)
        o_ref[:, LANES * j:LANES * (j + 1)] = o.astype(BF16)


def _diff_attention(cq, ck, cvt, cache_k, cache_v, layer, diff_lambda, subln_g, *,
                    n_batch, t_seq, tq, nj, lam_init):
    n = cq.shape[1]
    nq = t_seq // tq
    kg = cvt.shape[-1]
    has_cache = cache_k is not None
    in_specs = [
        pl.BlockSpec((nj, tq, LANES), lambda b, i, j: (j, b * nq + i, 0)),
        pl.BlockSpec((nj, t_seq, LANES), lambda b, i, j: (j, b, 0)),
        pl.BlockSpec((nj, t_seq // kg, LANES, kg), lambda b, i, j: (j, b, 0, 0)),
    ]
    args = [cq, ck, cvt]
    if has_cache:
        past = cache_k.shape[1]
        cspec = pl.BlockSpec((1, past, LANES * nj), lambda b, i, j: (b * DEPTH + layer, 0, j))
        in_specs += [cspec, cspec]
        args += [cache_k, cache_v]
    in_specs += [
        pl.BlockSpec((1, 4, C_QK_DIM), lambda b, i, j: (layer, 0, 0)),
        pl.BlockSpec((1, LANES), lambda b, i, j: (0, 0)),
    ]
    args += [diff_lambda, subln_g]
    return pl.pallas_call(
        functools.partial(_diff_kernel, has_cache=has_cache, nj=nj, lam_init=lam_init),
        grid=(n_batch, nq, 2 // nj),
        in_specs=in_specs,
        out_specs=pl.BlockSpec((tq, LANES * nj), lambda b, i, j: (b * nq + i, j)),
        out_shape=jax.ShapeDtypeStruct((n, 2 * LANES), BF16),
        compiler_params=_params(),
        name="diff_lat" if has_cache else "diff_ctx",
    )(*args)


HALO = 16


def _route(logits_t, bias_col):
    scores = _sigmoid(logits_t)
    sel = scores + bias_col
    row = lambda a, e: a[e:e + 1, :]
    grp = []
    for g in range(N_EXPERT_GROUPS):
        s0, s1, s2, s3 = (row(sel, 4 * g + k) for k in range(4))
        hi1, lo1 = jnp.maximum(s0, s1), jnp.minimum(s0, s1)
        hi2, lo2 = jnp.maximum(s2, s3), jnp.minimum(s2, s3)
        grp.append(jnp.maximum(hi1, hi2) + jnp.maximum(jnp.minimum(hi1, hi2), jnp.maximum(lo1, lo2)))
    best = grp[0]
    tg = jnp.zeros_like(best, dtype=jnp.int32)
    for g in range(1, N_EXPERT_GROUPS):
        better = grp[g] > best
        best = jnp.where(better, grp[g], best)
        tg = jnp.where(better, g, tg)

    def in_group(a, k):
        out = row(a, k)
        for g in range(1, N_EXPERT_GROUPS):
            out = jnp.where(tg == g, row(a, 4 * g + k), out)
        return out

    vals = [in_group(sel, k) for k in range(EXPERTS_PER_GROUP)]
    scs = [in_group(scores, k) for k in range(EXPERTS_PER_GROUP)]

    def first_argmax(vs):
        bv, bi = vs[0], jnp.zeros_like(tg)
        for k in range(1, len(vs)):
            better = vs[k] > bv
            bv = jnp.where(better, vs[k], bv)
            bi = jnp.where(better, k, bi)
        return bi

    i1 = first_argmax(vals)
    i2 = first_argmax([jnp.where(i1 == k, -jnp.inf, vals[k]) for k in range(EXPERTS_PER_GROUP)])

    def pick(vs, idx):
        out = vs[0]
        for k in range(1, len(vs)):
            out = jnp.where(idx == k, vs[k], out)
        return out

    w1, w2 = pick(scs, i1), pick(scs, i2)
    tot = w1 + w2
    return 4 * tg + i1, 4 * tg + i2, w1 / tot, w2 / tot


def _outproj_kernel(oa_ref, gb_ref, gu_ref, gup_ref, gun_ref, cw_ref, oc_ref, x_ref, mod_ref,
                    w_ref, g2_ref, rwt_ref, rb_ref,
                    x1_ref, h2t_ref, route_ref, wts_ref, *, tm, t_seq):
    i = pl.program_id(0)
    gu = gu_ref[...].astype(F32)
    rowi = lax.broadcasted_iota(jnp.int32, gu.shape, 0)
    pos = (i * tm + rowi) % t_seq
    prev = pltpu.roll(gu, 1, 0)
    prev = jnp.where(rowi == 0, gup_ref[HALO - 1:HALO, :].astype(F32), prev)
    prev = jnp.where(pos == 0, 0.0, prev)
    nxt = pltpu.roll(gu, tm - 1, 0)
    nxt = jnp.where(rowi == tm - 1, gun_ref[0:1, :].astype(F32), nxt)
    nxt = jnp.where(pos == t_seq - 1, 0.0, nxt)
    cw = cw_ref[0]
    ob = gb_ref[...].astype(F32) * (prev * cw[0:1] + gu * cw[1:2] + nxt * cw[2:3])

    mix = (_dot(oa_ref[...], w_ref[0:512, :])
           + _dot(ob.astype(BF16), w_ref[512:768, :])
           + _dot(oc_ref[...], w_ref[768:1024, :]))
    x1 = x_ref[...] + mod_ref[0, 0, 2:3, :] * mix
    x1_ref[...] = x1
    ms = jnp.mean(x1 * x1, axis=-1, keepdims=True)
    h2 = x1 * lax.rsqrt(ms + EPS) * g2_ref[...]
    h2 = h2 * (1.0 + mod_ref[0, 0, 4:5, :]) + mod_ref[0, 0, 3:4, :]
    _store_token_tiles(h2t_ref, h2)

    hh, hl = _split(h2)
    rwt = rwt_ref[...]
    both = _dot_nt(rwt, hh)
    logits_t = both[0:N_EXPERTS] + both[N_EXPERTS:2 * N_EXPERTS] + _dot_nt(rwt[0:N_EXPERTS], hl)
    e0, e1, w0, w1 = _route(logits_t, rb_ref[...])
    zeros = jnp.zeros_like(w0)
    route_ref[...] = jnp.concatenate(
        [e0.astype(F32), e1.astype(F32), w0, w1, zeros, zeros, zeros, zeros], axis=0)
    srow = lax.broadcasted_iota(jnp.int32, (LANES, tm), 0)
    wts_t = jnp.where(srow == 0, w0, 0.0) + jnp.where(srow == 1, w1, 0.0)
    wts_ref[...] = wts_t.T


def _outproj(oa, gb, gu, conv_w, oc, x, mod, layer, mod_row, w_out, g2, rwt, rb, *, tm, t_seq):
    n = x.shape[0]
    row = lambda i: (i, 0)
    const2 = lambda i: (0, 0)
    hb = tm // HALO
    last = n // HALO - 1
    return pl.pallas_call(
        functools.partial(_outproj_kernel, tm=tm, t_seq=t_seq),
        grid=(n // tm,),
        in_specs=[
            pl.BlockSpec((tm, 4 * LANES), row),
            pl.BlockSpec((tm, B_WIDTH), row),
            pl.BlockSpec((tm, B_WIDTH), row),
            pl.BlockSpec((HALO, B_WIDTH), lambda i: (jnp.maximum(i * hb - 1, 0), 0)),
            pl.BlockSpec((HALO, B_WIDTH), lambda i: (jnp.minimum((i + 1) * hb, last), 0)),
            pl.BlockSpec((1, 3, B_WIDTH), lambda i: (layer, 0, 0)),
            pl.BlockSpec((tm, 2 * LANES), row),
            pl.BlockSpec((tm, D_MODEL), row),
            pl.BlockSpec((1, 1, N_MOD, D_MODEL), lambda i: (layer, mod_row(i), 0, 0)),
            pl.BlockSpec((D_MODEL, D_MODEL), const2),
            pl.BlockSpec((1, D_MODEL), const2),
            pl.BlockSpec((2 * N_EXPERTS, D_MODEL), const2),
            pl.BlockSpec((N_EXPERTS, 1), const2),
        ],
        out_specs=[
            pl.BlockSpec((tm, D_MODEL), row),
            pl.BlockSpec((tm * TOKEN_TILE_ROWS, LANES), row),
            pl.BlockSpec((8, tm), lambda i: (0, i)),
            pl.BlockSpec((tm, LANES), row),
        ],
        out_shape=[
            jax.ShapeDtypeStruct((n, D_MODEL), F32),
            jax.ShapeDtypeStruct((n * TOKEN_TILE_ROWS, LANES), F32),
            jax.ShapeDtypeStruct((8, n), F32),
            jax.ShapeDtypeStruct((n, LANES), F32),
        ],
        compiler_params=_params(),
        name="outproj",
    )(oa, gb, gu, gu, gu, conv_w, oc, x, mod, w_out, g2, rwt, rb)


TOKEN_TILE_ROWS = D_MODEL // LANES
EXPERT_TILE = 256
PLAN_BLOCK = 512


def _load_token_tiles(ref, n_tokens, lead=()):
    return jnp.concatenate(
        [ref[lead + (pl.ds(c, n_tokens, stride=TOKEN_TILE_ROWS), slice(None))]
         for c in range(TOKEN_TILE_ROWS)], axis=1)


def _store_token_tiles(ref, x):
    n_tokens = x.shape[0]
    for c in range(TOKEN_TILE_ROWS):
        ref[pl.ds(c, n_tokens, stride=TOKEN_TILE_ROWS), :] = x[:, LANES * c:LANES * (c + 1)]


def _moe_plan_kernel(route_ref, tri_ref, pos_ref, info_ref, *, n, n_tiles):
    eid = lax.broadcasted_iota(jnp.int32, (N_EXPERTS, PLAN_BLOCK), 0)

    def block(b):
        sl = slice(PLAN_BLOCK * b, PLAN_BLOCK * (b + 1))
        e0 = route_ref[0:1, sl].astype(jnp.int32)
        e1 = route_ref[1:2, sl].astype(jnp.int32)
        return e0, e1, jnp.where((eid == e0) | (eid == e1), 1.0, 0.0)

    counts = jnp.zeros((N_EXPERTS, 1), F32)
    for b in range(n // PLAN_BLOCK):
        counts = counts + jnp.sum(block(b)[2], axis=1, keepdims=True)
    ntile = jnp.floor((counts + (EXPERT_TILE - 1)) * (1.0 / EXPERT_TILE))
    sub = lax.broadcasted_iota(jnp.int32, (N_EXPERTS, LANES), 0)
    lane = lax.broadcasted_iota(jnp.int32, (N_EXPERTS, LANES), 1)
    ntile_row = jnp.sum(jnp.where(sub == lane, ntile, 0.0), axis=0, keepdims=True)
    first_tile = jnp.sum(jnp.where(lane < sub, ntile_row, 0.0), axis=1, keepdims=True)
    first_row = first_tile * EXPERT_TILE

    carry = jnp.zeros((N_EXPERTS, 1), F32)
    for b in range(n // PLAN_BLOCK):
        e0, e1, mb = block(b)
        rank = _dot(mb.astype(BF16), tri_ref[...]) + carry
        carry = carry + jnp.sum(mb, axis=1, keepdims=True)
        slot = first_row + rank
        p0 = jnp.sum(jnp.where(eid == e0, slot, 0.0), axis=0, keepdims=True)
        p1 = jnp.sum(jnp.where(eid == e1, slot, 0.0), axis=0, keepdims=True)
        pad = jnp.zeros((6, PLAN_BLOCK), jnp.int32)
        pos_ref[:, PLAN_BLOCK * b:PLAN_BLOCK * (b + 1)] = jnp.concatenate(
            [p0.astype(jnp.int32), p1.astype(jnp.int32), pad], axis=0)

    tile = lax.broadcasted_iota(jnp.int32, (1, LANES), 1).astype(F32)
    total = jnp.sum(ntile, axis=0, keepdims=True)
    expert_of_tile = jnp.sum(jnp.where(first_tile <= tile, 1.0, 0.0), axis=0, keepdims=True) - 1.0
    valid = jnp.where(tile < total, 1.0, 0.0)
    ends_expert = jnp.sum(jnp.where((first_tile + ntile == tile + 1.0) & (ntile > 0.0), 1.0, 0.0),
                          axis=0, keepdims=True)
    needs_zero = jnp.where((ends_expert > 0.0) | (valid == 0.0), 1.0, 0.0)
    zrow = jnp.zeros((5, LANES), jnp.int32)
    info_ref[...] = jnp.concatenate(
        [expert_of_tile.astype(jnp.int32), valid.astype(jnp.int32), needs_zero.astype(jnp.int32),
         zrow], axis=0)


def _moe_plan(route, tri, *, n_tiles):
    n = route.shape[1]
    return pl.pallas_call(
        functools.partial(_moe_plan_kernel, n=n, n_tiles=n_tiles),
        grid=(1,),
        in_specs=[pl.BlockSpec((8, n), lambda i: (0, 0)),
                  pl.BlockSpec((PLAN_BLOCK, PLAN_BLOCK), lambda i: (0, 0))],
        out_specs=[pl.BlockSpec((8, n), lambda i: (0, 0)),
                   pl.BlockSpec((8, LANES), lambda i: (0, 0))],
        out_shape=[jax.ShapeDtypeStruct((8, n), jnp.int32),
                   jax.ShapeDtypeStruct((8, LANES), jnp.int32)],
        compiler_params=_params(),
        name="moe_plan",
    )(route, tri)


def _token_tile(ref, t):
    return ref.at[pl.ds(pl.multiple_of(t * TOKEN_TILE_ROWS, TOKEN_TILE_ROWS), TOKEN_TILE_ROWS)]


def _token_tiles(ref, first, count):
    rows = count * TOKEN_TILE_ROWS
    return ref.at[pl.ds(pl.multiple_of(first * TOKEN_TILE_ROWS, TOKEN_TILE_ROWS), rows)]


def _dispatch_kernel(zero_ref, pos_ref, h_ref, xs_hbm, zbuf, zsem, sem, *, n_tiles):
    i = pl.program_id(0)

    @pl.when(i == 0)
    def _():
        zbuf[...] = jnp.zeros_like(zbuf)

        def fill_copy(j):
            return pltpu.make_async_copy(zbuf, _token_tiles(xs_hbm, j * EXPERT_TILE, EXPERT_TILE), zsem)

        def fill(j, c):
            @pl.when(zero_ref[j] == 1)
            def _():
                fill_copy(j).start()
            return c

        def drain(j, c):
            @pl.when(zero_ref[j] == 1)
            def _():
                fill_copy(j).wait()
            return c

        lax.fori_loop(0, n_tiles, fill, 0)
        lax.fori_loop(0, n_tiles, drain, 0)

    for r in range(PLAN_BLOCK):
        src = h_ref.at[TOKEN_TILE_ROWS * r:TOKEN_TILE_ROWS * (r + 1)]
        for s in range(2):
            pltpu.make_async_copy(src, _token_tile(xs_hbm, pos_ref[s, r]), sem).start(priority=s)
    for _ in range(2):
        pltpu.make_async_copy(h_ref, _token_tiles(xs_hbm, 0, PLAN_BLOCK), sem).wait()


def _dispatch(needs_zero, pos, h2t, *, n_tiles):
    n = h2t.shape[0] // TOKEN_TILE_ROWS
    return pl.pallas_call(
        functools.partial(_dispatch_kernel, n_tiles=n_tiles),
        grid_spec=pltpu.PrefetchScalarGridSpec(
            num_scalar_prefetch=1,
            grid=(n // PLAN_BLOCK,),
            in_specs=[pl.BlockSpec((8, PLAN_BLOCK), lambda i, z: (0, i), memory_space=pltpu.SMEM),
                      pl.BlockSpec((PLAN_BLOCK * TOKEN_TILE_ROWS, LANES), lambda i, z: (i, 0))],
            out_specs=pl.BlockSpec(memory_space=pl.ANY),
            scratch_shapes=[pltpu.VMEM((EXPERT_TILE * TOKEN_TILE_ROWS, LANES), F32),
                            pltpu.SemaphoreType.DMA(()), pltpu.SemaphoreType.DMA(())]),
        out_shape=jax.ShapeDtypeStruct((n_tiles * EXPERT_TILE * TOKEN_TILE_ROWS, LANES), F32),
        compiler_params=_params(),
        name="moe_dispatch",
    )(needs_zero, pos, h2t)


def _experts_kernel(expert_ref, valid_ref, x_ref, w1_ref, w3_ref, w2_ref, o_ref):
    j = pl.program_id(0)

    @pl.when(valid_ref[j] == 1)
    def _():
        x = _load_token_tiles(x_ref, EXPERT_TILE).astype(BF16)
        a = _dot(x, w1_ref[0].astype(BF16))
        b = _dot(x, w3_ref[0].astype(BF16))
        hid = (a * _sigmoid(a)) * b
        _store_token_tiles(o_ref, _dot(hid.astype(BF16), w2_ref[0].astype(BF16)))

    @pl.when(valid_ref[j] == 0)
    def _():
        o_ref[...] = jnp.zeros_like(o_ref)


def _experts(expert_of_tile, valid, xs, w1, w3, w2, layer, *, n_tiles):
    rows = EXPERT_TILE * TOKEN_TILE_ROWS
    wmap = lambda j, e, v: (layer * N_EXPERTS + e[j], 0, 0)
    return pl.pallas_call(
        _experts_kernel,
        grid_spec=pltpu.PrefetchScalarGridSpec(
            num_scalar_prefetch=2,
            grid=(n_tiles,),
            in_specs=[pl.BlockSpec((rows, LANES), lambda j, e, v: (j, 0)),
                      pl.BlockSpec((1, D_MODEL, EXPERT_FF), wmap),
                      pl.BlockSpec((1, D_MODEL, EXPERT_FF), wmap),
                      pl.BlockSpec((1, EXPERT_FF, D_MODEL), wmap)],
            out_specs=pl.BlockSpec((rows, LANES), lambda j, e, v: (j, 0))),
        out_shape=jax.ShapeDtypeStruct((n_tiles * rows, LANES), F32),
        compiler_params=_params(),
        name="moe_experts",
    )(expert_of_tile, valid, xs, w1, w3, w2)


def _combine_kernel(pos_ref, next_pos_ref, ys_hbm, wts_ref, x1_ref, mod_ref, fg_ref, o_ref, buf, sem,
                    *, final):
    i = pl.program_id(0)
    last = pl.num_programs(0) - 1
    slot = i % 2

    def gather(p_ref, sl):
        for r in range(PLAN_BLOCK):
            for s in range(2):
                dst = buf.at[sl, s, TOKEN_TILE_ROWS * r:TOKEN_TILE_ROWS * (r + 1)]
                pltpu.make_async_copy(_token_tile(ys_hbm, p_ref[s, r]), dst, sem.at[sl]).start(
                    priority=s)

    def drain(sl):
        for s in range(2):
            pltpu.make_async_copy(_token_tiles(ys_hbm, 0, PLAN_BLOCK), buf.at[sl, s], sem.at[sl]).wait()

    @pl.when(i == 0)
    def _():
        gather(pos_ref, 0)

    gather(next_pos_ref, 1 - slot)
    drain(slot)
    y0 = _load_token_tiles(buf, PLAN_BLOCK, lead=(slot, 0))
    y1 = _load_token_tiles(buf, PLAN_BLOCK, lead=(slot, 1))
    wts = wts_ref[...]
    moe = wts[:, 0:1] * y0 + wts[:, 1:2] * y1
    x2 = x1_ref[...] + mod_ref[0, 0, 5:6, :] * moe
    if final:
        ms = jnp.mean(x2 * x2, axis=-1, keepdims=True)
        x2 = x2 * lax.rsqrt(ms + EPS) * fg_ref[...]
    o_ref[...] = x2

    @pl.when(i == last)
    def _():
        drain(1 - slot)


def _combine(pos, ys, wts, x1, mod, layer, mod_row, fg, *, final):
    n = x1.shape[0]
    nb = n // PLAN_BLOCK
    row = lambda i: (i, 0)
    return pl.pallas_call(
        functools.partial(_combine_kernel, final=final),
        grid=(nb,),
        in_specs=[
            pl.BlockSpec((8, PLAN_BLOCK), lambda i: (0, i), memory_space=pltpu.SMEM),
            pl.BlockSpec((8, PLAN_BLOCK), lambda i: (0, jnp.minimum(i + 1, nb - 1)),
                         memory_space=pltpu.SMEM),
            pl.BlockSpec(memory_space=pl.ANY),
            pl.BlockSpec((PLAN_BLOCK, LANES), row),
            pl.BlockSpec((PLAN_BLOCK, D_MODEL), row),
            pl.BlockSpec((1, 1, N_MOD, D_MODEL), lambda i: (layer, mod_row(i), 0, 0)),
            pl.BlockSpec((1, D_MODEL), lambda i: (0, 0)),
        ],
        out_specs=pl.BlockSpec((PLAN_BLOCK, D_MODEL), row),
        out_shape=jax.ShapeDtypeStruct((n, D_MODEL), F32),
        scratch_shapes=[pltpu.VMEM((2, 2, PLAN_BLOCK * TOKEN_TILE_ROWS, LANES), F32),
                        pltpu.SemaphoreType.DMA((2,))],
        compiler_params=pltpu.CompilerParams(vmem_limit_bytes=VMEM_LIMIT_BYTES,
                                             dimension_semantics=("arbitrary",)),
        name="moe_combine",
    )(pos, pos, ys, wts, x1, mod, fg)


def _moe(h2t, route, wts, x1, mod, layer, mod_row, w1, w3, w2, fg, tri, *, final):
    n = x1.shape[0]
    n_tiles = 2 * n // EXPERT_TILE + N_EXPERTS
    pos, info = _moe_plan(route, tri, n_tiles=n_tiles)
    expert_of_tile, valid, needs_zero = info[0, :n_tiles], info[1, :n_tiles], info[2, :n_tiles]
    xs = _dispatch(needs_zero, pos, h2t, n_tiles=n_tiles)
    ys = _experts(expert_of_tile, valid, xs, w1, w3, w2, layer, n_tiles=n_tiles)
    return _combine(pos, ys, wts, x1, mod, layer, mod_row, fg, final=final)


def _rope_tables(n_tokens, rot_dim):
    pos = np.arange(n_tokens)
    r = (pos // GRID_W).astype(np.float64)[:, None]
    col = (pos % GRID_W).astype(np.float64)[:, None]
    n_freq = rot_dim // 4
    inv_freq = ROPE_THETA ** (-np.arange(n_freq, dtype=np.float64) / n_freq)
    ang = np.concatenate([r * inv_freq, col * inv_freq], axis=-1)
    ang = np.repeat(ang, 2, axis=-1)
    sign = np.tile(np.array([-1.0, 1.0]), rot_dim // 2)
    cos = np.tile(np.cos(ang), (1, LANES // rot_dim))
    sin = np.tile(np.sin(ang) * sign, (1, LANES // rot_dim))
    return jnp.asarray(cos, F32), jnp.asarray(sin, F32)


def _block_diag_ones():
    idx = np.arange(2 * LANES) // 64
    return jnp.asarray((idx[:, None] == idx[None, :]).astype(np.float32), BF16)


def _pair_heads_by_kv_group(w, axis):
    shape = w.shape
    lead, tail = shape[:axis], shape[axis + 1:]
    w = w.reshape(lead + (A_KV_HEADS, A_HEADS // A_KV_HEADS, A_HEAD_DIM) + tail)
    w = jnp.swapaxes(w, axis, axis + 1)
    return w.reshape(shape)


def kernel(x_prompt, x_sample, cache_gqa_k, cache_gqa_v, cache_diff_k, cache_diff_v, c, c_ctx, w_in,
           w_out, norm1_g, norm2_g, ada_w, ada_b, gqa_qnorm_g, gqa_knorm_g, conv_w, diff_lambda,
           diff_subln_g, router_w, router_bias, moe_w1, moe_w3, moe_w2, final_g):
    n_ctx_b, t_ctx, _ = x_prompt.shape
    n_lat_b, t_lat, _ = x_sample.shape
    past = cache_gqa_k.shape[2]
    n_q = A_HEADS * A_HEAD_DIM

    cvec = jnp.concatenate([c_ctx[None, :], c, jnp.zeros((8 - 1 - n_lat_b, D_MODEL), F32)], axis=0)
    mod = _adaln(cvec, ada_w, ada_b).reshape(DEPTH, 8, N_MOD, D_MODEL)

    w_in_b = jnp.concatenate(
        [_pair_heads_by_kv_group(w_in[:, :, :n_q], 2), w_in[:, :, n_q:]], axis=2).astype(BF16)
    w_out_b = jnp.concatenate(
        [_pair_heads_by_kv_group(w_out[:, :n_q, :], 1), w_out[:, n_q:, :]], axis=1).astype(BF16)
    w1_b = moe_w1.reshape(DEPTH * N_EXPERTS, D_MODEL, EXPERT_FF)
    w3_b = moe_w3.reshape(DEPTH * N_EXPERTS, D_MODEL, EXPERT_FF)
    w2_b = moe_w2.reshape(DEPTH * N_EXPERTS, EXPERT_FF, D_MODEL)
    rw_t = router_w.T
    rw_hi = rw_t.astype(BF16)
    rwt = jnp.concatenate([rw_hi, (rw_t - rw_hi.astype(F32)).astype(BF16)], axis=0)
    rb = router_bias.reshape(N_EXPERTS, 1)
    bd = _block_diag_ones()
    tri = jnp.asarray(np.triu(np.ones((PLAN_BLOCK, PLAN_BLOCK), np.float32), k=1), BF16)
    fg = final_g.reshape(1, D_MODEL)
    tables = _rope_tables(t_lat, A_HEAD_DIM) + _rope_tables(t_lat, C_QK_DIM)

    ck_cache = cache_gqa_k.reshape(n_lat_b * DEPTH, past, A_KV_HEADS * A_HEAD_DIM)
    cv_cache = cache_gqa_v.reshape(n_lat_b * DEPTH, past, A_KV_HEADS * A_HEAD_DIM)
    dk_cache = cache_diff_k.reshape(n_lat_b * DEPTH, past, C_HEADS * 2 * C_QK_DIM)
    dv_cache = cache_diff_v.reshape(n_lat_b * DEPTH, past, C_HEADS * C_V_DIM)

    xp = x_prompt.reshape(n_ctx_b * t_ctx, D_MODEL)
    xs = x_sample.reshape(n_lat_b * t_lat, D_MODEL)
    tm = 512
    tq = 256
    state = tuple(jnp.zeros((n_ctx_b, DEPTH, t_ctx, w), F32)
                  for w in (LANES, LANES, 2 * LANES, 2 * LANES))
    for l in range(DEPTH):
        lam_init = 0.8 - 0.6 * math.exp(-0.3 * l)
        g1 = norm1_g[l].reshape(1, D_MODEL)
        g2 = norm2_g[l].reshape(1, D_MODEL)
        qg = jnp.tile(gqa_qnorm_g[l], 4).reshape(1, 2 * LANES)
        kg = jnp.tile(gqa_knorm_g[l], 4).reshape(1, 2 * LANES)
        sg = jnp.tile(diff_subln_g[l], 2).reshape(1, LANES)
        final = l == DEPTH - 1

        def run_stream(x, n_batch, t_seq, mod_row_of, tabs, caches, nj_gqa, nj_diff, key_group,
                       state_in=None):
            ctx = caches is None
            outs = _inproj(x, mod, l, lambda i: mod_row_of(i, tm), g1, w_in_b[l], qg, kg, bd, tabs,
                           tm=tm, ctx=ctx, key_group=key_group, t_seq=t_seq, state_in=state_in)
            q, k, vt, gb, gu, cq, ck, cvt = outs[:8]
            c_gk, c_gv, c_dk, c_dv = caches if caches is not None else (None,) * 4
            oa = _gqa_attention(q, k, vt, c_gk, c_gv, l, n_batch=n_batch, t_seq=t_seq, tq=tq,
                                nj=nj_gqa)
            oc = _diff_attention(cq, ck, cvt, c_dk, c_dv, l, diff_lambda, sg, n_batch=n_batch,
                                 t_seq=t_seq, tq=tq, nj=nj_diff, lam_init=lam_init)
            x1, h2t, route, wts = _outproj(oa, gb, gu, conv_w, oc, x, mod, l,
                                           lambda i: mod_row_of(i, tm), w_out_b[l], g2, rwt, rb,
                                           tm=tm, t_seq=t_seq)
            x2 = _moe(h2t, route, wts, x1, mod, l, lambda i: mod_row_of(i, PLAN_BLOCK), w1_b, w3_b,
                      w2_b, fg, tri, final=final)
            return x2, outs[8:]

        xp, state = run_stream(xp, n_ctx_b, t_ctx, lambda i, t: 0, None, None, 4, 2, t_ctx, state)
        xs, _ = run_stream(xs, n_lat_b, t_lat, lambda i, t: 1 + (i * t) // t_lat, tables,
                           (ck_cache, cv_cache, dk_cache, dv_cache), 4, 2, 256)

    def heads(a, n_heads, dim):
        return a.reshape(n_ctx_b, DEPTH, t_ctx, n_heads, dim)

    return (xp.reshape(n_ctx_b, t_ctx, D_MODEL), xs.reshape(n_lat_b, t_lat, D_MODEL),
            heads(state[0], A_KV_HEADS, A_HEAD_DIM), heads(state[1], A_KV_HEADS, A_HEAD_DIM),
            heads(state[2], C_HEADS, 2 * C_QK_DIM), heads(state[3], C_HEADS, C_V_DIM))
```

```python
import functools
import math

import numpy as np
import jax
import jax.numpy as jnp
from jax import lax
from jax.experimental import pallas as pl
from jax.experimental.pallas import tpu as pltpu

D_MODEL = 1024
DEPTH = 2
GRID_W = 64
A_HEADS = 8
A_KV_HEADS = 2
A_HEAD_DIM = 64
B_WIDTH = 256
C_HEADS = 4
C_QK_DIM = 32
C_V_DIM = 64
N_EXPERTS = 16
EXPERTS_PER_GROUP = 4
N_EXPERT_GROUPS = 4
EXPERT_FF = 512
ROPE_THETA = 10000.0
EPS = 1e-6
N_MOD = 6
W_IN_WIDTH = 2304
LOG2E = math.log2(math.e)

LANES = 128
HEAD_ROWS = 64
ONES_ROWS = 16
VMEM_LIMIT_BYTES = 56 * 1024 * 1024

F32 = jnp.float32
BF16 = jnp.bfloat16

_dot = functools.partial(jnp.dot, preferred_element_type=F32)


def _dot_nt(a, b):
    return lax.dot_general(a, b, (((1,), (1,)), ((), ())), preferred_element_type=F32)


def _split(a):
    hi = a.astype(BF16)
    lo = (a - hi.astype(F32)).astype(BF16)
    return hi, lo


def _dot3(a, b):
    ah, al = _split(a)
    bh, bl = _split(b)
    return _dot(ah, bh) + _dot(ah, bl) + _dot(al, bh)


def _sigmoid(x):
    return 1.0 / (1.0 + jnp.exp(-x))


def _seg_rms(x, bd, g):
    ss = _dot((x * x).astype(BF16), bd)
    return x * lax.rsqrt(ss * (1.0 / 64.0) + EPS) * g


def _rope(x, cos, sin_signed):
    lane = lax.broadcasted_iota(jnp.int32, x.shape, 1)
    even = (lane & 1) == 0
    partner = jnp.where(even, pltpu.roll(x, LANES - 1, 1), pltpu.roll(x, 1, 1))
    return x * cos + partner * sin_signed


def _params():
    return pltpu.CompilerParams(vmem_limit_bytes=VMEM_LIMIT_BYTES)


def _adaln_kernel(c_ref, w_ref, b_ref, o_ref):
    c = c_ref[...]
    s = c * _sigmoid(c)
    o_ref[0] = _dot3(s, w_ref[0]) + b_ref[0]


def _adaln(cvec, ada_w, ada_b):
    tn = 1536
    nt = (N_MOD * D_MODEL) // tn
    return pl.pallas_call(
        _adaln_kernel,
        grid=(DEPTH, nt),
        in_specs=[
            pl.BlockSpec((8, D_MODEL), lambda l, n: (0, 0)),
            pl.BlockSpec((1, D_MODEL, tn), lambda l, n: (l, 0, n)),
            pl.BlockSpec((1, 1, tn), lambda l, n: (l, 0, n)),
        ],
        out_specs=pl.BlockSpec((1, 8, tn), lambda l, n: (l, 0, n)),
        out_shape=jax.ShapeDtypeStruct((DEPTH, 8, N_MOD * D_MODEL), F32),
        compiler_params=_params(),
        name="adaln",
    )(cvec, ada_w, ada_b.reshape(DEPTH, 1, N_MOD * D_MODEL))


def _inproj_kernel(*refs, rope, ctx, kg, n_state_in):
    it = iter(refs)
    x_ref, mod_ref, g1_ref, w_ref, qg_ref, kg_ref, bd_ref = (next(it) for _ in range(7))
    if rope:
        cos_a, sin_a, cos_c, sin_c = (next(it) for _ in range(4))
    for _ in range(n_state_in):
        next(it)
    q_ref, k_ref, vt_ref, gb_ref, gu_ref, cq_ref, ck_ref, cvt_ref = (next(it) for _ in range(8))
    if ctx:
        k32_ref, v32_ref, ck32_ref, cv32_ref = (next(it) for _ in range(4))

    def store_state(ref, a):
        ref[...] = a.reshape(ref.shape)

    x = x_ref[...]
    ms = jnp.mean(x * x, axis=-1, keepdims=True)
    y = x * lax.rsqrt(ms + EPS) * g1_ref[...]
    h = y * (1.0 + mod_ref[0, 0, 1:2, :]) + mod_ref[0, 0, 0:1, :]
    hb = h.astype(BF16)
    bd = bd_ref[...]

    def seg(lo, width):
        return _dot(hb, w_ref[:, lo:lo + width])

    def store_transposed(dst, a):
        at = a.T.astype(BF16)
        for c in range(a.shape[0] // kg):
            dst[c] = at[:, kg * c:kg * (c + 1)]

    halves = lambda a: (a[:, :LANES], a[:, LANES:])

    for jj in range(2):
        qq = _seg_rms(seg(2 * LANES * jj, 2 * LANES), bd, qg_ref[...])
        for j, q in zip((2 * jj, 2 * jj + 1), halves(qq)):
            if rope:
                q = _rope(q, cos_a[...], sin_a[...])
            q_ref[j] = (q * (A_HEAD_DIM ** -0.5 * LOG2E)).astype(BF16)

    kv = seg(512, 2 * LANES)
    k, v = halves(_seg_rms(kv, bd, kg_ref[...]))[0], kv[:, LANES:]
    if rope:
        k = _rope(k, cos_a[...], sin_a[...])
    if ctx:
        store_state(k32_ref, k)
        store_state(v32_ref, v)
    k_ref[0] = k.astype(BF16)
    store_transposed(vt_ref.at[0], v)

    gb_ref[...] = seg(768, B_WIDTH).astype(BF16)
    gu_ref[...] = (seg(1024, B_WIDTH) * seg(1280, B_WIDTH)).astype(BF16)

    cq2, ck2, cv2 = seg(1536, 2 * LANES), seg(1792, 2 * LANES), seg(2048, 2 * LANES)
    if ctx:
        store_state(ck32_ref, ck2)
        store_state(cv32_ref, cv2)
    for j, (cq, ck, cv) in enumerate(zip(halves(cq2), halves(ck2), halves(cv2))):
        if rope:
            cq = _rope(cq, cos_c[...], sin_c[...])
            ck = _rope(ck, cos_c[...], sin_c[...])
        cq_ref[j] = (cq * (C_QK_DIM ** -0.5 * LOG2E)).astype(BF16)
        ck_ref[j] = ck.astype(BF16)
        store_transposed(cvt_ref.at[j], cv)


def _inproj(x, mod, layer, mod_row, g1, w_in, qg, kg, bd, tables, *, tm, ctx, key_group, t_seq,
            state_in):
    n = x.shape[0]
    rope = tables is not None
    row = lambda i: (i, 0)
    const2 = lambda i: (0, 0)
    pair3 = lambda i: (0, i, 0)
    tile4 = lambda i: (0, i, 0, 0)
    nc = tm // key_group
    in_specs = [
        pl.BlockSpec((tm, D_MODEL), row),
        pl.BlockSpec((1, 1, N_MOD, D_MODEL), lambda i: (layer, mod_row(i), 0, 0)),
        pl.BlockSpec((1, D_MODEL), const2),
        pl.BlockSpec((D_MODEL, W_IN_WIDTH), const2),
        pl.BlockSpec((1, 2 * LANES), const2),
        pl.BlockSpec((1, 2 * LANES), const2),
        pl.BlockSpec((2 * LANES, 2 * LANES), const2),
    ]
    args = [x, mod, g1, w_in, qg, kg, bd]
    if rope:
        t = tables[0].shape[0]
        tab = lambda i: (i % (t // tm), 0)
        in_specs += [pl.BlockSpec((tm, LANES), tab)] * 4
        args += list(tables)
    out_specs = [
        pl.BlockSpec((4, tm, LANES), pair3),
        pl.BlockSpec((1, tm, LANES), pair3),
        pl.BlockSpec((1, nc, LANES, key_group), tile4),
        pl.BlockSpec((tm, B_WIDTH), row),
        pl.BlockSpec((tm, B_WIDTH), row),
        pl.BlockSpec((2, tm, LANES), pair3),
        pl.BlockSpec((2, tm, LANES), pair3),
        pl.BlockSpec((2, nc, LANES, key_group), tile4),
    ]
    out_shape = [
        jax.ShapeDtypeStruct((4, n, LANES), BF16),
        jax.ShapeDtypeStruct((1, n, LANES), BF16),
        jax.ShapeDtypeStruct((1, n // key_group, LANES, key_group), BF16),
        jax.ShapeDtypeStruct((n, B_WIDTH), BF16),
        jax.ShapeDtypeStruct((n, B_WIDTH), BF16),
        jax.ShapeDtypeStruct((2, n, LANES), BF16),
        jax.ShapeDtypeStruct((2, n, LANES), BF16),
        jax.ShapeDtypeStruct((2, n // key_group, LANES, key_group), BF16),
    ]
    aliases = {}
    n_state_in = 0
    if ctx:
        seqs = tm // t_seq
        for width in (LANES, LANES, 2 * LANES, 2 * LANES):
            out_specs.append(pl.BlockSpec((seqs, 1, t_seq, width), lambda i: (i, layer, 0, 0)))
            out_shape.append(jax.ShapeDtypeStruct((n // t_seq, DEPTH, t_seq, width), F32))
        if state_in is not None:
            n_state_in = len(state_in)
            for s, a in enumerate(state_in):
                aliases[len(args)] = len(out_shape) - n_state_in + s
                in_specs.append(pl.BlockSpec(memory_space=pl.ANY))
                args.append(a)
    return pl.pallas_call(
        functools.partial(_inproj_kernel, rope=rope, ctx=ctx, kg=key_group, n_state_in=n_state_in),
        grid=(n // tm,),
        in_specs=in_specs,
        out_specs=out_specs,
        out_shape=out_shape,
        input_output_aliases=aliases,
        compiler_params=_params(),
        name="inproj_ctx" if ctx else "inproj_lat",
    )(*args)


def _attend(units, k_ref, vt_ref, caches):
    tq = units[0][0].shape[0]
    nu = len(units)
    kg = vt_ref.shape[-1]
    n_groups = k_ref.shape[1] // kg
    blocks = sorted({j for _, _, j in units})
    n_stages = n_groups + (0 if caches is None else 1)

    def load(stage):
        if caches is not None:
            if stage == 0:
                return {j: (caches[j][0].astype(BF16), caches[j][1].T.astype(BF16)) for j in blocks}
            stage -= 1
        return {j: (k_ref[j, kg * stage:kg * (stage + 1), :], vt_ref[j, stage]) for j in blocks}

    def value_rows(vt, half):
        own = vt[HEAD_ROWS * half:HEAD_ROWS * (half + 1), :]
        return jnp.concatenate([own, jnp.ones((ONES_ROWS, vt.shape[1]), BF16)], axis=0)

    def score(kv):
        return [_dot_nt(kv[j][0], q) for q, _, j in units]

    ms = [jnp.full((1, tq), -jnp.inf, F32)] * nu
    accs = [jnp.zeros((HEAD_ROWS + ONES_ROWS, tq), F32)] * nu
    kv = load(0)
    scores = score(kv)
    for s in range(n_stages):
        cur_scores, cur_kv = scores, kv
        if s + 1 < n_stages:
            kv = load(s + 1)
            scores = score(kv)
        new_ms, ps = [], []
        for u in range(nu):
            half = cur_scores[u].shape[0] // 2
            s1, s2 = cur_scores[u][:half], cur_scores[u][half:]
            m1 = jnp.maximum(ms[u], jnp.max(s1, axis=0, keepdims=True))
            p1 = jnp.exp2(s1 - m1).astype(BF16)
            m = jnp.maximum(m1, jnp.max(s2, axis=0, keepdims=True))
            p2 = jnp.exp2(s2 - m).astype(BF16)
            new_ms.append(m)
            ps.append(jnp.concatenate([p1 * jnp.exp2(m1 - m).astype(BF16), p2], axis=0))
        pvs = [_dot(value_rows(cur_kv[j][1], half), ps[u]) for u, (_, half, j) in enumerate(units)]
        accs = [jnp.exp2(ms[u] - new_ms[u]) * accs[u] + pvs[u] for u in range(nu)]
        ms = new_ms
    return [acc[0:HEAD_ROWS] / acc[HEAD_ROWS:HEAD_ROWS + 1] for acc in accs]


def _lane_mask(q, lo, width):
    lane = lax.broadcasted_iota(jnp.int32, q.shape, 1)
    return jnp.where((lane >= lo) & (lane < lo + width), q, jnp.zeros_like(q))


def _gqa_kernel(*refs, has_cache, nj):
    if has_cache:
        q_ref, k_ref, vt_ref, ck_ref, cv_ref, o_ref = refs
        caches = {0: (ck_ref[0], cv_ref[0])}
    else:
        q_ref, k_ref, vt_ref, o_ref = refs
        caches = None
    units = [(_lane_mask(q_ref[j], 64 * u, 64), u, 0) for j in range(nj) for u in range(2)]
    outs = _attend(units, k_ref, vt_ref, caches)
    for j in range(nj):
        ot = jnp.concatenate(outs[2 * j:2 * j + 2], axis=0)
        o_ref[:, LANES * j:LANES * (j + 1)] = ot.T.astype(BF16)


def _gqa_attention(q, k, vt, cache_k, cache_v, layer, *, n_batch, t_seq, tq, nj):
    n = q.shape[1]
    nq = t_seq // tq
    kg = vt.shape[-1]
    has_cache = cache_k is not None
    in_specs = [
        pl.BlockSpec((nj, tq, LANES), lambda b, i, j: (j, b * nq + i, 0)),
        pl.BlockSpec((1, t_seq, LANES), lambda b, i, j: (0, b, 0)),
        pl.BlockSpec((1, t_seq // kg, LANES, kg), lambda b, i, j: (0, b, 0, 0)),
    ]
    args = [q, k, vt]
    if has_cache:
        past = cache_k.shape[1]
        cspec = pl.BlockSpec((1, past, LANES), lambda b, i, j: (b * DEPTH + layer, 0, 0))
        in_specs += [cspec, cspec]
        args += [cache_k, cache_v]
    return pl.pallas_call(
        functools.partial(_gqa_kernel, has_cache=has_cache, nj=nj),
        grid=(n_batch, nq, 4 // nj),
        in_specs=in_specs,
        out_specs=pl.BlockSpec((tq, LANES * nj), lambda b, i, j: (b * nq + i, j)),
        out_shape=jax.ShapeDtypeStruct((n, 4 * LANES), BF16),
        compiler_params=_params(),
        name="gqa_lat" if has_cache else "gqa_ctx",
    )(*args)


def _diff_kernel(*refs, has_cache, nj, lam_init):
    if has_cache:
        q_ref, k_ref, vt_ref, ck_ref, cv_ref, lam_ref, g_ref, o_ref = refs
    else:
        q_ref, k_ref, vt_ref, lam_ref, g_ref, o_ref = refs
    lv = lam_ref[0]
    lam = (jnp.exp(jnp.sum(lv[0:1] * lv[1:2], axis=-1, keepdims=True))
           - jnp.exp(jnp.sum(lv[2:3] * lv[3:4], axis=-1, keepdims=True)) + lam_init)
    units = [(_lane_mask(q_ref[j], 32 * c, 32), c // 2, j) for j in range(nj) for c in range(4)]
    caches = None
    if has_cache:
        caches = {j: (ck_ref[0, :, LANES * j:LANES * (j + 1)], cv_ref[0, :, LANES * j:LANES * (j + 1)])
                  for j in range(nj)}
    outs = _attend(units, k_ref, vt_ref, caches)
    for j in range(nj):
        res = outs[4 * j:4 * j + 4]
        heads = []
        for i in range(2):
            o = res[2 * i] - lam * res[2 * i + 1]
            ms = jnp.mean(o * o, axis=0, keepdims=True)
            heads.append(o * lax.rsqrt(ms + EPS))
        o = jnp.concatenate(heads, axis=0).T * g_ref[...] * (1.0 - lam_init)
        o_ref[:, LANES * j:LANES * (j + 1)] = o.astype(BF16)


def _diff_attention(cq, ck, cvt, cache_k, cache_v, layer, diff_lambda, subln_g, *,
                    n_batch, t_seq, tq, nj, lam_init):
    n = cq.shape[1]
    nq = t_seq // tq
    kg = cvt.shape[-1]
    has_cache = cache_k is not None
    in_specs = [
        pl.BlockSpec((nj, tq, LANES), lambda b, i, j: (j, b * nq + i, 0)),
        pl.BlockSpec((nj, t_seq, LANES), lambda b, i, j: (j, b, 0)),
        pl.BlockSpec((nj, t_seq // kg, LANES, kg), lambda b, i, j: (j, b, 0, 0)),
    ]
    args = [cq, ck, cvt]
    if has_cache:
        past = cache_k.shape[1]
        cspec = pl.BlockSpec((1, past, LANES * nj), lambda b, i, j: (b * DEPTH + layer, 0, j))
        in_specs += [cspec, cspec]
        args += [cache_k, cache_v]
    in_specs += [
        pl.BlockSpec((1, 4, C_QK_DIM), lambda b, i, j: (layer, 0, 0)),
        pl.BlockSpec((1, LANES), lambda b, i, j: (0, 0)),
    ]
    args += [diff_lambda, subln_g]
    return pl.pallas_call(
        functools.partial(_diff_kernel, has_cache=has_cache, nj=nj, lam_init=lam_init),
        grid=(n_batch, nq, 2 // nj),
        in_specs=in_specs,
        out_specs=pl.BlockSpec((tq, LANES * nj), lambda b, i, j: (b * nq + i, j)),
        out_shape=jax.ShapeDtypeStruct((n, 2 * LANES), BF16),
        compiler_params=_params(),
        name="diff_lat" if has_cache else "diff_ctx",
    )(*args)


HALO = 16


def _route(logits_t, bias_col):
    scores = _sigmoid(logits_t)
    sel = scores + bias_col
    row = lambda a, e: a[e:e + 1, :]
    grp = []
    for g in range(N_EXPERT_GROUPS):
        s0, s1, s2, s3 = (row(sel, 4 * g + k) for k in range(4))
        hi1, lo1 = jnp.maximum(s0, s1), jnp.minimum(s0, s1)
        hi2, lo2 = jnp.maximum(s2, s3), jnp.minimum(s2, s3)
        grp.append(jnp.maximum(hi1, hi2) + jnp.maximum(jnp.minimum(hi1, hi2), jnp.maximum(lo1, lo2)))
    best = grp[0]
    tg = jnp.zeros_like(best, dtype=jnp.int32)
    for g in range(1, N_EXPERT_GROUPS):
        better = grp[g] > best
        best = jnp.where(better, grp[g], best)
        tg = jnp.where(better, g, tg)

    def in_group(a, k):
        out = row(a, k)
        for g in range(1, N_EXPERT_GROUPS):
            out = jnp.where(tg == g, row(a, 4 * g + k), out)
        return out

    vals = [in_group(sel, k) for k in range(EXPERTS_PER_GROUP)]
    scs = [in_group(scores, k) for k in range(EXPERTS_PER_GROUP)]

    def first_argmax(vs):
        bv, bi = vs[0], jnp.zeros_like(tg)
        for k in range(1, len(vs)):
            better = vs[k] > bv
            bv = jnp.where(better, vs[k], bv)
            bi = jnp.where(better, k, bi)
        return bi

    i1 = first_argmax(vals)
    i2 = first_argmax([jnp.where(i1 == k, -jnp.inf, vals[k]) for k in range(EXPERTS_PER_GROUP)])

    def pick(vs, idx):
        out = vs[0]
        for k in range(1, len(vs)):
            out = jnp.where(idx == k, vs[k], out)
        return out

    w1, w2 = pick(scs, i1), pick(scs, i2)
    tot = w1 + w2
    return 4 * tg + i1, 4 * tg + i2, w1 / tot, w2 / tot


def _outproj_kernel(oa_ref, gb_ref, gu_ref, gup_ref, gun_ref, cw_ref, oc_ref, x_ref, mod_ref,
                    w_ref, g2_ref, rwt_ref, rb_ref,
                    x1_ref, h2t_ref, route_ref, wts_ref, *, tm, t_seq):
    i = pl.program_id(0)
    gu = gu_ref[...].astype(F32)
    rowi = lax.broadcasted_iota(jnp.int32, gu.shape, 0)
    pos = (i * tm + rowi) % t_seq
    prev = pltpu.roll(gu, 1, 0)
    prev = jnp.where(rowi == 0, gup_ref[HALO - 1:HALO, :].astype(F32), prev)
    prev = jnp.where(pos == 0, 0.0, prev)
    nxt = pltpu.roll(gu, tm - 1, 0)
    nxt = jnp.where(rowi == tm - 1, gun_ref[0:1, :].astype(F32), nxt)
    nxt = jnp.where(pos == t_seq - 1, 0.0, nxt)
    cw = cw_ref[0]
    ob = gb_ref[...].astype(F32) * (prev * cw[0:1] + gu * cw[1:2] + nxt * cw[2:3])

    mix = (_dot(oa_ref[...], w_ref[0:512, :])
           + _dot(ob.astype(BF16), w_ref[512:768, :])
           + _dot(oc_ref[...], w_ref[768:1024, :]))
    x1 = x_ref[...] + mod_ref[0, 0, 2:3, :] * mix
    x1_ref[...] = x1
    ms = jnp.mean(x1 * x1, axis=-1, keepdims=True)
    h2 = x1 * lax.rsqrt(ms + EPS) * g2_ref[...]
    h2 = h2 * (1.0 + mod_ref[0, 0, 4:5, :]) + mod_ref[0, 0, 3:4, :]
    _store_token_tiles(h2t_ref, h2)

    hh, hl = _split(h2)
    rwt = rwt_ref[...]
    both = _dot_nt(rwt, hh)
    logits_t = both[0:N_EXPERTS] + both[N_EXPERTS:2 * N_EXPERTS] + _dot_nt(rwt[0:N_EXPERTS], hl)
    e0, e1, w0, w1 = _route(logits_t, rb_ref[...])
    zeros = jnp.zeros_like(w0)
    route_ref[...] = jnp.concatenate(
        [e0.astype(F32), e1.astype(F32), w0, w1, zeros, zeros, zeros, zeros], axis=0)
    srow = lax.broadcasted_iota(jnp.int32, (LANES, tm), 0)
    wts_t = jnp.where(srow == 0, w0, 0.0) + jnp.where(srow == 1, w1, 0.0)
    wts_ref[...] = wts_t.T


def _outproj(oa, gb, gu, conv_w, oc, x, mod, layer, mod_row, w_out, g2, rwt, rb, *, tm, t_seq):
    n = x.shape[0]
    row = lambda i: (i, 0)
    const2 = lambda i: (0, 0)
    hb = tm // HALO
    last = n // HALO - 1
    return pl.pallas_call(
        functools.partial(_outproj_kernel, tm=tm, t_seq=t_seq),
        grid=(n // tm,),
        in_specs=[
            pl.BlockSpec((tm, 4 * LANES), row),
            pl.BlockSpec((tm, B_WIDTH), row),
            pl.BlockSpec((tm, B_WIDTH), row),
            pl.BlockSpec((HALO, B_WIDTH), lambda i: (jnp.maximum(i * hb - 1, 0), 0)),
            pl.BlockSpec((HALO, B_WIDTH), lambda i: (jnp.minimum((i + 1) * hb, last), 0)),
            pl.BlockSpec((1, 3, B_WIDTH), lambda i: (layer, 0, 0)),
            pl.BlockSpec((tm, 2 * LANES), row),
            pl.BlockSpec((tm, D_MODEL), row),
            pl.BlockSpec((1, 1, N_MOD, D_MODEL), lambda i: (layer, mod_row(i), 0, 0)),
            pl.BlockSpec((D_MODEL, D_MODEL), const2),
            pl.BlockSpec((1, D_MODEL), const2),
            pl.BlockSpec((2 * N_EXPERTS, D_MODEL), const2),
            pl.BlockSpec((N_EXPERTS, 1), const2),
        ],
        out_specs=[
            pl.BlockSpec((tm, D_MODEL), row),
            pl.BlockSpec((tm * TOKEN_TILE_ROWS, LANES), row),
            pl.BlockSpec((8, tm), lambda i: (0, i)),
            pl.BlockSpec((tm, LANES), row),
        ],
        out_shape=[
            jax.ShapeDtypeStruct((n, D_MODEL), F32),
            jax.ShapeDtypeStruct((n * TOKEN_TILE_ROWS, LANES), F32),
            jax.ShapeDtypeStruct((8, n), F32),
            jax.ShapeDtypeStruct((n, LANES), F32),
        ],
        compiler_params=_params(),
        name="outproj",
    )(oa, gb, gu, gu, gu, conv_w, oc, x, mod, w_out, g2, rwt, rb)


TOKEN_TILE_ROWS = D_MODEL // LANES
EXPERT_TILE = 512
PLAN_BLOCK = 512


def _load_token_tiles(ref, n_tokens, lead=()):
    return jnp.concatenate(
        [ref[lead + (pl.ds(c, n_tokens, stride=TOKEN_TILE_ROWS), slice(None))]
         for c in range(TOKEN_TILE_ROWS)], axis=1)


def _store_token_tiles(ref, x):
    n_tokens = x.shape[0]
    for c in range(TOKEN_TILE_ROWS):
        ref[pl.ds(c, n_tokens, stride=TOKEN_TILE_ROWS), :] = x[:, LANES * c:LANES * (c + 1)]


def _moe_plan_kernel(route_ref, tri_ref, pos_ref, info_ref, *, n, n_tiles):
    eid = lax.broadcasted_iota(jnp.int32, (N_EXPERTS, PLAN_BLOCK), 0)

    def block(b):
        sl = slice(PLAN_BLOCK * b, PLAN_BLOCK * (b + 1))
        e0 = route_ref[0:1, sl].astype(jnp.int32)
        e1 = route_ref[1:2, sl].astype(jnp.int32)
        return e0, e1, jnp.where((eid == e0) | (eid == e1), 1.0, 0.0)

    counts = jnp.zeros((N_EXPERTS, 1), F32)
    for b in range(n // PLAN_BLOCK):
        counts = counts + jnp.sum(block(b)[2], axis=1, keepdims=True)
    ntile = jnp.floor((counts + (EXPERT_TILE - 1)) * (1.0 / EXPERT_TILE))
    sub = lax.broadcasted_iota(jnp.int32, (N_EXPERTS, LANES), 0)
    lane = lax.broadcasted_iota(jnp.int32, (N_EXPERTS, LANES), 1)
    ntile_row = jnp.sum(jnp.where(sub == lane, ntile, 0.0), axis=0, keepdims=True)
    first_tile = jnp.sum(jnp.where(lane < sub, ntile_row, 0.0), axis=1, keepdims=True)
    first_row = first_tile * EXPERT_TILE

    carry = jnp.zeros((N_EXPERTS, 1), F32)
    for b in range(n // PLAN_BLOCK):
        e0, e1, mb = block(b)
        rank = _dot(mb.astype(BF16), tri_ref[...]) + carry
        carry = carry + jnp.sum(mb, axis=1, keepdims=True)
        slot = first_row + rank
        p0 = jnp.sum(jnp.where(eid == e0, slot, 0.0), axis=0, keepdims=True)
        p1 = jnp.sum(jnp.where(eid == e1, slot, 0.0), axis=0, keepdims=True)
        pad = jnp.zeros((6, PLAN_BLOCK), jnp.int32)
        pos_ref[:, PLAN_BLOCK * b:PLAN_BLOCK * (b + 1)] = jnp.concatenate(
            [p0.astype(jnp.int32), p1.astype(jnp.int32), pad], axis=0)

    tile = lax.broadcasted_iota(jnp.int32, (1, LANES), 1).astype(F32)
    total = jnp.sum(ntile, axis=0, keepdims=True)
    expert_of_tile = jnp.sum(jnp.where(first_tile <= tile, 1.0, 0.0), axis=0, keepdims=True) - 1.0
    valid = jnp.where(tile < total, 1.0, 0.0)
    ends_expert = jnp.sum(jnp.where((first_tile + ntile == tile + 1.0) & (ntile > 0.0), 1.0, 0.0),
                          axis=0, keepdims=True)
    needs_zero = jnp.where((ends_expert > 0.0) | (valid == 0.0), 1.0, 0.0)
    zrow = jnp.zeros((5, LANES), jnp.int32)
    info_ref[...] = jnp.concatenate(
        [expert_of_tile.astype(jnp.int32), valid.astype(jnp.int32), needs_zero.astype(jnp.int32),
         zrow], axis=0)


def _moe_plan(route, tri, *, n_tiles):
    n = route.shape[1]
    return pl.pallas_call(
        functools.partial(_moe_plan_kernel, n=n, n_tiles=n_tiles),
        grid=(1,),
        in_specs=[pl.BlockSpec((8, n), lambda i: (0, 0)),
                  pl.BlockSpec((PLAN_BLOCK, PLAN_BLOCK), lambda i: (0, 0))],
        out_specs=[pl.BlockSpec((8, n), lambda i: (0, 0)),
                   pl.BlockSpec((8, LANES), lambda i: (0, 0))],
        out_shape=[jax.ShapeDtypeStruct((8, n), jnp.int32),
                   jax.ShapeDtypeStruct((8, LANES), jnp.int32)],
        compiler_params=_params(),
        name="moe_plan",
    )(route, tri)


def _token_tile(ref, t):
    return ref.at[pl.ds(pl.multiple_of(t * TOKEN_TILE_ROWS, TOKEN_TILE_ROWS), TOKEN_TILE_ROWS)]


def _token_tiles(ref, first, count):
    rows = count * TOKEN_TILE_ROWS
    return ref.at[pl.ds(pl.multiple_of(first * TOKEN_TILE_ROWS, TOKEN_TILE_ROWS), rows)]


def _dispatch_kernel(zero_ref, pos_ref, h_ref, xs_hbm, zbuf, zsem, sem, *, n_tiles):
    i = pl.program_id(0)

    @pl.when(i == 0)
    def _():
        zbuf[...] = jnp.zeros_like(zbuf)

        def fill_copy(j):
            return pltpu.make_async_copy(zbuf, _token_tiles(xs_hbm, j * EXPERT_TILE, EXPERT_TILE), zsem)

        def fill(j, c):
            @pl.when(zero_ref[j] == 1)
            def _():
                fill_copy(j).start()
            return c

        def drain(j, c):
            @pl.when(zero_ref[j] == 1)
            def _():
                fill_copy(j).wait()
            return c

        lax.fori_loop(0, n_tiles, fill, 0)
        lax.fori_loop(0, n_tiles, drain, 0)

    for r in range(PLAN_BLOCK):
        src = h_ref.at[TOKEN_TILE_ROWS * r:TOKEN_TILE_ROWS * (r + 1)]
        for s in range(2):
            pltpu.make_async_copy(src, _token_tile(xs_hbm, pos_ref[s, r]), sem).start(priority=s)
    for _ in range(2):
        pltpu.make_async_copy(h_ref, _token_tiles(xs_hbm, 0, PLAN_BLOCK), sem).wait()


def _dispatch(needs_zero, pos, h2t, *, n_tiles):
    n = h2t.shape[0] // TOKEN_TILE_ROWS
    return pl.pallas_call(
        functools.partial(_dispatch_kernel, n_tiles=n_tiles),
        grid_spec=pltpu.PrefetchScalarGridSpec(
            num_scalar_prefetch=1,
            grid=(n // PLAN_BLOCK,),
            in_specs=[pl.BlockSpec((8, PLAN_BLOCK), lambda i, z: (0, i), memory_space=pltpu.SMEM),
                      pl.BlockSpec((PLAN_BLOCK * TOKEN_TILE_ROWS, LANES), lambda i, z: (i, 0))],
            out_specs=pl.BlockSpec(memory_space=pl.ANY),
            scratch_shapes=[pltpu.VMEM((EXPERT_TILE * TOKEN_TILE_ROWS, LANES), F32),
                            pltpu.SemaphoreType.DMA(()), pltpu.SemaphoreType.DMA(())]),
        out_shape=jax.ShapeDtypeStruct((n_tiles * EXPERT_TILE * TOKEN_TILE_ROWS, LANES), F32),
        compiler_params=_params(),
        name="moe_dispatch",
    )(needs_zero, pos, h2t)


def _experts_kernel(expert_ref, valid_ref, x_ref, w1_ref, w3_ref, w2_ref, o_ref):
    j = pl.program_id(0)

    @pl.when(valid_ref[j] == 1)
    def _():
        x = _load_token_tiles(x_ref, EXPERT_TILE).astype(BF16)
        a = _dot(x, w1_ref[0].astype(BF16))
        b = _dot(x, w3_ref[0].astype(BF16))
        hid = (a * _sigmoid(a)) * b
        _store_token_tiles(o_ref, _dot(hid.astype(BF16), w2_ref[0].astype(BF16)))

    @pl.when(valid_ref[j] == 0)
    def _():
        o_ref[...] = jnp.zeros_like(o_ref)


def _experts(expert_of_tile, valid, xs, w1, w3, w2, layer, *, n_tiles):
    rows = EXPERT_TILE * TOKEN_TILE_ROWS
    wmap = lambda j, e, v: (layer * N_EXPERTS + e[j], 0, 0)
    return pl.pallas_call(
        _experts_kernel,
        grid_spec=pltpu.PrefetchScalarGridSpec(
            num_scalar_prefetch=2,
            grid=(n_tiles,),
            in_specs=[pl.BlockSpec((rows, LANES), lambda j, e, v: (j, 0)),
                      pl.BlockSpec((1, D_MODEL, EXPERT_FF), wmap),
                      pl.BlockSpec((1, D_MODEL, EXPERT_FF), wmap),
                      pl.BlockSpec((1, EXPERT_FF, D_MODEL), wmap)],
            out_specs=pl.BlockSpec((rows, LANES), lambda j, e, v: (j, 0))),
        out_shape=jax.ShapeDtypeStruct((n_tiles * rows, LANES), F32),
        compiler_params=_params(),
        name="moe_experts",
    )(expert_of_tile, valid, xs, w1, w3, w2)


def _combine_kernel(pos_ref, next_pos_ref, ys_hbm, wts_ref, x1_ref, mod_ref, fg_ref, o_ref, buf, sem,
                    *, final):
    i = pl.program_id(0)
    last = pl.num_programs(0) - 1
    slot = i % 2

    def gather(p_ref, sl):
        for r in range(PLAN_BLOCK):
            for s in range(2):
                dst = buf.at[sl, s, TOKEN_TILE_ROWS * r:TOKEN_TILE_ROWS * (r + 1)]
                pltpu.make_async_copy(_token_tile(ys_hbm, p_ref[s, r]), dst, sem.at[sl]).start(
                    priority=s)

    def drain(sl):
        for s in range(2):
            pltpu.make_async_copy(_token_tiles(ys_hbm, 0, PLAN_BLOCK), buf.at[sl, s], sem.at[sl]).wait()

    @pl.when(i == 0)
    def _():
        gather(pos_ref, 0)

    gather(next_pos_ref, 1 - slot)
    drain(slot)
    y0 = _load_token_tiles(buf, PLAN_BLOCK, lead=(slot, 0))
    y1 = _load_token_tiles(buf, PLAN_BLOCK, lead=(slot, 1))
    wts = wts_ref[...]
    moe = wts[:, 0:1] * y0 + wts[:, 1:2] * y1
    x2 = x1_ref[...] + mod_ref[0, 0, 5:6, :] * moe
    if final:
        ms = jnp.mean(x2 * x2, axis=-1, keepdims=True)
        x2 = x2 * lax.rsqrt(ms + EPS) * fg_ref[...]
    o_ref[...] = x2

    @pl.when(i == last)
    def _():
        drain(1 - slot)


def _combine(pos, ys, wts, x1, mod, layer, mod_row, fg, *, final):
    n = x1.shape[0]
    nb = n // PLAN_BLOCK
    row = lambda i: (i, 0)
    return pl.pallas_call(
        functools.partial(_combine_kernel, final=final),
        grid=(nb,),
        in_specs=[
            pl.BlockSpec((8, PLAN_BLOCK), lambda i: (0, i), memory_space=pltpu.SMEM),
            pl.BlockSpec((8, PLAN_BLOCK), lambda i: (0, jnp.minimum(i + 1, nb - 1)),
                         memory_space=pltpu.SMEM),
            pl.BlockSpec(memory_space=pl.ANY),
            pl.BlockSpec((PLAN_BLOCK, LANES), row),
            pl.BlockSpec((PLAN_BLOCK, D_MODEL), row),
            pl.BlockSpec((1, 1, N_MOD, D_MODEL), lambda i: (layer, mod_row(i), 0, 0)),
            pl.BlockSpec((1, D_MODEL), lambda i: (0, 0)),
        ],
        out_specs=pl.BlockSpec((PLAN_BLOCK, D_MODEL), row),
        out_shape=jax.ShapeDtypeStruct((n, D_MODEL), F32),
        scratch_shapes=[pltpu.VMEM((2, 2, PLAN_BLOCK * TOKEN_TILE_ROWS, LANES), F32),
                        pltpu.SemaphoreType.DMA((2,))],
        compiler_params=pltpu.CompilerParams(vmem_limit_bytes=VMEM_LIMIT_BYTES,
                                             dimension_semantics=("arbitrary",)),
        name="moe_combine",
    )(pos, pos, ys, wts, x1, mod, fg)


def _moe(h2t, route, wts, x1, mod, layer, mod_row, w1, w3, w2, fg, tri, *, final):
    n = x1.shape[0]
    n_tiles = 2 * n // EXPERT_TILE + N_EXPERTS
    pos, info = _moe_plan(route, tri, n_tiles=n_tiles)
    expert_of_tile, valid, needs_zero = info[0, :n_tiles], info[1, :n_tiles], info[2, :n_tiles]
    xs = _dispatch(needs_zero, pos, h2t, n_tiles=n_tiles)
    ys = _experts(expert_of_tile, valid, xs, w1, w3, w2, layer, n_tiles=n_tiles)
    return _combine(pos, ys, wts, x1, mod, layer, mod_row, fg, final=final)


def _rope_tables(n_tokens, rot_dim):
    pos = np.arange(n_tokens)
    r = (pos // GRID_W).astype(np.float64)[:, None]
    col = (pos % GRID_W).astype(np.float64)[:, None]
    n_freq = rot_dim // 4
    inv_freq = ROPE_THETA ** (-np.arange(n_freq, dtype=np.float64) / n_freq)
    ang = np.concatenate([r * inv_freq, col * inv_freq], axis=-1)
    ang = np.repeat(ang, 2, axis=-1)
    sign = np.tile(np.array([-1.0, 1.0]), rot_dim // 2)
    cos = np.tile(np.cos(ang), (1, LANES // rot_dim))
    sin = np.tile(np.sin(ang) * sign, (1, LANES // rot_dim))
    return jnp.asarray(cos, F32), jnp.asarray(sin, F32)


def _block_diag_ones():
    idx = np.arange(2 * LANES) // 64
    return jnp.asarray((idx[:, None] == idx[None, :]).astype(np.float32), BF16)


def _pair_heads_by_kv_group(w, axis):
    shape = w.shape
    lead, tail = shape[:axis], shape[axis + 1:]
    w = w.reshape(lead + (A_KV_HEADS, A_HEADS // A_KV_HEADS, A_HEAD_DIM) + tail)
    w = jnp.swapaxes(w, axis, axis + 1)
    return w.reshape(shape)


def kernel(x_prompt, x_sample, cache_gqa_k, cache_gqa_v, cache_diff_k, cache_diff_v, c, c_ctx, w_in,
           w_out, norm1_g, norm2_g, ada_w, ada_b, gqa_qnorm_g, gqa_knorm_g, conv_w, diff_lambda,
           diff_subln_g, router_w, router_bias, moe_w1, moe_w3, moe_w2, final_g):
    n_ctx_b, t_ctx, _ = x_prompt.shape
    n_lat_b, t_lat, _ = x_sample.shape
    past = cache_gqa_k.shape[2]
    n_q = A_HEADS * A_HEAD_DIM

    cvec = jnp.concatenate([c_ctx[None, :], c, jnp.zeros((8 - 1 - n_lat_b, D_MODEL), F32)], axis=0)
    mod = _adaln(cvec, ada_w, ada_b).reshape(DEPTH, 8, N_MOD, D_MODEL)

    w_in_b = jnp.concatenate(
        [_pair_heads_by_kv_group(w_in[:, :, :n_q], 2), w_in[:, :, n_q:]], axis=2).astype(BF16)
    w_out_b = jnp.concatenate(
        [_pair_heads_by_kv_group(w_out[:, :n_q, :], 1), w_out[:, n_q:, :]], axis=1).astype(BF16)
    w1_b = moe_w1.reshape(DEPTH * N_EXPERTS, D_MODEL, EXPERT_FF)
    w3_b = moe_w3.reshape(DEPTH * N_EXPERTS, D_MODEL, EXPERT_FF)
    w2_b = moe_w2.reshape(DEPTH * N_EXPERTS, EXPERT_FF, D_MODEL)
    rw_t = router_w.T
    rw_hi = rw_t.astype(BF16)
    rwt = jnp.concatenate([rw_hi, (rw_t - rw_hi.astype(F32)).astype(BF16)], axis=0)
    rb = router_bias.reshape(N_EXPERTS, 1)
    bd = _block_diag_ones()
    tri = jnp.asarray(np.triu(np.ones((PLAN_BLOCK, PLAN_BLOCK), np.float32), k=1), BF16)
    fg = final_g.reshape(1, D_MODEL)
    tables = _rope_tables(t_lat, A_HEAD_DIM) + _rope_tables(t_lat, C_QK_DIM)

    ck_cache = cache_gqa_k.reshape(n_lat_b * DEPTH, past, A_KV_HEADS * A_HEAD_DIM)
    cv_cache = cache_gqa_v.reshape(n_lat_b * DEPTH, past, A_KV_HEADS * A_HEAD_DIM)
    dk_cache = cache_diff_k.reshape(n_lat_b * DEPTH, past, C_HEADS * 2 * C_QK_DIM)
    dv_cache = cache_diff_v.reshape(n_lat_b * DEPTH, past, C_HEADS * C_V_DIM)

    xp = x_prompt.reshape(n_ctx_b * t_ctx, D_MODEL)
    xs = x_sample.reshape(n_lat_b * t_lat, D_MODEL)
    tm = 512
    tq = 256
    state = tuple(jnp.zeros((n_ctx_b, DEPTH, t_ctx, w), F32)
                  for w in (LANES, LANES, 2 * LANES, 2 * LANES))
    for l in range(DEPTH):
        lam_init = 0.8 - 0.6 * math.exp(-0.3 * l)
        g1 = norm1_g[l].reshape(1, D_MODEL)
        g2 = norm2_g[l].reshape(1, D_MODEL)
        qg = jnp.tile(gqa_qnorm_g[l], 4).reshape(1, 2 * LANES)
        kg = jnp.tile(gqa_knorm_g[l], 4).reshape(1, 2 * LANES)
        sg = jnp.tile(diff_subln_g[l], 2).reshape(1, LANES)
        final = l == DEPTH - 1

        def run_stream(x, n_batch, t_seq, mod_row_of, tabs, caches, nj_gqa, nj_diff, key_group,
                       state_in=None):
            ctx = caches is None
            outs = _inproj(x, mod, l, lambda i: mod_row_of(i, tm), g1, w_in_b[l], qg, kg, bd, tabs,
                           tm=tm, ctx=ctx, key_group=key_group, t_seq=t_seq, state_in=state_in)
            q, k, vt, gb, gu, cq, ck, cvt = outs[:8]
            c_gk, c_gv, c_dk, c_dv = caches if caches is not None else (None,) * 4
            oa = _gqa_attention(q, k, vt, c_gk, c_gv, l, n_batch=n_batch, t_seq=t_seq, tq=tq,
                                nj=nj_gqa)
            oc = _diff_attention(cq, ck, cvt, c_dk, c_dv, l, diff_lambda, sg, n_batch=n_batch,
                                 t_seq=t_seq, tq=tq, nj=nj_diff, lam_init=lam_init)
            x1, h2t, route, wts = _outproj(oa, gb, gu, conv_w, oc, x, mod, l,
                                           lambda i: mod_row_of(i, tm), w_out_b[l], g2, rwt, rb,
                                           tm=tm, t_seq=t_seq)
            x2 = _moe(h2t, route, wts, x1, mod, l, lambda i: mod_row_of(i, PLAN_BLOCK), w1_b, w3_b,
                      w2_b, fg, tri, final=final)
            return x2, outs[8:]

        xp, state = run_stream(xp, n_ctx_b, t_ctx, lambda i, t: 0, None, None, 4, 2, t_ctx, state)
        xs, _ = run_stream(xs, n_lat_b, t_lat, lambda i, t: 1 + (i * t) // t_lat, tables,
                           (ck_cache, cv_cache, dk_cache, dv_cache), 4, 2, 256)

    def heads(a, n_heads, dim):
        return a.reshape(n_ctx_b, DEPTH, t_ctx, n_heads, dim)

    return (xp.reshape(n_ctx_b, t_ctx, D_MODEL), xs.reshape(n_lat_b, t_lat, D_MODEL),
            heads(state[0], A_KV_HEADS, A_HEAD_DIM), heads(state[1], A_KV_HEADS, A_HEAD_DIM),
            heads(state[2], C_HEADS, 2 * C_QK_DIM), heads(state[3], C_HEADS, C_V_DIM))
```

```python
import functools
import math

import numpy as np
import jax
import jax.numpy as jnp
from jax import lax
from jax.experimental import pallas as pl
from jax.experimental.pallas import tpu as pltpu

D_MODEL = 1024
DEPTH = 2
GRID_W = 64
A_HEADS = 8
A_KV_HEADS = 2
A_HEAD_DIM = 64
B_WIDTH = 256
C_HEADS = 4
C_QK_DIM = 32
C_V_DIM = 64
N_EXPERTS = 16
EXPERTS_PER_GROUP = 4
N_EXPERT_GROUPS = 4
EXPERT_FF = 512
ROPE_THETA = 10000.0
EPS = 1e-6
N_MOD = 6
W_IN_WIDTH = 2304
LOG2E = math.log2(math.e)

LANES = 128
HEAD_ROWS = 64
ONES_ROWS = 16
VMEM_LIMIT_BYTES = 56 * 1024 * 1024

F32 = jnp.float32
BF16 = jnp.bfloat16

_dot = functools.partial(jnp.dot, preferred_element_type=F32)


def _dot_nt(a, b):
    return lax.dot_general(a, b, (((1,), (1,)), ((), ())), preferred_element_type=F32)


def _split(a):
    hi = a.astype(BF16)
    lo = (a - hi.astype(F32)).astype(BF16)
    return hi, lo


def _dot3(a, b):
    ah, al = _split(a)
    bh, bl = _split(b)
    return _dot(ah, bh) + _dot(ah, bl) + _dot(al, bh)


def _sigmoid(x):
    return 1.0 / (1.0 + jnp.exp(-x))


def _seg_rms(x, bd, g):
    ss = _dot((x * x).astype(BF16), bd)
    return x * lax.rsqrt(ss * (1.0 / 64.0) + EPS) * g


def _rope(x, cos, sin_signed):
    lane = lax.broadcasted_iota(jnp.int32, x.shape, 1)
    even = (lane & 1) == 0
    partner = jnp.where(even, pltpu.roll(x, LANES - 1, 1), pltpu.roll(x, 1, 1))
    return x * cos + partner * sin_signed


def _params():
    return pltpu.CompilerParams(vmem_limit_bytes=VMEM_LIMIT_BYTES)


def _adaln_kernel(c_ref, w_ref, b_ref, o_ref):
    c = c_ref[...]
    s = c * _sigmoid(c)
    o_ref[0] = _dot3(s, w_ref[0]) + b_ref[0]


def _adaln(cvec, ada_w, ada_b):
    tn = 1536
    nt = (N_MOD * D_MODEL) // tn
    return pl.pallas_call(
        _adaln_kernel,
        grid=(DEPTH, nt),
        in_specs=[
            pl.BlockSpec((8, D_MODEL), lambda l, n: (0, 0)),
            pl.BlockSpec((1, D_MODEL, tn), lambda l, n: (l, 0, n)),
            pl.BlockSpec((1, 1, tn), lambda l, n: (l, 0, n)),
        ],
        out_specs=pl.BlockSpec((1, 8, tn), lambda l, n: (l, 0, n)),
        out_shape=jax.ShapeDtypeStruct((DEPTH, 8, N_MOD * D_MODEL), F32),
        compiler_params=_params(),
        name="adaln",
    )(cvec, ada_w, ada_b.reshape(DEPTH, 1, N_MOD * D_MODEL))


def _inproj_kernel(*refs, rope, ctx, kg, n_state_in):
    it = iter(refs)
    x_ref, mod_ref, g1_ref, w_ref, qg_ref, kg_ref, bd_ref = (next(it) for _ in range(7))
    if rope:
        cos_a, sin_a, cos_c, sin_c = (next(it) for _ in range(4))
    for _ in range(n_state_in):
        next(it)
    q_ref, k_ref, vt_ref, gb_ref, gu_ref, cq_ref, ck_ref, cvt_ref = (next(it) for _ in range(8))
    if ctx:
        k32_ref, v32_ref, ck32_ref, cv32_ref = (next(it) for _ in range(4))

    def store_state(ref, a):
        ref[...] = a.reshape(ref.shape)

    x = x_ref[...]
    ms = jnp.mean(x * x, axis=-1, keepdims=True)
    y = x * lax.rsqrt(ms + EPS) * g1_ref[...]
    h = y * (1.0 + mod_ref[0, 0, 1:2, :]) + mod_ref[0, 0, 0:1, :]
    hb = h.astype(BF16)
    bd = bd_ref[...]

    def seg(lo, width):
        return _dot(hb, w_ref[:, lo:lo + width])

    def store_transposed(dst, a):
        at = a.T.astype(BF16)
        for c in range(a.shape[0] // kg):
            dst[c] = at[:, kg * c:kg * (c + 1)]

    halves = lambda a: (a[:, :LANES], a[:, LANES:])

    for jj in range(2):
        qq = _seg_rms(seg(2 * LANES * jj, 2 * LANES), bd, qg_ref[...])
        for j, q in zip((2 * jj, 2 * jj + 1), halves(qq)):
            if rope:
                q = _rope(q, cos_a[...], sin_a[...])
            q_ref[j] = (q * (A_HEAD_DIM ** -0.5 * LOG2E)).astype(BF16)

    kv = seg(512, 2 * LANES)
    k, v = halves(_seg_rms(kv, bd, kg_ref[...]))[0], kv[:, LANES:]
    if rope:
        k = _rope(k, cos_a[...], sin_a[...])
    if ctx:
        store_state(k32_ref, k)
        store_state(v32_ref, v)
    k_ref[0] = k.astype(BF16)
    store_transposed(vt_ref.at[0], v)

    gb_ref[...] = seg(768, B_WIDTH).astype(BF16)
    gu_ref[...] = (seg(1024, B_WIDTH) * seg(1280, B_WIDTH)).astype(BF16)

    cq2, ck2, cv2 = seg(1536, 2 * LANES), seg(1792, 2 * LANES), seg(2048, 2 * LANES)
    if ctx:
        store_state(ck32_ref, ck2)
        store_state(cv32_ref, cv2)
    for j, (cq, ck, cv) in enumerate(zip(halves(cq2), halves(ck2), halves(cv2))):
        if rope:
            cq = _rope(cq, cos_c[...], sin_c[...])
            ck = _rope(ck, cos_c[...], sin_c[...])
        cq_ref[j] = (cq * (C_QK_DIM ** -0.5 * LOG2E)).astype(BF16)
        ck_ref[j] = ck.astype(BF16)
        store_transposed(cvt_ref.at[j], cv)


def _inproj(x, mod, layer, mod_row, g1, w_in, qg, kg, bd, tables, *, tm, ctx, key_group, t_seq,
            state_in):
    n = x.shape[0]
    rope = tables is not None
    row = lambda i: (i, 0)
    const2 = lambda i: (0, 0)
    pair3 = lambda i: (0, i, 0)
    tile4 = lambda i: (0, i, 0, 0)
    nc = tm // key_group
    in_specs = [
        pl.BlockSpec((tm, D_MODEL), row),
        pl.BlockSpec((1, 1, N_MOD, D_MODEL), lambda i: (layer, mod_row(i), 0, 0)),
        pl.BlockSpec((1, D_MODEL), const2),
        pl.BlockSpec((D_MODEL, W_IN_WIDTH), const2),
        pl.BlockSpec((1, 2 * LANES), const2),
        pl.BlockSpec((1, 2 * LANES), const2),
        pl.BlockSpec((2 * LANES, 2 * LANES), const2),
    ]
    args = [x, mod, g1, w_in, qg, kg, bd]
    if rope:
        t = tables[0].shape[0]
        tab = lambda i: (i % (t // tm), 0)
        in_specs += [pl.BlockSpec((tm, LANES), tab)] * 4
        args += list(tables)
    out_specs = [
        pl.BlockSpec((4, tm, LANES), pair3),
        pl.BlockSpec((1, tm, LANES), pair3),
        pl.BlockSpec((1, nc, LANES, key_group), tile4),
        pl.BlockSpec((tm, B_WIDTH), row),
        pl.BlockSpec((tm, B_WIDTH), row),
        pl.BlockSpec((2, tm, LANES), pair3),
        pl.BlockSpec((2, tm, LANES), pair3),
        pl.BlockSpec((2, nc, LANES, key_group), tile4),
    ]
    out_shape = [
        jax.ShapeDtypeStruct((4, n, LANES), BF16),
        jax.ShapeDtypeStruct((1, n, LANES), BF16),
        jax.ShapeDtypeStruct((1, n // key_group, LANES, key_group), BF16),
        jax.ShapeDtypeStruct((n, B_WIDTH), BF16),
        jax.ShapeDtypeStruct((n, B_WIDTH), BF16),
        jax.ShapeDtypeStruct((2, n, LANES), BF16),
        jax.ShapeDtypeStruct((2, n, LANES), BF16),
        jax.ShapeDtypeStruct((2, n // key_group, LANES, key_group), BF16),
    ]
    aliases = {}
    n_state_in = 0
    if ctx:
        seqs = tm // t_seq
        for width in (LANES, LANES, 2 * LANES, 2 * LANES):
            out_specs.append(pl.BlockSpec((seqs, 1, t_seq, width), lambda i: (i, layer, 0, 0)))
            out_shape.append(jax.ShapeDtypeStruct((n // t_seq, DEPTH, t_seq, width), F32))
        if state_in is not None:
            n_state_in = len(state_in)
            for s, a in enumerate(state_in):
                aliases[len(args)] = len(out_shape) - n_state_in + s
                in_specs.append(pl.BlockSpec(memory_space=pl.ANY))
                args.append(a)
    return pl.pallas_call(
        functools.partial(_inproj_kernel, rope=rope, ctx=ctx, kg=key_group, n_state_in=n_state_in),
        grid=(n // tm,),
        in_specs=in_specs,
        out_specs=out_specs,
        out_shape=out_shape,
        input_output_aliases=aliases,
        compiler_params=_params(),
        name="inproj_ctx" if ctx else "inproj_lat",
    )(*args)


def _attend(units, k_ref, vt_ref, caches):
    tq = units[0][0].shape[0]
    nu = len(units)
    kg = vt_ref.shape[-1]
    n_groups = k_ref.shape[1] // kg
    blocks = sorted({j for _, _, j in units})
    n_stages = n_groups + (0 if caches is None else 1)

    def load(stage):
        if caches is not None:
            if stage == 0:
                return {j: (caches[j][0].astype(BF16), caches[j][1].T.astype(BF16)) for j in blocks}
            stage -= 1
        return {j: (k_ref[j, kg * stage:kg * (stage + 1), :], vt_ref[j, stage]) for j in blocks}

    def value_rows(vt, half):
        own = vt[HEAD_ROWS * half:HEAD_ROWS * (half + 1), :]
        return jnp.concatenate([own, jnp.ones((ONES_ROWS, vt.shape[1]), BF16)], axis=0)

    def score(kv):
        return [_dot_nt(kv[j][0], q) for q, _, j in units]

    ms = [jnp.full((1, tq), -jnp.inf, F32)] * nu
    accs = [jnp.zeros((HEAD_ROWS + ONES_ROWS, tq), F32)] * nu
    kv = load(0)
    scores = score(kv)
    for s in range(n_stages):
        cur_scores, cur_kv = scores, kv
        if s + 1 < n_stages:
            kv = load(s + 1)
            scores = score(kv)
        new_ms, ps = [], []
        for u in range(nu):
            half = cur_scores[u].shape[0] // 2
            s1, s2 = cur_scores[u][:half], cur_scores[u][half:]
            m1 = jnp.maximum(ms[u], jnp.max(s1, axis=0, keepdims=True))
            p1 = jnp.exp2(s1 - m1).astype(BF16)
            m = jnp.maximum(m1, jnp.max(s2, axis=0, keepdims=True))
            p2 = jnp.exp2(s2 - m).astype(BF16)
            new_ms.append(m)
            ps.append(jnp.concatenate([p1 * jnp.exp2(m1 - m).astype(BF16), p2], axis=0))
        pvs = [_dot(value_rows(cur_kv[j][1], half), ps[u]) for u, (_, half, j) in enumerate(units)]
        accs = [jnp.exp2(ms[u] - new_ms[u]) * accs[u] + pvs[u] for u in range(nu)]
        ms = new_ms
    return [acc[0:HEAD_ROWS] / acc[HEAD_ROWS:HEAD_ROWS + 1] for acc in accs]


def _lane_mask(q, lo, width):
    lane = lax.broadcasted_iota(jnp.int32, q.shape, 1)
    return jnp.where((lane >= lo) & (lane < lo + width), q, jnp.zeros_like(q))


def _gqa_kernel(*refs, has_cache, nj):
    if has_cache:
        q_ref, k_ref, vt_ref, ck_ref, cv_ref, o_ref = refs
        caches = {0: (ck_ref[0], cv_ref[0])}
    else:
        q_ref, k_ref, vt_ref, o_ref = refs
        caches = None
    units = [(_lane_mask(q_ref[j], 64 * u, 64), u, 0) for j in range(nj) for u in range(2)]
    outs = _attend(units, k_ref, vt_ref, caches)
    for j in range(nj):
        ot = jnp.concatenate(outs[2 * j:2 * j + 2], axis=0)
        o_ref[:, LANES * j:LANES * (j + 1)] = ot.T.astype(BF16)


def _gqa_attention(q, k, vt, cache_k, cache_v, layer, *, n_batch, t_seq, tq, nj):
    n = q.shape[1]
    nq = t_seq // tq
    kg = vt.shape[-1]
    has_cache = cache_k is not None
    in_specs = [
        pl.BlockSpec((nj, tq, LANES), lambda b, i, j: (j, b * nq + i, 0)),
        pl.BlockSpec((1, t_seq, LANES), lambda b, i, j: (0, b, 0)),
        pl.BlockSpec((1, t_seq // kg, LANES, kg), lambda b, i, j: (0, b, 0, 0)),
    ]
    args = [q, k, vt]
    if has_cache:
        past = cache_k.shape[1]
        cspec = pl.BlockSpec((1, past, LANES), lambda b, i, j: (b * DEPTH + layer, 0, 0))
        in_specs += [cspec, cspec]
        args += [cache_k, cache_v]
    return pl.pallas_call(
        functools.partial(_gqa_kernel, has_cache=has_cache, nj=nj),
        grid=(n_batch, nq, 4 // nj),
        in_specs=in_specs,
        out_specs=pl.BlockSpec((tq, LANES * nj), lambda b, i, j: (b * nq + i, j)),
        out_shape=jax.ShapeDtypeStruct((n, 4 * LANES), BF16),
        compiler_params=_params(),
        name="gqa_lat" if has_cache else "gqa_ctx",
    )(*args)


def _diff_kernel(*refs, has_cache, nj, lam_init):
    if has_cache:
        q_ref, k_ref, vt_ref, ck_ref, cv_ref, lam_ref, g_ref, o_ref = refs
    else:
        q_ref, k_ref, vt_ref, lam_ref, g_ref, o_ref = refs
    lv = lam_ref[0]
    lam = (jnp.exp(jnp.sum(lv[0:1] * lv[1:2], axis=-1, keepdims=True))
           - jnp.exp(jnp.sum(lv[2:3] * lv[3:4], axis=-1, keepdims=True)) + lam_init)
    units = [(_lane_mask(q_ref[j], 32 * c, 32), c // 2, j) for j in range(nj) for c in range(4)]
    caches = None
    if has_cache:
        caches = {j: (ck_ref[0, :, LANES * j:LANES * (j + 1)], cv_ref[0, :, LANES * j:LANES * (j + 1)])
                  for j in range(nj)}
    outs = _attend(units, k_ref, vt_ref, caches)
    for j in range(nj):
        res = outs[4 * j:4 * j + 4]
        heads = []
        for i in range(2):
            o = res[2 * i] - lam * res[2 * i + 1]
            ms = jnp.mean(o * o, axis=0, keepdims=True)
            heads.append(o * lax.rsqrt(ms + EPS))
        o = jnp.concatenate(heads, axis=0).T * g_ref[...] * (1.0 - lam_init)
        o_ref[:, LANES * j:LANES * (j + 1)] = o.astype(BF16)


def _diff_attention(cq, ck, cvt, cache_k, cache_v, layer, diff_lambda, subln_g, *,
                    n_batch, t_seq, tq, nj, lam_init):
    n = cq.shape[1]
    nq = t_seq // tq
    kg = cvt.shape[-1]
    has_cache = cache_k is not None
    in_specs = [
        pl.BlockSpec((nj, tq, LANES), lambda b, i, j: (j, b * nq + i, 0)),
        pl.BlockSpec((nj, t_seq, LANES), lambda b, i, j: (j, b, 0)),
        pl.BlockSpec((nj, t_seq // kg, LANES, kg), lambda b, i, j: (j, b, 0, 0)),
    ]
    args = [cq, ck, cvt]
    if has_cache:
        past = cache_k.shape[1]
        cspec = pl.BlockSpec((1, past, LANES * nj), lambda b, i, j: (b * DEPTH + layer, 0, j))
        in_specs += [cspec, cspec]
        args += [cache_k, cache_v]
    in_specs += [
        pl.BlockSpec((1, 4, C_QK_DIM), lambda b, i, j: (layer, 0, 0)),
        pl.BlockSpec((1, LANES), lambda b, i, j: (0, 0)),
    ]
    args += [diff_lambda, subln_g]
    return pl.pallas_call(
        functools.partial(_diff_kernel, has_cache=has_cache, nj=nj, lam_init=lam_init),
        grid=(n_batch, nq, 2 // nj),
        in_specs=in_specs,
        out_specs=pl.BlockSpec((tq, LANES * nj), lambda b, i, j: (b * nq + i, j)),
        out_shape=jax.ShapeDtypeStruct((n, 2 * LANES), BF16),
        compiler_params=_params(),
        name="diff_lat" if has_cache else "diff_ctx",
    )(*args)


HALO = 16


def _route(logits_t, bias_col):
    scores = _sigmoid(logits_t)
    sel = scores + bias_col
    row = lambda a, e: a[e:e + 1, :]
    grp = []
    for g in range(N_EXPERT_GROUPS):
        s0, s1, s2, s3 = (row(sel, 4 * g + k) for k in range(4))
        hi1, lo1 = jnp.maximum(s0, s1), jnp.minimum(s0, s1)
        hi2, lo2 = jnp.maximum(s2, s3), jnp.minimum(s2, s3)
        grp.append(jnp.maximum(hi1, hi2) + jnp.maximum(jnp.minimum(hi1, hi2), jnp.maximum(lo1, lo2)))
    best = grp[0]
    tg = jnp.zeros_like(best, dtype=jnp.int32)
    for g in range(1, N_EXPERT_GROUPS):
        better = grp[g] > best
        best = jnp.where(better, grp[g], best)
        tg = jnp.where(better, g, tg)

    def in_group(a, k):
        out = row(a, k)
        for g in range(1, N_EXPERT_GROUPS):
            out = jnp.where(tg == g, row(a, 4 * g + k), out)
        return out

    vals = [in_group(sel, k) for k in range(EXPERTS_PER_GROUP)]
    scs = [in_group(scores, k) for k in range(EXPERTS_PER_GROUP)]

    def first_argmax(vs):
        bv, bi = vs[0], jnp.zeros_like(tg)
        for k in range(1, len(vs)):
            better = vs[k] > bv
            bv = jnp.where(better, vs[k], bv)
            bi = jnp.where(better, k, bi)
        return bi

    i1 = first_argmax(vals)
    i2 = first_argmax([jnp.where(i1 == k, -jnp.inf, vals[k]) for k in range(EXPERTS_PER_GROUP)])

    def pick(vs, idx):
        out = vs[0]
        for k in range(1, len(vs)):
            out = jnp.where(idx == k, vs[k], out)
        return out

    w1, w2 = pick(scs, i1), pick(scs, i2)
    tot = w1 + w2
    return 4 * tg + i1, 4 * tg + i2, w1 / tot, w2 / tot


def _outproj_kernel(oa_ref, gb_ref, gu_ref, gup_ref, gun_ref, cw_ref, oc_ref, x_ref, mod_ref,
                    w_ref, g2_ref, rwt_ref, rb_ref,
                    x1_ref, h2t_ref, route_ref, wts_ref, *, tm, t_seq):
    i = pl.program_id(0)
    gu = gu_ref[...].astype(F32)
    rowi = lax.broadcasted_iota(jnp.int32, gu.shape, 0)
    pos = (i * tm + rowi) % t_seq
    prev = pltpu.roll(gu, 1, 0)
    prev = jnp.where(rowi == 0, gup_ref[HALO - 1:HALO, :].astype(F32), prev)
    prev = jnp.where(pos == 0, 0.0, prev)
    nxt = pltpu.roll(gu, tm - 1, 0)
    nxt = jnp.where(rowi == tm - 1, gun_ref[0:1, :].astype(F32), nxt)
    nxt = jnp.where(pos == t_seq - 1, 0.0, nxt)
    cw = cw_ref[0]
    ob = gb_ref[...].astype(F32) * (prev * cw[0:1] + gu * cw[1:2] + nxt * cw[2:3])

    mix = (_dot(oa_ref[...], w_ref[0:512, :])
           + _dot(ob.astype(BF16), w_ref[512:768, :])
           + _dot(oc_ref[...], w_ref[768:1024, :]))
    x1 = x_ref[...] + mod_ref[0, 0, 2:3, :] * mix
    x1_ref[...] = x1
    ms = jnp.mean(x1 * x1, axis=-1, keepdims=True)
    h2 = x1 * lax.rsqrt(ms + EPS) * g2_ref[...]
    h2 = h2 * (1.0 + mod_ref[0, 0, 4:5, :]) + mod_ref[0, 0, 3:4, :]
    _store_token_tiles(h2t_ref, h2)

    hh, hl = _split(h2)
    rwt = rwt_ref[...]
    both = _dot_nt(rwt, hh)
    logits_t = both[0:N_EXPERTS] + both[N_EXPERTS:2 * N_EXPERTS] + _dot_nt(rwt[0:N_EXPERTS], hl)
    e0, e1, w0, w1 = _route(logits_t, rb_ref[...])
    zeros = jnp.zeros_like(w0)
    route_ref[...] = jnp.concatenate(
        [e0.astype(F32), e1.astype(F32), w0, w1, zeros, zeros, zeros, zeros], axis=0)
    srow = lax.broadcasted_iota(jnp.int32, (LANES, tm), 0)
    wts_t = jnp.where(srow == 0, w0, 0.0) + jnp.where(srow == 1, w1, 0.0)
    wts_ref[...] = wts_t.T


def _outproj(oa, gb, gu, conv_w, oc, x, mod, layer, mod_row, w_out, g2, rwt, rb, *, tm, t_seq):
    n = x.shape[0]
    row = lambda i: (i, 0)
    const2 = lambda i: (0, 0)
    hb = tm // HALO
    last = n // HALO - 1
    return pl.pallas_call(
        functools.partial(_outproj_kernel, tm=tm, t_seq=t_seq),
        grid=(n // tm,),
        in_specs=[
            pl.BlockSpec((tm, 4 * LANES), row),
            pl.BlockSpec((tm, B_WIDTH), row),
            pl.BlockSpec((tm, B_WIDTH), row),
            pl.BlockSpec((HALO, B_WIDTH), lambda i: (jnp.maximum(i * hb - 1, 0), 0)),
            pl.BlockSpec((HALO, B_WIDTH), lambda i: (jnp.minimum((i + 1) * hb, last), 0)),
            pl.BlockSpec((1, 3, B_WIDTH), lambda i: (layer, 0, 0)),
            pl.BlockSpec((tm, 2 * LANES), row),
            pl.BlockSpec((tm, D_MODEL), row),
            pl.BlockSpec((1, 1, N_MOD, D_MODEL), lambda i: (layer, mod_row(i), 0, 0)),
            pl.BlockSpec((D_MODEL, D_MODEL), const2),
            pl.BlockSpec((1, D_MODEL), const2),
            pl.BlockSpec((2 * N_EXPERTS, D_MODEL), const2),
            pl.BlockSpec((N_EXPERTS, 1), const2),
        ],
        out_specs=[
            pl.BlockSpec((tm, D_MODEL), row),
            pl.BlockSpec((tm * TOKEN_TILE_ROWS, LANES), row),
            pl.BlockSpec((8, tm), lambda i: (0, i)),
            pl.BlockSpec((tm, LANES), row),
        ],
        out_shape=[
            jax.ShapeDtypeStruct((n, D_MODEL), F32),
            jax.ShapeDtypeStruct((n * TOKEN_TILE_ROWS, LANES), F32),
            jax.ShapeDtypeStruct((8, n), F32),
            jax.ShapeDtypeStruct((n, LANES), F32),
        ],
        compiler_params=_params(),
        name="outproj",
    )(oa, gb, gu, gu, gu, conv_w, oc, x, mod, w_out, g2, rwt, rb)


TOKEN_TILE_ROWS = D_MODEL // LANES
EXPERT_TILE = 512
ZERO_CHUNK = 128
PLAN_BLOCK = 512


def _load_token_tiles(ref, n_tokens, lead=()):
    return jnp.concatenate(
        [ref[lead + (pl.ds(c, n_tokens, stride=TOKEN_TILE_ROWS), slice(None))]
         for c in range(TOKEN_TILE_ROWS)], axis=1)


def _store_token_tiles(ref, x):
    n_tokens = x.shape[0]
    for c in range(TOKEN_TILE_ROWS):
        ref[pl.ds(c, n_tokens, stride=TOKEN_TILE_ROWS), :] = x[:, LANES * c:LANES * (c + 1)]


def _moe_plan_kernel(route_ref, tri_ref, pos_ref, info_ref, *, n, n_tiles):
    eid = lax.broadcasted_iota(jnp.int32, (N_EXPERTS, PLAN_BLOCK), 0)

    def block(b):
        sl = slice(PLAN_BLOCK * b, PLAN_BLOCK * (b + 1))
        e0 = route_ref[0:1, sl].astype(jnp.int32)
        e1 = route_ref[1:2, sl].astype(jnp.int32)
        return e0, e1, jnp.where((eid == e0) | (eid == e1), 1.0, 0.0)

    counts = jnp.zeros((N_EXPERTS, 1), F32)
    for b in range(n // PLAN_BLOCK):
        counts = counts + jnp.sum(block(b)[2], axis=1, keepdims=True)
    ntile = jnp.floor((counts + (EXPERT_TILE - 1)) * (1.0 / EXPERT_TILE))
    sub = lax.broadcasted_iota(jnp.int32, (N_EXPERTS, LANES), 0)
    lane = lax.broadcasted_iota(jnp.int32, (N_EXPERTS, LANES), 1)
    ntile_row = jnp.sum(jnp.where(sub == lane, ntile, 0.0), axis=0, keepdims=True)
    first_tile = jnp.sum(jnp.where(lane < sub, ntile_row, 0.0), axis=1, keepdims=True)
    first_row = first_tile * EXPERT_TILE

    carry = jnp.zeros((N_EXPERTS, 1), F32)
    for b in range(n // PLAN_BLOCK):
        e0, e1, mb = block(b)
        rank = _dot(mb.astype(BF16), tri_ref[...]) + carry
        carry = carry + jnp.sum(mb, axis=1, keepdims=True)
        slot = first_row + rank
        p0 = jnp.sum(jnp.where(eid == e0, slot, 0.0), axis=0, keepdims=True)
        p1 = jnp.sum(jnp.where(eid == e1, slot, 0.0), axis=0, keepdims=True)
        pad = jnp.zeros((6, PLAN_BLOCK), jnp.int32)
        pos_ref[:, PLAN_BLOCK * b:PLAN_BLOCK * (b + 1)] = jnp.concatenate(
            [p0.astype(jnp.int32), p1.astype(jnp.int32), pad], axis=0)

    tile = lax.broadcasted_iota(jnp.int32, (1, LANES), 1).astype(F32)
    total = jnp.sum(ntile, axis=0, keepdims=True)
    expert_of_tile = jnp.sum(jnp.where(first_tile <= tile, 1.0, 0.0), axis=0, keepdims=True) - 1.0
    valid = jnp.where(tile < total, 1.0, 0.0)
    to_row = lambda col: jnp.sum(jnp.where(sub == lane, col, 0.0), axis=0, keepdims=True)
    fill_from = jnp.floor((first_row + counts) * (1.0 / ZERO_CHUNK))
    fill_to = (first_row + ntile * EXPERT_TILE) * (1.0 / ZERO_CHUNK)
    zrow = jnp.zeros((4, LANES), jnp.int32)
    info_ref[...] = jnp.concatenate(
        [expert_of_tile.astype(jnp.int32), valid.astype(jnp.int32),
         to_row(fill_from).astype(jnp.int32), to_row(fill_to).astype(jnp.int32), zrow], axis=0)


def _moe_plan(route, tri, *, n_tiles):
    n = route.shape[1]
    return pl.pallas_call(
        functools.partial(_moe_plan_kernel, n=n, n_tiles=n_tiles),
        grid=(1,),
        in_specs=[pl.BlockSpec((8, n), lambda i: (0, 0)),
                  pl.BlockSpec((PLAN_BLOCK, PLAN_BLOCK), lambda i: (0, 0))],
        out_specs=[pl.BlockSpec((8, n), lambda i: (0, 0)),
                   pl.BlockSpec((8, LANES), lambda i: (0, 0))],
        out_shape=[jax.ShapeDtypeStruct((8, n), jnp.int32),
                   jax.ShapeDtypeStruct((8, LANES), jnp.int32)],
        compiler_params=_params(),
        name="moe_plan",
    )(route, tri)


def _token_tile(ref, t):
    return ref.at[pl.ds(pl.multiple_of(t * TOKEN_TILE_ROWS, TOKEN_TILE_ROWS), TOKEN_TILE_ROWS)]


def _token_tiles(ref, first, count):
    rows = count * TOKEN_TILE_ROWS
    return ref.at[pl.ds(pl.multiple_of(first * TOKEN_TILE_ROWS, TOKEN_TILE_ROWS), rows)]


def _dispatch_kernel(fill_from_ref, fill_to_ref, valid_ref, pos_ref, h_ref, xs_hbm, zbuf, zsem, sem,
                     *, n_tiles):
    i = pl.program_id(0)

    @pl.when(i == 0)
    def _():
        zbuf[...] = jnp.zeros_like(zbuf)
        per_tile = EXPERT_TILE // ZERO_CHUNK

        def fill_copy(chunk):
            return pltpu.make_async_copy(zbuf, _token_tiles(xs_hbm, chunk * ZERO_CHUNK, ZERO_CHUNK), zsem)

        def each_chunk(do):
            for e in range(N_EXPERTS):
                for c in range(per_tile):
                    chunk = fill_from_ref[e] + c

                    @pl.when(chunk < fill_to_ref[e])
                    def _():
                        do(fill_copy(chunk))

            def unused_tile(j, carry):
                @pl.when(valid_ref[j] == 0)
                def _():
                    for c in range(per_tile):
                        do(fill_copy(j * per_tile + c))
                return carry

            lax.fori_loop(0, n_tiles, unused_tile, 0)

        each_chunk(lambda copy: copy.start())
        each_chunk(lambda copy: copy.wait())

    for r in range(PLAN_BLOCK):
        src = h_ref.at[TOKEN_TILE_ROWS * r:TOKEN_TILE_ROWS * (r + 1)]
        for s in range(2):
            pltpu.make_async_copy(src, _token_tile(xs_hbm, pos_ref[s, r]), sem).start(priority=s)
    for _ in range(2):
        pltpu.make_async_copy(h_ref, _token_tiles(xs_hbm, 0, PLAN_BLOCK), sem).wait()


def _dispatch(fill_from, fill_to, valid, pos, h2t, *, n_tiles):
    n = h2t.shape[0] // TOKEN_TILE_ROWS
    return pl.pallas_call(
        functools.partial(_dispatch_kernel, n_tiles=n_tiles),
        grid_spec=pltpu.PrefetchScalarGridSpec(
            num_scalar_prefetch=3,
            grid=(n // PLAN_BLOCK,),
            in_specs=[pl.BlockSpec((8, PLAN_BLOCK), lambda i, a, b, v: (0, i),
                                   memory_space=pltpu.SMEM),
                      pl.BlockSpec((PLAN_BLOCK * TOKEN_TILE_ROWS, LANES), lambda i, a, b, v: (i, 0))],
            out_specs=pl.BlockSpec(memory_space=pl.ANY),
            scratch_shapes=[pltpu.VMEM((ZERO_CHUNK * TOKEN_TILE_ROWS, LANES), F32),
                            pltpu.SemaphoreType.DMA(()), pltpu.SemaphoreType.DMA(())]),
        out_shape=jax.ShapeDtypeStruct((n_tiles * EXPERT_TILE * TOKEN_TILE_ROWS, LANES), F32),
        compiler_params=_params(),
        name="moe_dispatch",
    )(fill_from, fill_to, valid, pos, h2t)


def _experts_kernel(expert_ref, valid_ref, x_ref, w1_ref, w3_ref, w2_ref, o_ref):
    j = pl.program_id(0)

    @pl.when(valid_ref[j] == 1)
    def _():
        x = _load_token_tiles(x_ref, EXPERT_TILE).astype(BF16)
        a = _dot(x, w1_ref[0].astype(BF16))
        b = _dot(x, w3_ref[0].astype(BF16))
        hid = (a * _sigmoid(a)) * b
        _store_token_tiles(o_ref, _dot(hid.astype(BF16), w2_ref[0].astype(BF16)))

    @pl.when(valid_ref[j] == 0)
    def _():
        o_ref[...] = jnp.zeros_like(o_ref)


def _experts(expert_of_tile, valid, xs, w1, w3, w2, layer, *, n_tiles):
    rows = EXPERT_TILE * TOKEN_TILE_ROWS
    wmap = lambda j, e, v: (layer * N_EXPERTS + e[j], 0, 0)
    return pl.pallas_call(
        _experts_kernel,
        grid_spec=pltpu.PrefetchScalarGridSpec(
            num_scalar_prefetch=2,
            grid=(n_tiles,),
            in_specs=[pl.BlockSpec((rows, LANES), lambda j, e, v: (j * v[j], 0)),
                      pl.BlockSpec((1, D_MODEL, EXPERT_FF), wmap),
                      pl.BlockSpec((1, D_MODEL, EXPERT_FF), wmap),
                      pl.BlockSpec((1, EXPERT_FF, D_MODEL), wmap)],
            out_specs=pl.BlockSpec((rows, LANES), lambda j, e, v: (j, 0))),
        out_shape=jax.ShapeDtypeStruct((n_tiles * rows, LANES), F32),
        compiler_params=_params(),
        name="moe_experts",
    )(expert_of_tile, valid, xs, w1, w3, w2)


def _combine_kernel(pos_ref, next_pos_ref, ys_hbm, wts_ref, x1_ref, mod_ref, fg_ref, o_ref, buf, sem,
                    *, final):
    i = pl.program_id(0)
    last = pl.num_programs(0) - 1
    slot = i % 2

    def gather(p_ref, sl):
        for r in range(PLAN_BLOCK):
            for s in range(2):
                dst = buf.at[sl, s, TOKEN_TILE_ROWS * r:TOKEN_TILE_ROWS * (r + 1)]
                pltpu.make_async_copy(_token_tile(ys_hbm, p_ref[s, r]), dst, sem.at[sl]).start(
                    priority=s)

    def drain(sl):
        for s in range(2):
            pltpu.make_async_copy(_token_tiles(ys_hbm, 0, PLAN_BLOCK), buf.at[sl, s], sem.at[sl]).wait()

    @pl.when(i == 0)
    def _():
        gather(pos_ref, 0)

    gather(next_pos_ref, 1 - slot)
    drain(slot)
    y0 = _load_token_tiles(buf, PLAN_BLOCK, lead=(slot, 0))
    y1 = _load_token_tiles(buf, PLAN_BLOCK, lead=(slot, 1))
    wts = wts_ref[...]
    moe = wts[:, 0:1] * y0 + wts[:, 1:2] * y1
    x2 = x1_ref[...] + mod_ref[0, 0, 5:6, :] * moe
    if final:
        ms = jnp.mean(x2 * x2, axis=-1, keepdims=True)
        x2 = x2 * lax.rsqrt(ms + EPS) * fg_ref[...]
    o_ref[...] = x2

    @pl.when(i == last)
    def _():
        drain(1 - slot)


def _combine(pos, ys, wts, x1, mod, layer, mod_row, fg, *, final):
    n = x1.shape[0]
    nb = n // PLAN_BLOCK
    row = lambda i: (i, 0)
    return pl.pallas_call(
        functools.partial(_combine_kernel, final=final),
        grid=(nb,),
        in_specs=[
            pl.BlockSpec((8, PLAN_BLOCK), lambda i: (0, i), memory_space=pltpu.SMEM),
            pl.BlockSpec((8, PLAN_BLOCK), lambda i: (0, jnp.minimum(i + 1, nb - 1)),
                         memory_space=pltpu.SMEM),
            pl.BlockSpec(memory_space=pl.ANY),
            pl.BlockSpec((PLAN_BLOCK, LANES), row),
            pl.BlockSpec((PLAN_BLOCK, D_MODEL), row),
            pl.BlockSpec((1, 1, N_MOD, D_MODEL), lambda i: (layer, mod_row(i), 0, 0)),
            pl.BlockSpec((1, D_MODEL), lambda i: (0, 0)),
        ],
        out_specs=pl.BlockSpec((PLAN_BLOCK, D_MODEL), row),
        out_shape=jax.ShapeDtypeStruct((n, D_MODEL), F32),
        scratch_shapes=[pltpu.VMEM((2, 2, PLAN_BLOCK * TOKEN_TILE_ROWS, LANES), F32),
                        pltpu.SemaphoreType.DMA((2,))],
        compiler_params=pltpu.CompilerParams(vmem_limit_bytes=VMEM_LIMIT_BYTES,
                                             dimension_semantics=("arbitrary",)),
        name="moe_combine",
    )(pos, pos, ys, wts, x1, mod, fg)


def _moe(h2t, route, wts, x1, mod, layer, mod_row, w1, w3, w2, fg, tri, *, final):
    n = x1.shape[0]
    n_tiles = 2 * n // EXPERT_TILE + N_EXPERTS
    pos, info = _moe_plan(route, tri, n_tiles=n_tiles)
    expert_of_tile, valid = info[0, :n_tiles], info[1, :n_tiles]
    xs = _dispatch(info[2, :N_EXPERTS], info[3, :N_EXPERTS], valid, pos, h2t, n_tiles=n_tiles)
    ys = _experts(expert_of_tile, valid, xs, w1, w3, w2, layer, n_tiles=n_tiles)
    return _combine(pos, ys, wts, x1, mod, layer, mod_row, fg, final=final)


def _rope_tables(n_tokens, rot_dim):
    pos = np.arange(n_tokens)
    r = (pos // GRID_W).astype(np.float64)[:, None]
    col = (pos % GRID_W).astype(np.float64)[:, None]
    n_freq = rot_dim // 4
    inv_freq = ROPE_THETA ** (-np.arange(n_freq, dtype=np.float64) / n_freq)
    ang = np.concatenate([r * inv_freq, col * inv_freq], axis=-1)
    ang = np.repeat(ang, 2, axis=-1)
    sign = np.tile(np.array([-1.0, 1.0]), rot_dim // 2)
    cos = np.tile(np.cos(ang), (1, LANES // rot_dim))
    sin = np.tile(np.sin(ang) * sign, (1, LANES // rot_dim))
    return jnp.asarray(cos, F32), jnp.asarray(sin, F32)


def _block_diag_ones():
    idx = np.arange(2 * LANES) // 64
    return jnp.asarray((idx[:, None] == idx[None, :]).astype(np.float32), BF16)


def _pair_heads_by_kv_group(w, axis):
    shape = w.shape
    lead, tail = shape[:axis], shape[axis + 1:]
    w = w.reshape(lead + (A_KV_HEADS, A_HEADS // A_KV_HEADS, A_HEAD_DIM) + tail)
    w = jnp.swapaxes(w, axis, axis + 1)
    return w.reshape(shape)


def kernel(x_prompt, x_sample, cache_gqa_k, cache_gqa_v, cache_diff_k, cache_diff_v, c, c_ctx, w_in,
           w_out, norm1_g, norm2_g, ada_w, ada_b, gqa_qnorm_g, gqa_knorm_g, conv_w, diff_lambda,
           diff_subln_g, router_w, router_bias, moe_w1, moe_w3, moe_w2, final_g):
    n_ctx_b, t_ctx, _ = x_prompt.shape
    n_lat_b, t_lat, _ = x_sample.shape
    past = cache_gqa_k.shape[2]
    n_q = A_HEADS * A_HEAD_DIM

    cvec = jnp.concatenate([c_ctx[None, :], c, jnp.zeros((8 - 1 - n_lat_b, D_MODEL), F32)], axis=0)
    mod = _adaln(cvec, ada_w, ada_b).reshape(DEPTH, 8, N_MOD, D_MODEL)

    w_in_b = jnp.concatenate(
        [_pair_heads_by_kv_group(w_in[:, :, :n_q], 2), w_in[:, :, n_q:]], axis=2).astype(BF16)
    w_out_b = jnp.concatenate(
        [_pair_heads_by_kv_group(w_out[:, :n_q, :], 1), w_out[:, n_q:, :]], axis=1).astype(BF16)
    w1_b = moe_w1.reshape(DEPTH * N_EXPERTS, D_MODEL, EXPERT_FF)
    w3_b = moe_w3.reshape(DEPTH * N_EXPERTS, D_MODEL, EXPERT_FF)
    w2_b = moe_w2.reshape(DEPTH * N_EXPERTS, EXPERT_FF, D_MODEL)
    rw_t = router_w.T
    rw_hi = rw_t.astype(BF16)
    rwt = jnp.concatenate([rw_hi, (rw_t - rw_hi.astype(F32)).astype(BF16)], axis=0)
    rb = router_bias.reshape(N_EXPERTS, 1)
    bd = _block_diag_ones()
    tri = jnp.asarray(np.triu(np.ones((PLAN_BLOCK, PLAN_BLOCK), np.float32), k=1), BF16)
    fg = final_g.reshape(1, D_MODEL)
    tables = _rope_tables(t_lat, A_HEAD_DIM) + _rope_tables(t_lat, C_QK_DIM)

    ck_cache = cache_gqa_k.reshape(n_lat_b * DEPTH, past, A_KV_HEADS * A_HEAD_DIM)
    cv_cache = cache_gqa_v.reshape(n_lat_b * DEPTH, past, A_KV_HEADS * A_HEAD_DIM)
    dk_cache = cache_diff_k.reshape(n_lat_b * DEPTH, past, C_HEADS * 2 * C_QK_DIM)
    dv_cache = cache_diff_v.reshape(n_lat_b * DEPTH, past, C_HEADS * C_V_DIM)

    xp = x_prompt.reshape(n_ctx_b * t_ctx, D_MODEL)
    xs = x_sample.reshape(n_lat_b * t_lat, D_MODEL)
    tm = 1024
    tq = 256
    state = tuple(jnp.zeros((n_ctx_b, DEPTH, t_ctx, w), F32)
                  for w in (LANES, LANES, 2 * LANES, 2 * LANES))
    for l in range(DEPTH):
        lam_init = 0.8 - 0.6 * math.exp(-0.3 * l)
        g1 = norm1_g[l].reshape(1, D_MODEL)
        g2 = norm2_g[l].reshape(1, D_MODEL)
        qg = jnp.tile(gqa_qnorm_g[l], 4).reshape(1, 2 * LANES)
        kg = jnp.tile(gqa_knorm_g[l], 4).reshape(1, 2 * LANES)
        sg = jnp.tile(diff_subln_g[l], 2).reshape(1, LANES)
        final = l == DEPTH - 1

        def run_stream(x, n_batch, t_seq, mod_row_of, tabs, caches, nj_gqa, nj_diff, key_group,
                       state_in=None):
            ctx = caches is None
            outs = _inproj(x, mod, l, lambda i: mod_row_of(i, tm), g1, w_in_b[l], qg, kg, bd, tabs,
                           tm=tm, ctx=ctx, key_group=key_group, t_seq=t_seq, state_in=state_in)
            q, k, vt, gb, gu, cq, ck, cvt = outs[:8]
            c_gk, c_gv, c_dk, c_dv = caches if caches is not None else (None,) * 4
            oa = _gqa_attention(q, k, vt, c_gk, c_gv, l, n_batch=n_batch, t_seq=t_seq, tq=tq,
                                nj=nj_gqa)
            oc = _diff_attention(cq, ck, cvt, c_dk, c_dv, l, diff_lambda, sg, n_batch=n_batch,
                                 t_seq=t_seq, tq=tq, nj=nj_diff, lam_init=lam_init)
            x1, h2t, route, wts = _outproj(oa, gb, gu, conv_w, oc, x, mod, l,
                                           lambda i: mod_row_of(i, tm), w_out_b[l], g2, rwt, rb,
                                           tm=tm, t_seq=t_seq)
            x2 = _moe(h2t, route, wts, x1, mod, l, lambda i: mod_row_of(i, PLAN_BLOCK), w1_b, w3_b,
                      w2_b, fg, tri, final=final)
            return x2, outs[8:]

        xp, state = run_stream(xp, n_ctx_b, t_ctx, lambda i, t: 0, None, None, 4, 2, t_ctx, state)
        xs, _ = run_stream(xs, n_lat_b, t_lat, lambda i, t: 1 + (i * t) // t_lat, tables,
                           (ck_cache, cv_cache, dk_cache, dv_cache), 2, 1, 256)

    def heads(a, n_heads, dim):
        return a.reshape(n_ctx_b, DEPTH, t_ctx, n_heads, dim)

    return (xp.reshape(n_ctx_b, t_ctx, D_MODEL), xs.reshape(n_lat_b, t_lat, D_MODEL),
            heads(state[0], A_KV_HEADS, A_HEAD_DIM), heads(state[1], A_KV_HEADS, A_HEAD_DIM),
            heads(state[2], C_HEADS, 2 * C_QK_DIM), heads(state[3], C_HEADS, C_V_DIM))
```

```python
import functools
import math

import numpy as np
import jax
import jax.numpy as jnp
from jax import lax
from jax.experimental import pallas as pl
from jax.experimental.pallas import tpu as pltpu

D_MODEL = 1024
DEPTH = 2
GRID_W = 64
A_HEADS = 8
A_KV_HEADS = 2
A_HEAD_DIM = 64
B_WIDTH = 256
C_HEADS = 4
C_QK_DIM = 32
C_V_DIM = 64
N_EXPERTS = 16
EXPERTS_PER_GROUP = 4
N_EXPERT_GROUPS = 4
EXPERT_FF = 512
ROPE_THETA = 10000.0
EPS = 1e-6
N_MOD = 6
W_IN_WIDTH = 2304
LOG2E = math.log2(math.e)

LANES = 128
HEAD_ROWS = 64
ONES_ROWS = 16
VMEM_LIMIT_BYTES = 56 * 1024 * 1024

F32 = jnp.float32
BF16 = jnp.bfloat16

_dot = functools.partial(jnp.dot, preferred_element_type=F32)


def _dot_nt(a, b):
    return lax.dot_general(a, b, (((1,), (1,)), ((), ())), preferred_element_type=F32)


def _split(a):
    hi = a.astype(BF16)
    lo = (a - hi.astype(F32)).astype(BF16)
    return hi, lo


def _dot3(a, b):
    ah, al = _split(a)
    bh, bl = _split(b)
    return _dot(ah, bh) + _dot(ah, bl) + _dot(al, bh)


def _sigmoid(x):
    return 1.0 / (1.0 + jnp.exp(-x))


def _seg_rms(x, bd, g):
    ss = _dot((x * x).astype(BF16), bd)
    return x * lax.rsqrt(ss * (1.0 / 64.0) + EPS) * g


def _rope(x, cos, sin_signed):
    lane = lax.broadcasted_iota(jnp.int32, x.shape, 1)
    even = (lane & 1) == 0
    partner = jnp.where(even, pltpu.roll(x, LANES - 1, 1), pltpu.roll(x, 1, 1))
    return x * cos + partner * sin_signed


def _params():
    return pltpu.CompilerParams(vmem_limit_bytes=VMEM_LIMIT_BYTES)


def _adaln_kernel(c_ref, w_ref, b_ref, o_ref):
    c = c_ref[...]
    s = c * _sigmoid(c)
    o_ref[0] = _dot3(s, w_ref[0]) + b_ref[0]


def _adaln(cvec, ada_w, ada_b):
    tn = 1536
    nt = (N_MOD * D_MODEL) // tn
    return pl.pallas_call(
        _adaln_kernel,
        grid=(DEPTH, nt),
        in_specs=[
            pl.BlockSpec((8, D_MODEL), lambda l, n: (0, 0)),
            pl.BlockSpec((1, D_MODEL, tn), lambda l, n: (l, 0, n)),
            pl.BlockSpec((1, 1, tn), lambda l, n: (l, 0, n)),
        ],
        out_specs=pl.BlockSpec((1, 8, tn), lambda l, n: (l, 0, n)),
        out_shape=jax.ShapeDtypeStruct((DEPTH, 8, N_MOD * D_MODEL), F32),
        compiler_params=_params(),
        name="adaln",
    )(cvec, ada_w, ada_b.reshape(DEPTH, 1, N_MOD * D_MODEL))


def _inproj_kernel(*refs, rope, ctx, kg, n_state_in):
    it = iter(refs)
    x_ref, mod_ref, g1_ref, w_ref, qg_ref, kg_ref, bd_ref = (next(it) for _ in range(7))
    if rope:
        cos_a, sin_a, cos_c, sin_c = (next(it) for _ in range(4))
    for _ in range(n_state_in):
        next(it)
    q_ref, k_ref, vt_ref, gb_ref, gu_ref, cq_ref, ck_ref, cvt_ref = (next(it) for _ in range(8))
    if ctx:
        k32_ref, v32_ref, ck32_ref, cv32_ref = (next(it) for _ in range(4))

    def store_state(ref, a):
        ref[...] = a.reshape(ref.shape)

    x = x_ref[...]
    ms = jnp.mean(x * x, axis=-1, keepdims=True)
    y = x * lax.rsqrt(ms + EPS) * g1_ref[...]
    h = y * (1.0 + mod_ref[0, 0, 1:2, :]) + mod_ref[0, 0, 0:1, :]
    hb = h.astype(BF16)
    bd = bd_ref[...]

    def seg(lo, width):
        return _dot(hb, w_ref[:, lo:lo + width])

    def store_transposed(dst, a):
        at = a.T.astype(BF16)
        for c in range(a.shape[0] // kg):
            dst[c] = at[:, kg * c:kg * (c + 1)]

    halves = lambda a: (a[:, :LANES], a[:, LANES:])

    for jj in range(2):
        qq = _seg_rms(seg(2 * LANES * jj, 2 * LANES), bd, qg_ref[...])
        for j, q in zip((2 * jj, 2 * jj + 1), halves(qq)):
            if rope:
                q = _rope(q, cos_a[...], sin_a[...])
            q_ref[j] = (q * (A_HEAD_DIM ** -0.5 * LOG2E)).astype(BF16)

    kv = seg(512, 2 * LANES)
    k, v = halves(_seg_rms(kv, bd, kg_ref[...]))[0], kv[:, LANES:]
    if rope:
        k = _rope(k, cos_a[...], sin_a[...])
    if ctx:
        store_state(k32_ref, k)
        store_state(v32_ref, v)
    k_ref[0] = k.astype(BF16)
    store_transposed(vt_ref.at[0], v)

    gb_ref[...] = seg(768, B_WIDTH).astype(BF16)
    gu_ref[...] = (seg(1024, B_WIDTH) * seg(1280, B_WIDTH)).astype(BF16)

    cq2, ck2, cv2 = seg(1536, 2 * LANES), seg(1792, 2 * LANES), seg(2048, 2 * LANES)
    if ctx:
        store_state(ck32_ref, ck2)
        store_state(cv32_ref, cv2)
    for j, (cq, ck, cv) in enumerate(zip(halves(cq2), halves(ck2), halves(cv2))):
        if rope:
            cq = _rope(cq, cos_c[...], sin_c[...])
            ck = _rope(ck, cos_c[...], sin_c[...])
        cq_ref[j] = (cq * (C_QK_DIM ** -0.5 * LOG2E)).astype(BF16)
        ck_ref[j] = ck.astype(BF16)
        store_transposed(cvt_ref.at[j], cv)


def _inproj(x, mod, layer, mod_row, g1, w_in, qg, kg, bd, tables, *, tm, ctx, key_group, t_seq,
            state_in):
    n = x.shape[0]
    rope = tables is not None
    row = lambda i: (i, 0)
    const2 = lambda i: (0, 0)
    pair3 = lambda i: (0, i, 0)
    tile4 = lambda i: (0, i, 0, 0)
    nc = tm // key_group
    in_specs = [
        pl.BlockSpec((tm, D_MODEL), row),
        pl.BlockSpec((1, 1, N_MOD, D_MODEL), lambda i: (layer, mod_row(i), 0, 0)),
        pl.BlockSpec((1, D_MODEL), const2),
        pl.BlockSpec((D_MODEL, W_IN_WIDTH), const2),
        pl.BlockSpec((1, 2 * LANES), const2),
        pl.BlockSpec((1, 2 * LANES), const2),
        pl.BlockSpec((2 * LANES, 2 * LANES), const2),
    ]
    args = [x, mod, g1, w_in, qg, kg, bd]
    if rope:
        t = tables[0].shape[0]
        tab = lambda i: (i % (t // tm), 0)
        in_specs += [pl.BlockSpec((tm, LANES), tab)] * 4
        args += list(tables)
    out_specs = [
        pl.BlockSpec((4, tm, LANES), pair3),
        pl.BlockSpec((1, tm, LANES), pair3),
        pl.BlockSpec((1, nc, LANES, key_group), tile4),
        pl.BlockSpec((tm, B_WIDTH), row),
        pl.BlockSpec((tm, B_WIDTH), row),
        pl.BlockSpec((2, tm, LANES), pair3),
        pl.BlockSpec((2, tm, LANES), pair3),
        pl.BlockSpec((2, nc, LANES, key_group), tile4),
    ]
    out_shape = [
        jax.ShapeDtypeStruct((4, n, LANES), BF16),
        jax.ShapeDtypeStruct((1, n, LANES), BF16),
        jax.ShapeDtypeStruct((1, n // key_group, LANES, key_group), BF16),
        jax.ShapeDtypeStruct((n, B_WIDTH), BF16),
        jax.ShapeDtypeStruct((n, B_WIDTH), BF16),
        jax.ShapeDtypeStruct((2, n, LANES), BF16),
        jax.ShapeDtypeStruct((2, n, LANES), BF16),
        jax.ShapeDtypeStruct((2, n // key_group, LANES, key_group), BF16),
    ]
    aliases = {}
    n_state_in = 0
    if ctx:
        seqs = tm // t_seq
        for width in (LANES, LANES, 2 * LANES, 2 * LANES):
            out_specs.append(pl.BlockSpec((seqs, 1, t_seq, width), lambda i: (i, layer, 0, 0)))
            out_shape.append(jax.ShapeDtypeStruct((n // t_seq, DEPTH, t_seq, width), F32))
        if state_in is not None:
            n_state_in = len(state_in)
            for s, a in enumerate(state_in):
                aliases[len(args)] = len(out_shape) - n_state_in + s
                in_specs.append(pl.BlockSpec(memory_space=pl.ANY))
                args.append(a)
    return pl.pallas_call(
        functools.partial(_inproj_kernel, rope=rope, ctx=ctx, kg=key_group, n_state_in=n_state_in),
        grid=(n // tm,),
        in_specs=in_specs,
        out_specs=out_specs,
        out_shape=out_shape,
        input_output_aliases=aliases,
        compiler_params=_params(),
        name="inproj_ctx" if ctx else "inproj_lat",
    )(*args)


def _attend(units, k_ref, vt_ref, caches, n_sub):
    tq = units[0][0].shape[0]
    nu = len(units)
    kg = vt_ref.shape[-1]
    n_groups = k_ref.shape[1] // n_sub // kg
    blocks = sorted({kb for _, _, kb in units})
    n_stages = n_groups + (0 if caches is None else 1)

    def load(stage):
        if caches is not None:
            if stage == 0:
                return {kb: (caches[kb[0]][0].astype(BF16), caches[kb[0]][1].T.astype(BF16))
                        for kb in blocks}
            stage -= 1
        out = {}
        for j, sub in blocks:
            g = sub * n_groups + stage
            out[(j, sub)] = (k_ref[j, kg * g:kg * (g + 1), :], vt_ref[j, g])
        return out

    def value_rows(vt, half):
        own = vt[HEAD_ROWS * half:HEAD_ROWS * (half + 1), :]
        return jnp.concatenate([own, jnp.ones((ONES_ROWS, vt.shape[1]), BF16)], axis=0)

    def score(kv):
        return [_dot_nt(kv[kb][0], q) for q, _, kb in units]

    ms = [jnp.full((1, tq), -jnp.inf, F32)] * nu
    accs = [jnp.zeros((HEAD_ROWS + ONES_ROWS, tq), F32)] * nu
    kv = load(0)
    scores = score(kv)
    for s in range(n_stages):
        cur_scores, cur_kv = scores, kv
        if s + 1 < n_stages:
            kv = load(s + 1)
            scores = score(kv)
        new_ms, ps = [], []
        for u in range(nu):
            half = cur_scores[u].shape[0] // 2
            s1, s2 = cur_scores[u][:half], cur_scores[u][half:]
            m1 = jnp.maximum(ms[u], jnp.max(s1, axis=0, keepdims=True))
            p1 = jnp.exp2(s1 - m1).astype(BF16)
            m = jnp.maximum(m1, jnp.max(s2, axis=0, keepdims=True))
            p2 = jnp.exp2(s2 - m).astype(BF16)
            new_ms.append(m)
            ps.append(jnp.concatenate([p1 * jnp.exp2(m1 - m).astype(BF16), p2], axis=0))
        pvs = [_dot(value_rows(cur_kv[kb][1], half), ps[u]) for u, (_, half, kb) in enumerate(units)]
        accs = [jnp.exp2(ms[u] - new_ms[u]) * accs[u] + pvs[u] for u in range(nu)]
        ms = new_ms
    return [acc[0:HEAD_ROWS] / acc[HEAD_ROWS:HEAD_ROWS + 1] for acc in accs]


def _lane_mask(q, lo, width):
    lane = lax.broadcasted_iota(jnp.int32, q.shape, 1)
    return jnp.where((lane >= lo) & (lane < lo + width), q, jnp.zeros_like(q))


def _query_rows(seqs, tq):
    return [slice(tq * s, tq * (s + 1)) for s in range(seqs)]


def _gqa_kernel(*refs, has_cache, nj, seqs, tq):
    if has_cache:
        q_ref, k_ref, vt_ref, ck_ref, cv_ref, o_ref = refs
        caches = {0: (ck_ref[0], cv_ref[0])}
    else:
        q_ref, k_ref, vt_ref, o_ref = refs
        caches = None
    rows = _query_rows(seqs, tq)
    units = [(_lane_mask(q_ref[j, rows[s], :], 64 * u, 64), u, (0, s))
             for s in range(seqs) for j in range(nj) for u in range(2)]
    outs = iter(_attend(units, k_ref, vt_ref, caches, seqs))
    for s in range(seqs):
        for j in range(nj):
            ot = jnp.concatenate([next(outs), next(outs)], axis=0)
            o_ref[rows[s], LANES * j:LANES * (j + 1)] = ot.T.astype(BF16)


def _gqa_attention(q, k, vt, cache_k, cache_v, layer, *, n_batch, t_seq, tq, nj, seqs):
    n = q.shape[1]
    nq = t_seq // tq
    kg = vt.shape[-1]
    has_cache = cache_k is not None
    assert seqs == 1 or (nq == 1 and not has_cache)
    in_specs = [
        pl.BlockSpec((nj, seqs * tq, LANES), lambda b, i, j: (j, b * nq + i, 0)),
        pl.BlockSpec((1, seqs * t_seq, LANES), lambda b, i, j: (0, b, 0)),
        pl.BlockSpec((1, seqs * t_seq // kg, LANES, kg), lambda b, i, j: (0, b, 0, 0)),
    ]
    args = [q, k, vt]
    if has_cache:
        past = cache_k.shape[1]
        cspec = pl.BlockSpec((1, past, LANES), lambda b, i, j: (b * DEPTH + layer, 0, 0))
        in_specs += [cspec, cspec]
        args += [cache_k, cache_v]
    return pl.pallas_call(
        functools.partial(_gqa_kernel, has_cache=has_cache, nj=nj, seqs=seqs, tq=tq),
        grid=(n_batch // seqs, nq, 4 // nj),
        in_specs=in_specs,
        out_specs=pl.BlockSpec((seqs * tq, LANES * nj), lambda b, i, j: (b * nq + i, j)),
        out_shape=jax.ShapeDtypeStruct((n, 4 * LANES), BF16),
        compiler_params=_params(),
        name="gqa_lat" if has_cache else "gqa_ctx",
    )(*args)


def _diff_kernel(*refs, has_cache, nj, seqs, tq, lam_init):
    if has_cache:
        q_ref, k_ref, vt_ref, ck_ref, cv_ref, lam_ref, g_ref, o_ref = refs
    else:
        q_ref, k_ref, vt_ref, lam_ref, g_ref, o_ref = refs
    lv = lam_ref[0]
    lam = (jnp.exp(jnp.sum(lv[0:1] * lv[1:2], axis=-1, keepdims=True))
           - jnp.exp(jnp.sum(lv[2:3] * lv[3:4], axis=-1, keepdims=True)) + lam_init)
    rows = _query_rows(seqs, tq)
    units = [(_lane_mask(q_ref[j, rows[s], :], 32 * c, 32), c // 2, (j, s))
             for s in range(seqs) for j in range(nj) for c in range(4)]
    caches = None
    if has_cache:
        caches = {j: (ck_ref[0, :, LANES * j:LANES * (j + 1)], cv_ref[0, :, LANES * j:LANES * (j + 1)])
                  for j in range(nj)}
    outs = iter(_attend(units, k_ref, vt_ref, caches, seqs))
    for s in range(seqs):
        for j in range(nj):
            res = [next(outs) for _ in range(4)]
            heads = []
            for i in range(2):
                o = res[2 * i] - lam * res[2 * i + 1]
                ms = jnp.mean(o * o, axis=0, keepdims=True)
                heads.append(o * lax.rsqrt(ms + EPS))
            o = jnp.concatenate(heads, axis=0).T * g_ref[...] * (1.0 - lam_init)
            o_ref[rows[s], LANES * j:LANES * (j + 1)] = o.astype(BF16)


def _diff_attention(cq, ck, cvt, cache_k, cache_v, layer, diff_lambda, subln_g, *,
                    n_batch, t_seq, tq, nj, seqs, lam_init):
    n = cq.shape[1]
    nq = t_seq // tq
    kg = cvt.shape[-1]
    has_cache = cache_k is not None
    assert seqs == 1 or (nq == 1 and not has_cache)
    in_specs = [
        pl.BlockSpec((nj, seqs * tq, LANES), lambda b, i, j: (j, b * nq + i, 0)),
        pl.BlockSpec((nj, seqs * t_seq, LANES), lambda b, i, j: (j, b, 0)),
        pl.BlockSpec((nj, seqs * t_seq // kg, LANES, kg), lambda b, i, j: (j, b, 0, 0)),
    ]
    args = [cq, ck, cvt]
    if has_cache:
        past = cache_k.shape[1]
        cspec = pl.BlockSpec((1, past, LANES * nj), lambda b, i, j: (b * DEPTH + layer, 0, j))
        in_specs += [cspec, cspec]
        args += [cache_k, cache_v]
    in_specs += [
        pl.BlockSpec((1, 4, C_QK_DIM), lambda b, i, j: (layer, 0, 0)),
        pl.BlockSpec((1, LANES), lambda b, i, j: (0, 0)),
    ]
    args += [diff_lambda, subln_g]
    return pl.pallas_call(
        functools.partial(_diff_kernel, has_cache=has_cache, nj=nj, seqs=seqs, tq=tq,
                          lam_init=lam_init),
        grid=(n_batch // seqs, nq, 2 // nj),
        in_specs=in_specs,
        out_specs=pl.BlockSpec((seqs * tq, LANES * nj), lambda b, i, j: (b * nq + i, j)),
        out_shape=jax.ShapeDtypeStruct((n, 2 * LANES), BF16),
        compiler_params=_params(),
        name="diff_lat" if has_cache else "diff_ctx",
    )(*args)


HALO = 16


def _route(logits_t, bias_col):
    scores = _sigmoid(logits_t)
    sel = scores + bias_col
    row = lambda a, e: a[e:e + 1, :]
    grp = []
    for g in range(N_EXPERT_GROUPS):
        s0, s1, s2, s3 = (row(sel, 4 * g + k) for k in range(4))
        hi1, lo1 = jnp.maximum(s0, s1), jnp.minimum(s0, s1)
        hi2, lo2 = jnp.maximum(s2, s3), jnp.minimum(s2, s3)
        grp.append(jnp.maximum(hi1, hi2) + jnp.maximum(jnp.minimum(hi1, hi2), jnp.maximum(lo1, lo2)))
    best = grp[0]
    tg = jnp.zeros_like(best, dtype=jnp.int32)
    for g in range(1, N_EXPERT_GROUPS):
        better = grp[g] > best
        best = jnp.where(better, grp[g], best)
        tg = jnp.where(better, g, tg)

    def in_group(a, k):
        out = row(a, k)
        for g in range(1, N_EXPERT_GROUPS):
            out = jnp.where(tg == g, row(a, 4 * g + k), out)
        return out

    vals = [in_group(sel, k) for k in range(EXPERTS_PER_GROUP)]
    scs = [in_group(scores, k) for k in range(EXPERTS_PER_GROUP)]

    def first_argmax(vs):
        bv, bi = vs[0], jnp.zeros_like(tg)
        for k in range(1, len(vs)):
            better = vs[k] > bv
            bv = jnp.where(better, vs[k], bv)
            bi = jnp.where(better, k, bi)
        return bi

    i1 = first_argmax(vals)
    i2 = first_argmax([jnp.where(i1 == k, -jnp.inf, vals[k]) for k in range(EXPERTS_PER_GROUP)])

    def pick(vs, idx):
        out = vs[0]
        for k in range(1, len(vs)):
            out = jnp.where(idx == k, vs[k], out)
        return out

    w1, w2 = pick(scs, i1), pick(scs, i2)
    tot = w1 + w2
    return 4 * tg + i1, 4 * tg + i2, w1 / tot, w2 / tot


def _outproj_kernel(oa_ref, gb_ref, gu_ref, gup_ref, gun_ref, cw_ref, oc_ref, x_ref, mod_ref,
                    w_ref, g2_ref, rwt_ref, rb_ref,
                    x1_ref, h2t_ref, route_ref, wts_ref, *, tm, t_seq):
    i = pl.program_id(0)
    gu = gu_ref[...].astype(F32)
    rowi = lax.broadcasted_iota(jnp.int32, gu.shape, 0)
    pos = (i * tm + rowi) % t_seq
    prev = pltpu.roll(gu, 1, 0)
    prev = jnp.where(rowi == 0, gup_ref[HALO - 1:HALO, :].astype(F32), prev)
    prev = jnp.where(pos == 0, 0.0, prev)
    nxt = pltpu.roll(gu, tm - 1, 0)
    nxt = jnp.where(rowi == tm - 1, gun_ref[0:1, :].astype(F32), nxt)
    nxt = jnp.where(pos == t_seq - 1, 0.0, nxt)
    cw = cw_ref[0]
    ob = gb_ref[...].astype(F32) * (prev * cw[0:1] + gu * cw[1:2] + nxt * cw[2:3])

    mix = (_dot(oa_ref[...], w_ref[0:512, :])
           + _dot(ob.astype(BF16), w_ref[512:768, :])
           + _dot(oc_ref[...], w_ref[768:1024, :]))
    x1 = x_ref[...] + mod_ref[0, 0, 2:3, :] * mix
    x1_ref[...] = x1
    ms = jnp.mean(x1 * x1, axis=-1, keepdims=True)
    h2 = x1 * lax.rsqrt(ms + EPS) * g2_ref[...]
    h2 = h2 * (1.0 + mod_ref[0, 0, 4:5, :]) + mod_ref[0, 0, 3:4, :]
    _store_token_tiles(h2t_ref, h2)

    hh, hl = _split(h2)
    rwt = rwt_ref[...]
    both = _dot_nt(rwt, hh)
    logits_t = both[0:N_EXPERTS] + both[N_EXPERTS:2 * N_EXPERTS] + _dot_nt(rwt[0:N_EXPERTS], hl)
    e0, e1, w0, w1 = _route(logits_t, rb_ref[...])
    zeros = jnp.zeros_like(w0)
    route_ref[...] = jnp.concatenate(
        [e0.astype(F32), e1.astype(F32), w0, w1, zeros, zeros, zeros, zeros], axis=0)
    srow = lax.broadcasted_iota(jnp.int32, (LANES, tm), 0)
    wts_t = jnp.where(srow == 0, w0, 0.0) + jnp.where(srow == 1, w1, 0.0)
    wts_ref[...] = wts_t.T


def _outproj(oa, gb, gu, conv_w, oc, x, mod, layer, mod_row, w_out, g2, rwt, rb, *, tm, t_seq):
    n = x.shape[0]
    row = lambda i: (i, 0)
    const2 = lambda i: (0, 0)
    hb = tm // HALO
    last = n // HALO - 1
    return pl.pallas_call(
        functools.partial(_outproj_kernel, tm=tm, t_seq=t_seq),
        grid=(n // tm,),
        in_specs=[
            pl.BlockSpec((tm, 4 * LANES), row),
            pl.BlockSpec((tm, B_WIDTH), row),
            pl.BlockSpec((tm, B_WIDTH), row),
            pl.BlockSpec((HALO, B_WIDTH), lambda i: (jnp.maximum(i * hb - 1, 0), 0)),
            pl.BlockSpec((HALO, B_WIDTH), lambda i: (jnp.minimum((i + 1) * hb, last), 0)),
            pl.BlockSpec((1, 3, B_WIDTH), lambda i: (layer, 0, 0)),
            pl.BlockSpec((tm, 2 * LANES), row),
            pl.BlockSpec((tm, D_MODEL), row),
            pl.BlockSpec((1, 1, N_MOD, D_MODEL), lambda i: (layer, mod_row(i), 0, 0)),
            pl.BlockSpec((D_MODEL, D_MODEL), const2),
            pl.BlockSpec((1, D_MODEL), const2),
            pl.BlockSpec((2 * N_EXPERTS, D_MODEL), const2),
            pl.BlockSpec((N_EXPERTS, 1), const2),
        ],
        out_specs=[
            pl.BlockSpec((tm, D_MODEL), row),
            pl.BlockSpec((tm * TOKEN_TILE_ROWS, LANES), row),
            pl.BlockSpec((8, tm), lambda i: (0, i)),
            pl.BlockSpec((tm, LANES), row),
        ],
        out_shape=[
            jax.ShapeDtypeStruct((n, D_MODEL), F32),
            jax.ShapeDtypeStruct((n * TOKEN_TILE_ROWS, LANES), F32),
            jax.ShapeDtypeStruct((8, n), F32),
            jax.ShapeDtypeStruct((n, LANES), F32),
        ],
        compiler_params=_params(),
        name="outproj",
    )(oa, gb, gu, gu, gu, conv_w, oc, x, mod, w_out, g2, rwt, rb)


TOKEN_TILE_ROWS = D_MODEL // LANES
EXPERT_TILE = 512
ZERO_CHUNK = 128
PLAN_BLOCK = 512


def _load_token_tiles(ref, n_tokens, lead=()):
    return jnp.concatenate(
        [ref[lead + (pl.ds(c, n_tokens, stride=TOKEN_TILE_ROWS), slice(None))]
         for c in range(TOKEN_TILE_ROWS)], axis=1)


def _store_token_tiles(ref, x):
    n_tokens = x.shape[0]
    for c in range(TOKEN_TILE_ROWS):
        ref[pl.ds(c, n_tokens, stride=TOKEN_TILE_ROWS), :] = x[:, LANES * c:LANES * (c + 1)]


def _moe_plan_kernel(route_ref, tri_ref, pos_ref, info_ref, *, n, n_tiles):
    eid = lax.broadcasted_iota(jnp.int32, (N_EXPERTS, PLAN_BLOCK), 0)

    def block(b):
        sl = slice(PLAN_BLOCK * b, PLAN_BLOCK * (b + 1))
        e0 = route_ref[0:1, sl].astype(jnp.int32)
        e1 = route_ref[1:2, sl].astype(jnp.int32)
        return e0, e1, jnp.where((eid == e0) | (eid == e1), 1.0, 0.0)

    counts = jnp.zeros((N_EXPERTS, 1), F32)
    for b in range(n // PLAN_BLOCK):
        counts = counts + jnp.sum(block(b)[2], axis=1, keepdims=True)
    ntile = jnp.floor((counts + (EXPERT_TILE - 1)) * (1.0 / EXPERT_TILE))
    sub = lax.broadcasted_iota(jnp.int32, (N_EXPERTS, LANES), 0)
    lane = lax.broadcasted_iota(jnp.int32, (N_EXPERTS, LANES), 1)
    ntile_row = jnp.sum(jnp.where(sub == lane, ntile, 0.0), axis=0, keepdims=True)
    first_tile = jnp.sum(jnp.where(lane < sub, ntile_row, 0.0), axis=1, keepdims=True)
    first_row = first_tile * EXPERT_TILE

    carry = jnp.zeros((N_EXPERTS, 1), F32)
    for b in range(n // PLAN_BLOCK):
        e0, e1, mb = block(b)
        rank = _dot(mb.astype(BF16), tri_ref[...]) + carry
        carry = carry + jnp.sum(mb, axis=1, keepdims=True)
        slot = first_row + rank
        p0 = jnp.sum(jnp.where(eid == e0, slot, 0.0), axis=0, keepdims=True)
        p1 = jnp.sum(jnp.where(eid == e1, slot, 0.0), axis=0, keepdims=True)
        pad = jnp.zeros((6, PLAN_BLOCK), jnp.int32)
        pos_ref[:, PLAN_BLOCK * b:PLAN_BLOCK * (b + 1)] = jnp.concatenate(
            [p0.astype(jnp.int32), p1.astype(jnp.int32), pad], axis=0)

    tile = lax.broadcasted_iota(jnp.int32, (1, LANES), 1).astype(F32)
    total = jnp.sum(ntile, axis=0, keepdims=True)
    expert_of_tile = jnp.sum(jnp.where(first_tile <= tile, 1.0, 0.0), axis=0, keepdims=True) - 1.0
    valid = jnp.where(tile < total, 1.0, 0.0)
    to_row = lambda col: jnp.sum(jnp.where(sub == lane, col, 0.0), axis=0, keepdims=True)
    fill_from = jnp.floor((first_row + counts) * (1.0 / ZERO_CHUNK))
    fill_to = (first_row + ntile * EXPERT_TILE) * (1.0 / ZERO_CHUNK)
    zrow = jnp.zeros((4, LANES), jnp.int32)
    info_ref[...] = jnp.concatenate(
        [expert_of_tile.astype(jnp.int32), valid.astype(jnp.int32),
         to_row(fill_from).astype(jnp.int32), to_row(fill_to).astype(jnp.int32), zrow], axis=0)


def _moe_plan(route, tri, *, n_tiles):
    n = route.shape[1]
    return pl.pallas_call(
        functools.partial(_moe_plan_kernel, n=n, n_tiles=n_tiles),
        grid=(1,),
        in_specs=[pl.BlockSpec((8, n), lambda i: (0, 0)),
                  pl.BlockSpec((PLAN_BLOCK, PLAN_BLOCK), lambda i: (0, 0))],
        out_specs=[pl.BlockSpec((8, n), lambda i: (0, 0)),
                   pl.BlockSpec((8, LANES), lambda i: (0, 0))],
        out_shape=[jax.ShapeDtypeStruct((8, n), jnp.int32),
                   jax.ShapeDtypeStruct((8, LANES), jnp.int32)],
        compiler_params=_params(),
        name="moe_plan",
    )(route, tri)


def _token_tile(ref, t):
    return ref.at[pl.ds(pl.multiple_of(t * TOKEN_TILE_ROWS, TOKEN_TILE_ROWS), TOKEN_TILE_ROWS)]


def _token_tiles(ref, first, count):
    rows = count * TOKEN_TILE_ROWS
    return ref.at[pl.ds(pl.multiple_of(first * TOKEN_TILE_ROWS, TOKEN_TILE_ROWS), rows)]


def _dispatch_kernel(fill_from_ref, fill_to_ref, valid_ref, pos_ref, h_ref, xs_hbm, zbuf, zsem, sem,
                     *, n_tiles):
    i = pl.program_id(0)

    @pl.when(i == 0)
    def _():
        zbuf[...] = jnp.zeros_like(zbuf)
        per_tile = EXPERT_TILE // ZERO_CHUNK

        def fill_copy(chunk):
            return pltpu.make_async_copy(zbuf, _token_tiles(xs_hbm, chunk * ZERO_CHUNK, ZERO_CHUNK), zsem)

        def each_chunk(do):
            for e in range(N_EXPERTS):
                for c in range(per_tile):
                    chunk = fill_from_ref[e] + c

                    @pl.when(chunk < fill_to_ref[e])
                    def _():
                        do(fill_copy(chunk))

            def unused_tile(j, carry):
                @pl.when(valid_ref[j] == 0)
                def _():
                    for c in range(per_tile):
                        do(fill_copy(j * per_tile + c))
                return carry

            lax.fori_loop(0, n_tiles, unused_tile, 0)

        each_chunk(lambda copy: copy.start())
        each_chunk(lambda copy: copy.wait())

    for r in range(PLAN_BLOCK):
        src = h_ref.at[TOKEN_TILE_ROWS * r:TOKEN_TILE_ROWS * (r + 1)]
        for s in range(2):
            pltpu.make_async_copy(src, _token_tile(xs_hbm, pos_ref[s, r]), sem).start(priority=s)
    for _ in range(2):
        pltpu.make_async_copy(h_ref, _token_tiles(xs_hbm, 0, PLAN_BLOCK), sem).wait()


def _dispatch(fill_from, fill_to, valid, pos, h2t, *, n_tiles):
    n = h2t.shape[0] // TOKEN_TILE_ROWS
    return pl.pallas_call(
        functools.partial(_dispatch_kernel, n_tiles=n_tiles),
        grid_spec=pltpu.PrefetchScalarGridSpec(
            num_scalar_prefetch=3,
            grid=(n // PLAN_BLOCK,),
            in_specs=[pl.BlockSpec((8, PLAN_BLOCK), lambda i, a, b, v: (0, i),
                                   memory_space=pltpu.SMEM),
                      pl.BlockSpec((PLAN_BLOCK * TOKEN_TILE_ROWS, LANES), lambda i, a, b, v: (i, 0))],
            out_specs=pl.BlockSpec(memory_space=pl.ANY),
            scratch_shapes=[pltpu.VMEM((ZERO_CHUNK * TOKEN_TILE_ROWS, LANES), F32),
                            pltpu.SemaphoreType.DMA(()), pltpu.SemaphoreType.DMA(())]),
        out_shape=jax.ShapeDtypeStruct((n_tiles * EXPERT_TILE * TOKEN_TILE_ROWS, LANES), F32),
        compiler_params=_params(),
        name="moe_dispatch",
    )(fill_from, fill_to, valid, pos, h2t)


def _experts_kernel(expert_ref, valid_ref, x_ref, w1_ref, w3_ref, w2_ref, o_ref):
    j = pl.program_id(0)

    @pl.when(valid_ref[j] == 1)
    def _():
        x = _load_token_tiles(x_ref, EXPERT_TILE).astype(BF16)
        a = _dot(x, w1_ref[0].astype(BF16))
        b = _dot(x, w3_ref[0].astype(BF16))
        hid = (a * _sigmoid(a)) * b
        _store_token_tiles(o_ref, _dot(hid.astype(BF16), w2_ref[0].astype(BF16)))

    @pl.when(valid_ref[j] == 0)
    def _():
        o_ref[...] = jnp.zeros_like(o_ref)


def _experts(expert_of_tile, valid, xs, w1, w3, w2, layer, *, n_tiles):
    rows = EXPERT_TILE * TOKEN_TILE_ROWS
    wmap = lambda j, e, v: (layer * N_EXPERTS + e[j], 0, 0)
    return pl.pallas_call(
        _experts_kernel,
        grid_spec=pltpu.PrefetchScalarGridSpec(
            num_scalar_prefetch=2,
            grid=(n_tiles,),
            in_specs=[pl.BlockSpec((rows, LANES), lambda j, e, v: (j * v[j], 0)),
                      pl.BlockSpec((1, D_MODEL, EXPERT_FF), wmap),
                      pl.BlockSpec((1, D_MODEL, EXPERT_FF), wmap),
                      pl.BlockSpec((1, EXPERT_FF, D_MODEL), wmap)],
            out_specs=pl.BlockSpec((rows, LANES), lambda j, e, v: (j, 0))),
        out_shape=jax.ShapeDtypeStruct((n_tiles * rows, LANES), F32),
        compiler_params=_params(),
        name="moe_experts",
    )(expert_of_tile, valid, xs, w1, w3, w2)


def _combine_kernel(pos_ref, next_pos_ref, ys_hbm, wts_ref, x1_ref, mod_ref, fg_ref, o_ref, buf, sem,
                    *, final):
    i = pl.program_id(0)
    last = pl.num_programs(0) - 1
    slot = i % 2

    def gather(p_ref, sl):
        for r in range(PLAN_BLOCK):
            for s in range(2):
                dst = buf.at[sl, s, TOKEN_TILE_ROWS * r:TOKEN_TILE_ROWS * (r + 1)]
                pltpu.make_async_copy(_token_tile(ys_hbm, p_ref[s, r]), dst, sem.at[sl]).start(
                    priority=s)

    def drain(sl):
        for s in range(2):
            pltpu.make_async_copy(_token_tiles(ys_hbm, 0, PLAN_BLOCK), buf.at[sl, s], sem.at[sl]).wait()

    @pl.when(i == 0)
    def _():
        gather(pos_ref, 0)

    gather(next_pos_ref, 1 - slot)
    drain(slot)
    y0 = _load_token_tiles(buf, PLAN_BLOCK, lead=(slot, 0))
    y1 = _load_token_tiles(buf, PLAN_BLOCK, lead=(slot, 1))
    wts = wts_ref[...]
    moe = wts[:, 0:1] * y0 + wts[:, 1:2] * y1
    x2 = x1_ref[...] + mod_ref[0, 0, 5:6, :] * moe
    if final:
        ms = jnp.mean(x2 * x2, axis=-1, keepdims=True)
        x2 = x2 * lax.rsqrt(ms + EPS) * fg_ref[...]
    o_ref[...] = x2

    @pl.when(i == last)
    def _():
        drain(1 - slot)


def _combine(pos, ys, wts, x1, mod, layer, mod_row, fg, *, final):
    n = x1.shape[0]
    nb = n // PLAN_BLOCK
    row = lambda i: (i, 0)
    return pl.pallas_call(
        functools.partial(_combine_kernel, final=final),
        grid=(nb,),
        in_specs=[
            pl.BlockSpec((8, PLAN_BLOCK), lambda i: (0, i), memory_space=pltpu.SMEM),
            pl.BlockSpec((8, PLAN_BLOCK), lambda i: (0, jnp.minimum(i + 1, nb - 1)),
                         memory_space=pltpu.SMEM),
            pl.BlockSpec(memory_space=pl.ANY),
            pl.BlockSpec((PLAN_BLOCK, LANES), row),
            pl.BlockSpec((PLAN_BLOCK, D_MODEL), row),
            pl.BlockSpec((1, 1, N_MOD, D_MODEL), lambda i: (layer, mod_row(i), 0, 0)),
            pl.BlockSpec((1, D_MODEL), lambda i: (0, 0)),
        ],
        out_specs=pl.BlockSpec((PLAN_BLOCK, D_MODEL), row),
        out_shape=jax.ShapeDtypeStruct((n, D_MODEL), F32),
        scratch_shapes=[pltpu.VMEM((2, 2, PLAN_BLOCK * TOKEN_TILE_ROWS, LANES), F32),
                        pltpu.SemaphoreType.DMA((2,))],
        compiler_params=pltpu.CompilerParams(vmem_limit_bytes=VMEM_LIMIT_BYTES,
                                             dimension_semantics=("arbitrary",)),
        name="moe_combine",
    )(pos, pos, ys, wts, x1, mod, fg)


def _moe(h2t, route, wts, x1, mod, layer, mod_row, w1, w3, w2, fg, tri, *, final):
    n = x1.shape[0]
    n_tiles = 2 * n // EXPERT_TILE + N_EXPERTS
    pos, info = _moe_plan(route, tri, n_tiles=n_tiles)
    expert_of_tile, valid = info[0, :n_tiles], info[1, :n_tiles]
    xs = _dispatch(info[2, :N_EXPERTS], info[3, :N_EXPERTS], valid, pos, h2t, n_tiles=n_tiles)
    ys = _experts(expert_of_tile, valid, xs, w1, w3, w2, layer, n_tiles=n_tiles)
    return _combine(pos, ys, wts, x1, mod, layer, mod_row, fg, final=final)


def _rope_tables(n_tokens, rot_dim):
    pos = np.arange(n_tokens)
    r = (pos // GRID_W).astype(np.float64)[:, None]
    col = (pos % GRID_W).astype(np.float64)[:, None]
    n_freq = rot_dim // 4
    inv_freq = ROPE_THETA ** (-np.arange(n_freq, dtype=np.float64) / n_freq)
    ang = np.concatenate([r * inv_freq, col * inv_freq], axis=-1)
    ang = np.repeat(ang, 2, axis=-1)
    sign = np.tile(np.array([-1.0, 1.0]), rot_dim // 2)
    cos = np.tile(np.cos(ang), (1, LANES // rot_dim))
    sin = np.tile(np.sin(ang) * sign, (1, LANES // rot_dim))
    return jnp.asarray(cos, F32), jnp.asarray(sin, F32)


def _block_diag_ones():
    idx = np.arange(2 * LANES) // 64
    return jnp.asarray((idx[:, None] == idx[None, :]).astype(np.float32), BF16)


def _pair_heads_by_kv_group(w, axis):
    shape = w.shape
    lead, tail = shape[:axis], shape[axis + 1:]
    w = w.reshape(lead + (A_KV_HEADS, A_HEADS // A_KV_HEADS, A_HEAD_DIM) + tail)
    w = jnp.swapaxes(w, axis, axis + 1)
    return w.reshape(shape)


def kernel(x_prompt, x_sample, cache_gqa_k, cache_gqa_v, cache_diff_k, cache_diff_v, c, c_ctx, w_in,
           w_out, norm1_g, norm2_g, ada_w, ada_b, gqa_qnorm_g, gqa_knorm_g, conv_w, diff_lambda,
           diff_subln_g, router_w, router_bias, moe_w1, moe_w3, moe_w2, final_g):
    n_ctx_b, t_ctx, _ = x_prompt.shape
    n_lat_b, t_lat, _ = x_sample.shape
    past = cache_gqa_k.shape[2]
    n_q = A_HEADS * A_HEAD_DIM

    cvec = jnp.concatenate([c_ctx[None, :], c, jnp.zeros((8 - 1 - n_lat_b, D_MODEL), F32)], axis=0)
    mod = _adaln(cvec, ada_w, ada_b).reshape(DEPTH, 8, N_MOD, D_MODEL)

    w_in_b = jnp.concatenate(
        [_pair_heads_by_kv_group(w_in[:, :, :n_q], 2), w_in[:, :, n_q:]], axis=2).astype(BF16)
    w_out_b = jnp.concatenate(
        [_pair_heads_by_kv_group(w_out[:, :n_q, :], 1), w_out[:, n_q:, :]], axis=1).astype(BF16)
    w1_b = moe_w1.reshape(DEPTH * N_EXPERTS, D_MODEL, EXPERT_FF)
    w3_b = moe_w3.reshape(DEPTH * N_EXPERTS, D_MODEL, EXPERT_FF)
    w2_b = moe_w2.reshape(DEPTH * N_EXPERTS, EXPERT_FF, D_MODEL)
    rw_t = router_w.T
    rw_hi = rw_t.astype(BF16)
    rwt = jnp.concatenate([rw_hi, (rw_t - rw_hi.astype(F32)).astype(BF16)], axis=0)
    rb = router_bias.reshape(N_EXPERTS, 1)
    bd = _block_diag_ones()
    tri = jnp.asarray(np.triu(np.ones((PLAN_BLOCK, PLAN_BLOCK), np.float32), k=1), BF16)
    fg = final_g.reshape(1, D_MODEL)
    tables = _rope_tables(t_lat, A_HEAD_DIM) + _rope_tables(t_lat, C_QK_DIM)

    ck_cache = cache_gqa_k.reshape(n_lat_b * DEPTH, past, A_KV_HEADS * A_HEAD_DIM)
    cv_cache = cache_gqa_v.reshape(n_lat_b * DEPTH, past, A_KV_HEADS * A_HEAD_DIM)
    dk_cache = cache_diff_k.reshape(n_lat_b * DEPTH, past, C_HEADS * 2 * C_QK_DIM)
    dv_cache = cache_diff_v.reshape(n_lat_b * DEPTH, past, C_HEADS * C_V_DIM)

    xp = x_prompt.reshape(n_ctx_b * t_ctx, D_MODEL)
    xs = x_sample.reshape(n_lat_b * t_lat, D_MODEL)
    tm = 1024
    tq = 256
    state = tuple(jnp.zeros((n_ctx_b, DEPTH, t_ctx, w), F32)
                  for w in (LANES, LANES, 2 * LANES, 2 * LANES))
    for l in range(DEPTH):
        lam_init = 0.8 - 0.6 * math.exp(-0.3 * l)
        g1 = norm1_g[l].reshape(1, D_MODEL)
        g2 = norm2_g[l].reshape(1, D_MODEL)
        qg = jnp.tile(gqa_qnorm_g[l], 4).reshape(1, 2 * LANES)
        kg = jnp.tile(gqa_knorm_g[l], 4).reshape(1, 2 * LANES)
        sg = jnp.tile(diff_subln_g[l], 2).reshape(1, LANES)
        final = l == DEPTH - 1

        def run_stream(x, n_batch, t_seq, mod_row_of, tabs, caches, nj_gqa, nj_diff, key_group,
                       seqs, state_in=None):
            ctx = caches is None
            outs = _inproj(x, mod, l, lambda i: mod_row_of(i, tm), g1, w_in_b[l], qg, kg, bd, tabs,
                           tm=tm, ctx=ctx, key_group=key_group, t_seq=t_seq, state_in=state_in)
            q, k, vt, gb, gu, cq, ck, cvt = outs[:8]
            c_gk, c_gv, c_dk, c_dv = caches if caches is not None else (None,) * 4
            oa = _gqa_attention(q, k, vt, c_gk, c_gv, l, n_batch=n_batch, t_seq=t_seq, tq=tq,
                                nj=nj_gqa, seqs=seqs)
            oc = _diff_attention(cq, ck, cvt, c_dk, c_dv, l, diff_lambda, sg, n_batch=n_batch,
                                 t_seq=t_seq, tq=tq, nj=nj_diff, seqs=seqs, lam_init=lam_init)
            x1, h2t, route, wts = _outproj(oa, gb, gu, conv_w, oc, x, mod, l,
                                           lambda i: mod_row_of(i, tm), w_out_b[l], g2, rwt, rb,
                                           tm=tm, t_seq=t_seq)
            x2 = _moe(h2t, route, wts, x1, mod, l, lambda i: mod_row_of(i, PLAN_BLOCK), w1_b, w3_b,
                      w2_b, fg, tri, final=final)
            return x2, outs[8:]

        xp, state = run_stream(xp, n_ctx_b, t_ctx, lambda i, t: 0, None, None, 4, 2, t_ctx, 4, state)
        xs, _ = run_stream(xs, n_lat_b, t_lat, lambda i, t: 1 + (i * t) // t_lat, tables,
                           (ck_cache, cv_cache, dk_cache, dv_cache), 2, 1, 256, 1)

    def heads(a, n_heads, dim):
        return a.reshape(n_ctx_b, DEPTH, t_ctx, n_heads, dim)

    return (xp.reshape(n_ctx_b, t_ctx, D_MODEL), xs.reshape(n_lat_b, t_lat, D_MODEL),
            heads(state[0], A_KV_HEADS, A_HEAD_DIM), heads(state[1], A_KV_HEADS, A_HEAD_DIM),
            heads(state[2], C_HEADS, 2 * C_QK_DIM), heads(state[3], C_HEADS, C_V_DIM))
```

```python
import functools
import math

import numpy as np
import jax
import jax.numpy as jnp
from jax import lax
from jax.experimental import pallas as pl
from jax.experimental.pallas import tpu as pltpu

D_MODEL = 1024
DEPTH = 2
GRID_W = 64
A_HEADS = 8
A_KV_HEADS = 2
A_HEAD_DIM = 64
B_WIDTH = 256
C_HEADS = 4
C_QK_DIM = 32
C_V_DIM = 64
N_EXPERTS = 16
EXPERTS_PER_GROUP = 4
N_EXPERT_GROUPS = 4
EXPERT_FF = 512
ROPE_THETA = 10000.0
EPS = 1e-6
N_MOD = 6
COL_KV = A_HEADS * A_HEAD_DIM
COL_GATE_B = COL_KV + 2 * A_KV_HEADS * A_HEAD_DIM
COL_GATE_C = COL_GATE_B + B_WIDTH
COL_U = COL_GATE_C + B_WIDTH
COL_CQ = COL_U + B_WIDTH
COL_CK = COL_CQ + C_HEADS * 2 * C_QK_DIM
COL_CV = COL_CK + C_HEADS * 2 * C_QK_DIM
W_IN_WIDTH = COL_CV + C_HEADS * C_V_DIM
LOG2E = math.log2(math.e)

LANES = 128
HEAD_ROWS = 64
ONES_ROWS = 16
VMEM_LIMIT_BYTES = 56 * 1024 * 1024

F32 = jnp.float32
BF16 = jnp.bfloat16

_dot = functools.partial(jnp.dot, preferred_element_type=F32)


def _dot_nt(a, b):
    return lax.dot_general(a, b, (((1,), (1,)), ((), ())), preferred_element_type=F32)


def _split(a):
    hi = a.astype(BF16)
    lo = (a - hi.astype(F32)).astype(BF16)
    return hi, lo


def _dot3(a, b):
    ah, al = _split(a)
    bh, bl = _split(b)
    return _dot(ah, bh) + _dot(ah, bl) + _dot(al, bh)


def _sigmoid(x):
    return 1.0 / (1.0 + jnp.exp(-x))


def _seg_rms(x, bd, g):
    ss = _dot((x * x).astype(BF16), bd)
    return x * lax.rsqrt(ss * (1.0 / 64.0) + EPS) * g


def _rope(x, cos, sin_signed):
    lane = lax.broadcasted_iota(jnp.int32, x.shape, 1)
    even = (lane & 1) == 0
    partner = jnp.where(even, pltpu.roll(x, LANES - 1, 1), pltpu.roll(x, 1, 1))
    return x * cos + partner * sin_signed


def _params():
    return pltpu.CompilerParams(vmem_limit_bytes=VMEM_LIMIT_BYTES)


def _adaln_kernel(c_ref, w_ref, b_ref, o_ref):
    c = c_ref[...]
    s = c * _sigmoid(c)
    o_ref[0] = _dot3(s, w_ref[0]) + b_ref[0]


def _adaln(cvec, ada_w, ada_b):
    tn = 1536
    nt = (N_MOD * D_MODEL) // tn
    return pl.pallas_call(
        _adaln_kernel,
        grid=(DEPTH, nt),
        in_specs=[
            pl.BlockSpec((8, D_MODEL), lambda l, n: (0, 0)),
            pl.BlockSpec((1, D_MODEL, tn), lambda l, n: (l, 0, n)),
            pl.BlockSpec((1, 1, tn), lambda l, n: (l, 0, n)),
        ],
        out_specs=pl.BlockSpec((1, 8, tn), lambda l, n: (l, 0, n)),
        out_shape=jax.ShapeDtypeStruct((DEPTH, 8, N_MOD * D_MODEL), F32),
        compiler_params=_params(),
        name="adaln",
    )(cvec, ada_w, ada_b.reshape(DEPTH, 1, N_MOD * D_MODEL))


def _inproj_kernel(*refs, rope, ctx, kg, n_state_in):
    it = iter(refs)
    x_ref, mod_ref, g1_ref, w_ref, qg_ref, kg_ref, bd_ref = (next(it) for _ in range(7))
    if rope:
        cos_a, sin_a, cos_c, sin_c = (next(it) for _ in range(4))
    for _ in range(n_state_in):
        next(it)
    q_ref, k_ref, vt_ref, gb_ref, gu_ref, cq_ref, ck_ref, cvt_ref = (next(it) for _ in range(8))
    if ctx:
        k32_ref, v32_ref, ck32_ref, cv32_ref = (next(it) for _ in range(4))

    def store_state(ref, a):
        ref[...] = a.reshape(ref.shape)

    x = x_ref[...]
    ms = jnp.mean(x * x, axis=-1, keepdims=True)
    y = x * lax.rsqrt(ms + EPS) * g1_ref[...]
    h = y * (1.0 + mod_ref[0, 0, 1:2, :]) + mod_ref[0, 0, 0:1, :]
    hb = h.astype(BF16)
    bd = bd_ref[...]

    def seg(lo, width):
        return _dot(hb, w_ref[:, lo:lo + width])

    def store_transposed(dst, a):
        at = a.T.astype(BF16)
        for c in range(a.shape[0] // kg):
            dst[c] = at[:, kg * c:kg * (c + 1)]

    halves = lambda a: (a[:, :LANES], a[:, LANES:])

    for jj in range(2):
        qq = _seg_rms(seg(2 * LANES * jj, 2 * LANES), bd, qg_ref[...])
        for j, q in zip((2 * jj, 2 * jj + 1), halves(qq)):
            if rope:
                q = _rope(q, cos_a[...], sin_a[...])
            q_ref[j] = (q * (A_HEAD_DIM ** -0.5 * LOG2E)).astype(BF16)

    kv = seg(COL_KV, 2 * LANES)
    k, v = halves(_seg_rms(kv, bd, kg_ref[...]))[0], kv[:, LANES:]
    if rope:
        k = _rope(k, cos_a[...], sin_a[...])
    if ctx:
        store_state(k32_ref, k)
        store_state(v32_ref, v)
    k_ref[0] = k.astype(BF16)
    store_transposed(vt_ref.at[0], v)

    gb_ref[...] = seg(COL_GATE_B, B_WIDTH).astype(BF16)
    gu_ref[...] = (seg(COL_GATE_C, B_WIDTH) * seg(COL_U, B_WIDTH)).astype(BF16)

    cq2, ck2, cv2 = (seg(col, 2 * LANES) for col in (COL_CQ, COL_CK, COL_CV))
    if ctx:
        store_state(ck32_ref, ck2)
        store_state(cv32_ref, cv2)
    for j, (cq, ck, cv) in enumerate(zip(halves(cq2), halves(ck2), halves(cv2))):
        if rope:
            cq = _rope(cq, cos_c[...], sin_c[...])
            ck = _rope(ck, cos_c[...], sin_c[...])
        cq_ref[j] = (cq * (C_QK_DIM ** -0.5 * LOG2E)).astype(BF16)
        ck_ref[j] = ck.astype(BF16)
        store_transposed(cvt_ref.at[j], cv)


def _inproj(x, mod, layer, mod_row, g1, w_in, qg, kg, bd, tables, *, tm, ctx, key_group, t_seq,
            state_in):
    n = x.shape[0]
    rope = tables is not None
    row = lambda i: (i, 0)
    const2 = lambda i: (0, 0)
    pair3 = lambda i: (0, i, 0)
    tile4 = lambda i: (0, i, 0, 0)
    nc = tm // key_group
    in_specs = [
        pl.BlockSpec((tm, D_MODEL), row),
        pl.BlockSpec((1, 1, N_MOD, D_MODEL), lambda i: (layer, mod_row(i), 0, 0)),
        pl.BlockSpec((1, D_MODEL), const2),
        pl.BlockSpec((D_MODEL, W_IN_WIDTH), const2),
        pl.BlockSpec((1, 2 * LANES), const2),
        pl.BlockSpec((1, 2 * LANES), const2),
        pl.BlockSpec((2 * LANES, 2 * LANES), const2),
    ]
    args = [x, mod, g1, w_in, qg, kg, bd]
    if rope:
        t = tables[0].shape[0]
        tab = lambda i: (i % (t // tm), 0)
        in_specs += [pl.BlockSpec((tm, LANES), tab)] * 4
        args += list(tables)
    out_specs = [
        pl.BlockSpec((4, tm, LANES), pair3),
        pl.BlockSpec((1, tm, LANES), pair3),
        pl.BlockSpec((1, nc, LANES, key_group), tile4),
        pl.BlockSpec((tm, B_WIDTH), row),
        pl.BlockSpec((tm, B_WIDTH), row),
        pl.BlockSpec((2, tm, LANES), pair3),
        pl.BlockSpec((2, tm, LANES), pair3),
        pl.BlockSpec((2, nc, LANES, key_group), tile4),
    ]
    out_shape = [
        jax.ShapeDtypeStruct((4, n, LANES), BF16),
        jax.ShapeDtypeStruct((1, n, LANES), BF16),
        jax.ShapeDtypeStruct((1, n // key_group, LANES, key_group), BF16),
        jax.ShapeDtypeStruct((n, B_WIDTH), BF16),
        jax.ShapeDtypeStruct((n, B_WIDTH), BF16),
        jax.ShapeDtypeStruct((2, n, LANES), BF16),
        jax.ShapeDtypeStruct((2, n, LANES), BF16),
        jax.ShapeDtypeStruct((2, n // key_group, LANES, key_group), BF16),
    ]
    aliases = {}
    n_state_in = 0
    if ctx:
        seqs = tm // t_seq
        for width in (LANES, LANES, 2 * LANES, 2 * LANES):
            out_specs.append(pl.BlockSpec((seqs, 1, t_seq, width), lambda i: (i, layer, 0, 0)))
            out_shape.append(jax.ShapeDtypeStruct((n // t_seq, DEPTH, t_seq, width), F32))
        if state_in is not None:
            n_state_in = len(state_in)
            for s, a in enumerate(state_in):
                aliases[len(args)] = len(out_shape) - n_state_in + s
                in_specs.append(pl.BlockSpec(memory_space=pl.ANY))
                args.append(a)
    return pl.pallas_call(
        functools.partial(_inproj_kernel, rope=rope, ctx=ctx, kg=key_group, n_state_in=n_state_in),
        grid=(n // tm,),
        in_specs=in_specs,
        out_specs=out_specs,
        out_shape=out_shape,
        input_output_aliases=aliases,
        compiler_params=_params(),
        name="inproj_ctx" if ctx else "inproj_lat",
    )(*args)


def _attend(units, k_ref, vt_ref, caches, n_sub):
    tq = units[0][0].shape[0]
    nu = len(units)
    kg = vt_ref.shape[-1]
    n_groups = k_ref.shape[1] // n_sub // kg
    blocks = sorted({kb for _, _, kb in units})
    n_stages = n_groups + (0 if caches is None else 1)

    def load(stage):
        if caches is not None:
            if stage == 0:
                return {kb: (caches[kb[0]][0].astype(BF16), caches[kb[0]][1].T.astype(BF16))
                        for kb in blocks}
            stage -= 1
        out = {}
        for j, sub in blocks:
            g = sub * n_groups + stage
            out[(j, sub)] = (k_ref[j, kg * g:kg * (g + 1), :], vt_ref[j, g])
        return out

    def value_rows(vt, half):
        own = vt[HEAD_ROWS * half:HEAD_ROWS * (half + 1), :]
        return jnp.concatenate([own, jnp.ones((ONES_ROWS, vt.shape[1]), BF16)], axis=0)

    def score(kv):
        return [_dot_nt(kv[kb][0], q) for q, _, kb in units]

    ms = [jnp.full((1, tq), -jnp.inf, F32)] * nu
    accs = [jnp.zeros((HEAD_ROWS + ONES_ROWS, tq), F32)] * nu
    kv = load(0)
    scores = score(kv)
    for s in range(n_stages):
        cur_scores, cur_kv = scores, kv
        if s + 1 < n_stages:
            kv = load(s + 1)
            scores = score(kv)
        new_ms, ps = [], []
        for u in range(nu):
            half = cur_scores[u].shape[0] // 2
            s1, s2 = cur_scores[u][:half], cur_scores[u][half:]
            m1 = jnp.maximum(ms[u], jnp.max(s1, axis=0, keepdims=True))
            p1 = jnp.exp2(s1 - m1).astype(BF16)
            m = jnp.maximum(m1, jnp.max(s2, axis=0, keepdims=True))
            p2 = jnp.exp2(s2 - m).astype(BF16)
            new_ms.append(m)
            ps.append(jnp.concatenate([p1 * jnp.exp2(m1 - m).astype(BF16), p2], axis=0))
        pvs = [_dot(value_rows(cur_kv[kb][1], half), ps[u]) for u, (_, half, kb) in enumerate(units)]
        accs = [jnp.exp2(ms[u] - new_ms[u]) * accs[u] + pvs[u] for u in range(nu)]
        ms = new_ms
    return [acc[0:HEAD_ROWS] / acc[HEAD_ROWS:HEAD_ROWS + 1] for acc in accs]


def _lane_mask(q, lo, width):
    lane = lax.broadcasted_iota(jnp.int32, q.shape, 1)
    return jnp.where((lane >= lo) & (lane < lo + width), q, jnp.zeros_like(q))


def _query_rows(seqs, tq):
    return [slice(tq * s, tq * (s + 1)) for s in range(seqs)]


def _gqa_kernel(*refs, has_cache, nj, seqs, tq):
    if has_cache:
        q_ref, k_ref, vt_ref, ck_ref, cv_ref, o_ref = refs
        caches = {0: (ck_ref[0], cv_ref[0])}
    else:
        q_ref, k_ref, vt_ref, o_ref = refs
        caches = None
    rows = _query_rows(seqs, tq)
    units = [(_lane_mask(q_ref[j, rows[s], :], 64 * u, 64), u, (0, s))
             for s in range(seqs) for j in range(nj) for u in range(2)]
    outs = iter(_attend(units, k_ref, vt_ref, caches, seqs))
    for s in range(seqs):
        for j in range(nj):
            ot = jnp.concatenate([next(outs), next(outs)], axis=0)
            o_ref[rows[s], LANES * j:LANES * (j + 1)] = ot.T.astype(BF16)


def _gqa_attention(q, k, vt, cache_k, cache_v, layer, *, n_batch, t_seq, tq, nj, seqs):
    n = q.shape[1]
    nq = t_seq // tq
    kg = vt.shape[-1]
    has_cache = cache_k is not None
    assert seqs == 1 or (nq == 1 and not has_cache)
    in_specs = [
        pl.BlockSpec((nj, seqs * tq, LANES), lambda b, i, j: (j, b * nq + i, 0)),
        pl.BlockSpec((1, seqs * t_seq, LANES), lambda b, i, j: (0, b, 0)),
        pl.BlockSpec((1, seqs * t_seq // kg, LANES, kg), lambda b, i, j: (0, b, 0, 0)),
    ]
    args = [q, k, vt]
    if has_cache:
        past = cache_k.shape[1]
        cspec = pl.BlockSpec((1, past, LANES), lambda b, i, j: (b * DEPTH + layer, 0, 0))
        in_specs += [cspec, cspec]
        args += [cache_k, cache_v]
    return pl.pallas_call(
        functools.partial(_gqa_kernel, has_cache=has_cache, nj=nj, seqs=seqs, tq=tq),
        grid=(n_batch // seqs, nq, 4 // nj),
        in_specs=in_specs,
        out_specs=pl.BlockSpec((seqs * tq, LANES * nj), lambda b, i, j: (b * nq + i, j)),
        out_shape=jax.ShapeDtypeStruct((n, 4 * LANES), BF16),
        compiler_params=_params(),
        name="gqa_lat" if has_cache else "gqa_ctx",
    )(*args)


def _diff_kernel(*refs, has_cache, nj, seqs, tq, lam_init):
    if has_cache:
        q_ref, k_ref, vt_ref, ck_ref, cv_ref, lam_ref, g_ref, o_ref = refs
    else:
        q_ref, k_ref, vt_ref, lam_ref, g_ref, o_ref = refs
    lv = lam_ref[0]
    lam = (jnp.exp(jnp.sum(lv[0:1] * lv[1:2], axis=-1, keepdims=True))
           - jnp.exp(jnp.sum(lv[2:3] * lv[3:4], axis=-1, keepdims=True)) + lam_init)
    rows = _query_rows(seqs, tq)
    units = [(_lane_mask(q_ref[j, rows[s], :], 32 * c, 32), c // 2, (j, s))
             for s in range(seqs) for j in range(nj) for c in range(4)]
    caches = None
    if has_cache:
        caches = {j: (ck_ref[0, :, LANES * j:LANES * (j + 1)], cv_ref[0, :, LANES * j:LANES * (j + 1)])
                  for j in range(nj)}
    outs = iter(_attend(units, k_ref, vt_ref, caches, seqs))
    for s in range(seqs):
        for j in range(nj):
            res = [next(outs) for _ in range(4)]
            heads = []
            for i in range(2):
                o = res[2 * i] - lam * res[2 * i + 1]
                ms = jnp.mean(o * o, axis=0, keepdims=True)
                heads.append(o * lax.rsqrt(ms + EPS))
            o = jnp.concatenate(heads, axis=0).T * g_ref[...] * (1.0 - lam_init)
            o_ref[rows[s], LANES * j:LANES * (j + 1)] = o.astype(BF16)


def _diff_attention(cq, ck, cvt, cache_k, cache_v, layer, diff_lambda, subln_g, *,
                    n_batch, t_seq, tq, nj, seqs, lam_init):
    n = cq.shape[1]
    nq = t_seq // tq
    kg = cvt.shape[-1]
    has_cache = cache_k is not None
    assert seqs == 1 or (nq == 1 and not has_cache)
    in_specs = [
        pl.BlockSpec((nj, seqs * tq, LANES), lambda b, i, j: (j, b * nq + i, 0)),
        pl.BlockSpec((nj, seqs * t_seq, LANES), lambda b, i, j: (j, b, 0)),
        pl.BlockSpec((nj, seqs * t_seq // kg, LANES, kg), lambda b, i, j: (j, b, 0, 0)),
    ]
    args = [cq, ck, cvt]
    if has_cache:
        past = cache_k.shape[1]
        cspec = pl.BlockSpec((1, past, LANES * nj), lambda b, i, j: (b * DEPTH + layer, 0, j))
        in_specs += [cspec, cspec]
        args += [cache_k, cache_v]
    in_specs += [
        pl.BlockSpec((1, 4, C_QK_DIM), lambda b, i, j: (layer, 0, 0)),
        pl.BlockSpec((1, LANES), lambda b, i, j: (0, 0)),
    ]
    args += [diff_lambda, subln_g]
    return pl.pallas_call(
        functools.partial(_diff_kernel, has_cache=has_cache, nj=nj, seqs=seqs, tq=tq,
                          lam_init=lam_init),
        grid=(n_batch // seqs, nq, 2 // nj),
        in_specs=in_specs,
        out_specs=pl.BlockSpec((seqs * tq, LANES * nj), lambda b, i, j: (b * nq + i, j)),
        out_shape=jax.ShapeDtypeStruct((n, 2 * LANES), BF16),
        compiler_params=_params(),
        name="diff_lat" if has_cache else "diff_ctx",
    )(*args)


HALO = 16


def _route(logits_t, bias_col):
    scores = _sigmoid(logits_t)
    sel = scores + bias_col
    row = lambda a, e: a[e:e + 1, :]
    grp = []
    for g in range(N_EXPERT_GROUPS):
        s0, s1, s2, s3 = (row(sel, 4 * g + k) for k in range(4))
        hi1, lo1 = jnp.maximum(s0, s1), jnp.minimum(s0, s1)
        hi2, lo2 = jnp.maximum(s2, s3), jnp.minimum(s2, s3)
        grp.append(jnp.maximum(hi1, hi2) + jnp.maximum(jnp.minimum(hi1, hi2), jnp.maximum(lo1, lo2)))
    best = grp[0]
    tg = jnp.zeros_like(best, dtype=jnp.int32)
    for g in range(1, N_EXPERT_GROUPS):
        better = grp[g] > best
        best = jnp.where(better, grp[g], best)
        tg = jnp.where(better, g, tg)

    def in_group(a, k):
        out = row(a, k)
        for g in range(1, N_EXPERT_GROUPS):
            out = jnp.where(tg == g, row(a, 4 * g + k), out)
        return out

    vals = [in_group(sel, k) for k in range(EXPERTS_PER_GROUP)]
    scs = [in_group(scores, k) for k in range(EXPERTS_PER_GROUP)]

    def first_argmax(vs):
        bv, bi = vs[0], jnp.zeros_like(tg)
        for k in range(1, len(vs)):
            better = vs[k] > bv
            bv = jnp.where(better, vs[k], bv)
            bi = jnp.where(better, k, bi)
        return bi

    i1 = first_argmax(vals)
    i2 = first_argmax([jnp.where(i1 == k, -jnp.inf, vals[k]) for k in range(EXPERTS_PER_GROUP)])

    def pick(vs, idx):
        out = vs[0]
        for k in range(1, len(vs)):
            out = jnp.where(idx == k, vs[k], out)
        return out

    w1, w2 = pick(scs, i1), pick(scs, i2)
    tot = w1 + w2
    return 4 * tg + i1, 4 * tg + i2, w1 / tot, w2 / tot


def _outproj_kernel(oa_ref, gb_ref, gu_ref, gup_ref, gun_ref, cw_ref, oc_ref, x_ref, mod_ref,
                    w_ref, g2_ref, rwt_ref, rb_ref,
                    x1_ref, h2t_ref, route_ref, wts_ref, *, tm, t_seq):
    i = pl.program_id(0)
    gu = gu_ref[...].astype(F32)
    rowi = lax.broadcasted_iota(jnp.int32, gu.shape, 0)
    pos = (i * tm + rowi) % t_seq
    prev = pltpu.roll(gu, 1, 0)
    prev = jnp.where(rowi == 0, gup_ref[HALO - 1:HALO, :].astype(F32), prev)
    prev = jnp.where(pos == 0, 0.0, prev)
    nxt = pltpu.roll(gu, tm - 1, 0)
    nxt = jnp.where(rowi == tm - 1, gun_ref[0:1, :].astype(F32), nxt)
    nxt = jnp.where(pos == t_seq - 1, 0.0, nxt)
    cw = cw_ref[0]
    ob = gb_ref[...].astype(F32) * (prev * cw[0:1] + gu * cw[1:2] + nxt * cw[2:3])

    row_b = A_HEADS * A_HEAD_DIM
    row_c = row_b + B_WIDTH
    mix = (_dot(oa_ref[...], w_ref[0:row_b, :])
           + _dot(ob.astype(BF16), w_ref[row_b:row_c, :])
           + _dot(oc_ref[...], w_ref[row_c:row_c + C_HEADS * C_V_DIM, :]))
    x1 = x_ref[...] + mod_ref[0, 0, 2:3, :] * mix
    x1_ref[...] = x1
    ms = jnp.mean(x1 * x1, axis=-1, keepdims=True)
    h2 = x1 * lax.rsqrt(ms + EPS) * g2_ref[...]
    h2 = h2 * (1.0 + mod_ref[0, 0, 4:5, :]) + mod_ref[0, 0, 3:4, :]
    _store_token_tiles(h2t_ref, h2)

    hh, hl = _split(h2)
    rwt = rwt_ref[...]
    both = _dot_nt(rwt, hh)
    logits_t = both[0:N_EXPERTS] + both[N_EXPERTS:2 * N_EXPERTS] + _dot_nt(rwt[0:N_EXPERTS], hl)
    e0, e1, w0, w1 = _route(logits_t, rb_ref[...])
    zeros = jnp.zeros_like(w0)
    route_ref[...] = jnp.concatenate(
        [e0.astype(F32), e1.astype(F32), w0, w1, zeros, zeros, zeros, zeros], axis=0)
    srow = lax.broadcasted_iota(jnp.int32, (LANES, tm), 0)
    wts_t = jnp.where(srow == 0, w0, 0.0) + jnp.where(srow == 1, w1, 0.0)
    wts_ref[...] = wts_t.T


def _outproj(oa, gb, gu, conv_w, oc, x, mod, layer, mod_row, w_out, g2, rwt, rb, *, tm, t_seq):
    n = x.shape[0]
    row = lambda i: (i, 0)
    const2 = lambda i: (0, 0)
    hb = tm // HALO
    last = n // HALO - 1
    return pl.pallas_call(
        functools.partial(_outproj_kernel, tm=tm, t_seq=t_seq),
        grid=(n // tm,),
        in_specs=[
            pl.BlockSpec((tm, 4 * LANES), row),
            pl.BlockSpec((tm, B_WIDTH), row),
            pl.BlockSpec((tm, B_WIDTH), row),
            pl.BlockSpec((HALO, B_WIDTH), lambda i: (jnp.maximum(i * hb - 1, 0), 0)),
            pl.BlockSpec((HALO, B_WIDTH), lambda i: (jnp.minimum((i + 1) * hb, last), 0)),
            pl.BlockSpec((1, 3, B_WIDTH), lambda i: (layer, 0, 0)),
            pl.BlockSpec((tm, 2 * LANES), row),
            pl.BlockSpec((tm, D_MODEL), row),
            pl.BlockSpec((1, 1, N_MOD, D_MODEL), lambda i: (layer, mod_row(i), 0, 0)),
            pl.BlockSpec((D_MODEL, D_MODEL), const2),
            pl.BlockSpec((1, D_MODEL), const2),
            pl.BlockSpec((2 * N_EXPERTS, D_MODEL), const2),
            pl.BlockSpec((N_EXPERTS, 1), const2),
        ],
        out_specs=[
            pl.BlockSpec((tm, D_MODEL), row),
            pl.BlockSpec((tm * TOKEN_TILE_ROWS, LANES), row),
            pl.BlockSpec((8, tm), lambda i: (0, i)),
            pl.BlockSpec((tm, LANES), row),
        ],
        out_shape=[
            jax.ShapeDtypeStruct((n, D_MODEL), F32),
            jax.ShapeDtypeStruct((n * TOKEN_TILE_ROWS, LANES), F32),
            jax.ShapeDtypeStruct((8, n), F32),
            jax.ShapeDtypeStruct((n, LANES), F32),
        ],
        compiler_params=_params(),
        name="outproj",
    )(oa, gb, gu, gu, gu, conv_w, oc, x, mod, w_out, g2, rwt, rb)


TOKEN_TILE_ROWS = D_MODEL // LANES
EXPERT_TILE = 512
ZERO_CHUNK = 128
PLAN_BLOCK = 512


def _load_token_tiles(ref, n_tokens, lead=()):
    return jnp.concatenate(
        [ref[lead + (pl.ds(c, n_tokens, stride=TOKEN_TILE_ROWS), slice(None))]
         for c in range(TOKEN_TILE_ROWS)], axis=1)


def _store_token_tiles(ref, x):
    n_tokens = x.shape[0]
    for c in range(TOKEN_TILE_ROWS):
        ref[pl.ds(c, n_tokens, stride=TOKEN_TILE_ROWS), :] = x[:, LANES * c:LANES * (c + 1)]


def _moe_plan_kernel(route_ref, tri_ref, pos_ref, info_ref, *, n):
    eid = lax.broadcasted_iota(jnp.int32, (N_EXPERTS, PLAN_BLOCK), 0)

    def block(b):
        sl = slice(PLAN_BLOCK * b, PLAN_BLOCK * (b + 1))
        e0 = route_ref[0:1, sl].astype(jnp.int32)
        e1 = route_ref[1:2, sl].astype(jnp.int32)
        return e0, e1, jnp.where((eid == e0) | (eid == e1), 1.0, 0.0)

    counts = jnp.zeros((N_EXPERTS, 1), F32)
    for b in range(n // PLAN_BLOCK):
        counts = counts + jnp.sum(block(b)[2], axis=1, keepdims=True)
    ntile = jnp.floor((counts + (EXPERT_TILE - 1)) * (1.0 / EXPERT_TILE))
    sub = lax.broadcasted_iota(jnp.int32, (N_EXPERTS, LANES), 0)
    lane = lax.broadcasted_iota(jnp.int32, (N_EXPERTS, LANES), 1)
    ntile_row = jnp.sum(jnp.where(sub == lane, ntile, 0.0), axis=0, keepdims=True)
    first_tile = jnp.sum(jnp.where(lane < sub, ntile_row, 0.0), axis=1, keepdims=True)
    first_row = first_tile * EXPERT_TILE

    carry = jnp.zeros((N_EXPERTS, 1), F32)
    for b in range(n // PLAN_BLOCK):
        e0, e1, mb = block(b)
        rank = _dot(mb.astype(BF16), tri_ref[...]) + carry
        carry = carry + jnp.sum(mb, axis=1, keepdims=True)
        slot = first_row + rank
        p0 = jnp.sum(jnp.where(eid == e0, slot, 0.0), axis=0, keepdims=True)
        p1 = jnp.sum(jnp.where(eid == e1, slot, 0.0), axis=0, keepdims=True)
        pad = jnp.zeros((6, PLAN_BLOCK), jnp.int32)
        pos_ref[:, PLAN_BLOCK * b:PLAN_BLOCK * (b + 1)] = jnp.concatenate(
            [p0.astype(jnp.int32), p1.astype(jnp.int32), pad], axis=0)

    tile = lax.broadcasted_iota(jnp.int32, (1, LANES), 1).astype(F32)
    total = jnp.sum(ntile, axis=0, keepdims=True)
    expert_of_tile = jnp.sum(jnp.where(first_tile <= tile, 1.0, 0.0), axis=0, keepdims=True) - 1.0
    valid = jnp.where(tile < total, 1.0, 0.0)
    to_row = lambda col: jnp.sum(jnp.where(sub == lane, col, 0.0), axis=0, keepdims=True)
    fill_from = jnp.floor((first_row + counts) * (1.0 / ZERO_CHUNK))
    fill_to = (first_row + ntile * EXPERT_TILE) * (1.0 / ZERO_CHUNK)
    zrow = jnp.zeros((4, LANES), jnp.int32)
    info_ref[...] = jnp.concatenate(
        [expert_of_tile.astype(jnp.int32), valid.astype(jnp.int32),
         to_row(fill_from).astype(jnp.int32), to_row(fill_to).astype(jnp.int32), zrow], axis=0)


def _moe_plan(route, tri):
    n = route.shape[1]
    return pl.pallas_call(
        functools.partial(_moe_plan_kernel, n=n),
        grid=(1,),
        in_specs=[pl.BlockSpec((8, n), lambda i: (0, 0)),
                  pl.BlockSpec((PLAN_BLOCK, PLAN_BLOCK), lambda i: (0, 0))],
        out_specs=[pl.BlockSpec((8, n), lambda i: (0, 0)),
                   pl.BlockSpec((8, LANES), lambda i: (0, 0))],
        out_shape=[jax.ShapeDtypeStruct((8, n), jnp.int32),
                   jax.ShapeDtypeStruct((8, LANES), jnp.int32)],
        compiler_params=_params(),
        name="moe_plan",
    )(route, tri)


def _token_tile(ref, t):
    return ref.at[pl.ds(pl.multiple_of(t * TOKEN_TILE_ROWS, TOKEN_TILE_ROWS), TOKEN_TILE_ROWS)]


def _token_tiles(ref, first, count):
    rows = count * TOKEN_TILE_ROWS
    return ref.at[pl.ds(pl.multiple_of(first * TOKEN_TILE_ROWS, TOKEN_TILE_ROWS), rows)]


def _dispatch_kernel(fill_from_ref, fill_to_ref, valid_ref, pos_ref, h_ref, xs_hbm, zbuf, zsem, sem,
                     *, n_tiles):
    i = pl.program_id(0)

    @pl.when(i == 0)
    def _():
        zbuf[...] = jnp.zeros_like(zbuf)
        per_tile = EXPERT_TILE // ZERO_CHUNK

        def fill_copy(chunk):
            return pltpu.make_async_copy(zbuf, _token_tiles(xs_hbm, chunk * ZERO_CHUNK, ZERO_CHUNK), zsem)

        def each_chunk(do):
            for e in range(N_EXPERTS):
                for c in range(per_tile):
                    chunk = fill_from_ref[e] + c

                    @pl.when(chunk < fill_to_ref[e])
                    def _():
                        do(fill_copy(chunk))

            def unused_tile(j, carry):
                @pl.when(valid_ref[j] == 0)
                def _():
                    for c in range(per_tile):
                        do(fill_copy(j * per_tile + c))
                return carry

            lax.fori_loop(0, n_tiles, unused_tile, 0)

        each_chunk(lambda copy: copy.start())
        each_chunk(lambda copy: copy.wait())

    for r in range(PLAN_BLOCK):
        src = h_ref.at[TOKEN_TILE_ROWS * r:TOKEN_TILE_ROWS * (r + 1)]
        for s in range(2):
            pltpu.make_async_copy(src, _token_tile(xs_hbm, pos_ref[s, r]), sem).start(priority=s)
    for _ in range(2):
        pltpu.make_async_copy(h_ref, _token_tiles(xs_hbm, 0, PLAN_BLOCK), sem).wait()


def _dispatch(fill_from, fill_to, valid, pos, h2t, *, n_tiles):
    n = h2t.shape[0] // TOKEN_TILE_ROWS
    return pl.pallas_call(
        functools.partial(_dispatch_kernel, n_tiles=n_tiles),
        grid_spec=pltpu.PrefetchScalarGridSpec(
            num_scalar_prefetch=3,
            grid=(n // PLAN_BLOCK,),
            in_specs=[pl.BlockSpec((8, PLAN_BLOCK), lambda i, a, b, v: (0, i),
                                   memory_space=pltpu.SMEM),
                      pl.BlockSpec((PLAN_BLOCK * TOKEN_TILE_ROWS, LANES), lambda i, a, b, v: (i, 0))],
            out_specs=pl.BlockSpec(memory_space=pl.ANY),
            scratch_shapes=[pltpu.VMEM((ZERO_CHUNK * TOKEN_TILE_ROWS, LANES), F32),
                            pltpu.SemaphoreType.DMA(()), pltpu.SemaphoreType.DMA(())]),
        out_shape=jax.ShapeDtypeStruct((n_tiles * EXPERT_TILE * TOKEN_TILE_ROWS, LANES), F32),
        compiler_params=_params(),
        name="moe_dispatch",
    )(fill_from, fill_to, valid, pos, h2t)


def _experts_kernel(expert_ref, valid_ref, x_ref, w1_ref, w3_ref, w2_ref, o_ref):
    j = pl.program_id(0)

    @pl.when(valid_ref[j] == 1)
    def _():
        x = _load_token_tiles(x_ref, EXPERT_TILE).astype(BF16)
        a = _dot(x, w1_ref[0].astype(BF16))
        b = _dot(x, w3_ref[0].astype(BF16))
        hid = (a * _sigmoid(a)) * b
        _store_token_tiles(o_ref, _dot(hid.astype(BF16), w2_ref[0].astype(BF16)))

    @pl.when(valid_ref[j] == 0)
    def _():
        o_ref[...] = jnp.zeros_like(o_ref)


def _experts(expert_of_tile, valid, xs, w1, w3, w2, layer, *, n_tiles):
    rows = EXPERT_TILE * TOKEN_TILE_ROWS
    wmap = lambda j, e, v: (layer * N_EXPERTS + e[j], 0, 0)
    return pl.pallas_call(
        _experts_kernel,
        grid_spec=pltpu.PrefetchScalarGridSpec(
            num_scalar_prefetch=2,
            grid=(n_tiles,),
            in_specs=[pl.BlockSpec((rows, LANES), lambda j, e, v: (j * v[j], 0)),
                      pl.BlockSpec((1, D_MODEL, EXPERT_FF), wmap),
                      pl.BlockSpec((1, D_MODEL, EXPERT_FF), wmap),
                      pl.BlockSpec((1, EXPERT_FF, D_MODEL), wmap)],
            out_specs=pl.BlockSpec((rows, LANES), lambda j, e, v: (j, 0))),
        out_shape=jax.ShapeDtypeStruct((n_tiles * rows, LANES), F32),
        compiler_params=_params(),
        name="moe_experts",
    )(expert_of_tile, valid, xs, w1, w3, w2)


def _combine_kernel(pos_ref, next_pos_ref, ys_hbm, wts_ref, x1_ref, mod_ref, fg_ref, o_ref, buf, sem,
                    *, final):
    i = pl.program_id(0)
    last = pl.num_programs(0) - 1
    slot = i % 2

    def gather(p_ref, sl):
        for r in range(PLAN_BLOCK):
            for s in range(2):
                dst = buf.at[sl, s, TOKEN_TILE_ROWS * r:TOKEN_TILE_ROWS * (r + 1)]
                pltpu.make_async_copy(_token_tile(ys_hbm, p_ref[s, r]), dst, sem.at[sl]).start(
                    priority=s)

    def drain(sl):
        for s in range(2):
            pltpu.make_async_copy(_token_tiles(ys_hbm, 0, PLAN_BLOCK), buf.at[sl, s], sem.at[sl]).wait()

    @pl.when(i == 0)
    def _():
        gather(pos_ref, 0)

    gather(next_pos_ref, 1 - slot)
    drain(slot)
    y0 = _load_token_tiles(buf, PLAN_BLOCK, lead=(slot, 0))
    y1 = _load_token_tiles(buf, PLAN_BLOCK, lead=(slot, 1))
    wts = wts_ref[...]
    moe = wts[:, 0:1] * y0 + wts[:, 1:2] * y1
    x2 = x1_ref[...] + mod_ref[0, 0, 5:6, :] * moe
    if final:
        ms = jnp.mean(x2 * x2, axis=-1, keepdims=True)
        x2 = x2 * lax.rsqrt(ms + EPS) * fg_ref[...]
    o_ref[...] = x2

    @pl.when(i == last)
    def _():
        drain(1 - slot)


def _combine(pos, ys, wts, x1, mod, layer, mod_row, fg, *, final):
    n = x1.shape[0]
    nb = n // PLAN_BLOCK
    row = lambda i: (i, 0)
    return pl.pallas_call(
        functools.partial(_combine_kernel, final=final),
        grid=(nb,),
        in_specs=[
            pl.BlockSpec((8, PLAN_BLOCK), lambda i: (0, i), memory_space=pltpu.SMEM),
            pl.BlockSpec((8, PLAN_BLOCK), lambda i: (0, jnp.minimum(i + 1, nb - 1)),
                         memory_space=pltpu.SMEM),
            pl.BlockSpec(memory_space=pl.ANY),
            pl.BlockSpec((PLAN_BLOCK, LANES), row),
            pl.BlockSpec((PLAN_BLOCK, D_MODEL), row),
            pl.BlockSpec((1, 1, N_MOD, D_MODEL), lambda i: (layer, mod_row(i), 0, 0)),
            pl.BlockSpec((1, D_MODEL), lambda i: (0, 0)),
        ],
        out_specs=pl.BlockSpec((PLAN_BLOCK, D_MODEL), row),
        out_shape=jax.ShapeDtypeStruct((n, D_MODEL), F32),
        scratch_shapes=[pltpu.VMEM((2, 2, PLAN_BLOCK * TOKEN_TILE_ROWS, LANES), F32),
                        pltpu.SemaphoreType.DMA((2,))],
        compiler_params=pltpu.CompilerParams(vmem_limit_bytes=VMEM_LIMIT_BYTES,
                                             dimension_semantics=("arbitrary",)),
        name="moe_combine",
    )(pos, pos, ys, wts, x1, mod, fg)


def _moe(h2t, route, wts, x1, mod, layer, mod_row, w1, w3, w2, fg, tri, *, final):
    n = x1.shape[0]
    n_tiles = 2 * n // EXPERT_TILE + N_EXPERTS
    assert n_tiles <= LANES
    pos, info = _moe_plan(route, tri)
    expert_of_tile, valid = info[0, :n_tiles], info[1, :n_tiles]
    xs = _dispatch(info[2, :N_EXPERTS], info[3, :N_EXPERTS], valid, pos, h2t, n_tiles=n_tiles)
    ys = _experts(expert_of_tile, valid, xs, w1, w3, w2, layer, n_tiles=n_tiles)
    return _combine(pos, ys, wts, x1, mod, layer, mod_row, fg, final=final)


def _rope_tables(n_tokens, rot_dim):
    pos = np.arange(n_tokens)
    r = (pos // GRID_W).astype(np.float64)[:, None]
    col = (pos % GRID_W).astype(np.float64)[:, None]
    n_freq = rot_dim // 4
    inv_freq = ROPE_THETA ** (-np.arange(n_freq, dtype=np.float64) / n_freq)
    ang = np.concatenate([r * inv_freq, col * inv_freq], axis=-1)
    ang = np.repeat(ang, 2, axis=-1)
    sign = np.tile(np.array([-1.0, 1.0]), rot_dim // 2)
    cos = np.tile(np.cos(ang), (1, LANES // rot_dim))
    sin = np.tile(np.sin(ang) * sign, (1, LANES // rot_dim))
    return jnp.asarray(cos, F32), jnp.asarray(sin, F32)


def _block_diag_ones():
    idx = np.arange(2 * LANES) // 64
    return jnp.asarray((idx[:, None] == idx[None, :]).astype(np.float32), BF16)


def _pair_heads_by_kv_group(w, axis):
    shape = w.shape
    lead, tail = shape[:axis], shape[axis + 1:]
    w = w.reshape(lead + (A_KV_HEADS, A_HEADS // A_KV_HEADS, A_HEAD_DIM) + tail)
    w = jnp.swapaxes(w, axis, axis + 1)
    return w.reshape(shape)


def kernel(x_prompt, x_sample, cache_gqa_k, cache_gqa_v, cache_diff_k, cache_diff_v, c, c_ctx, w_in,
           w_out, norm1_g, norm2_g, ada_w, ada_b, gqa_qnorm_g, gqa_knorm_g, conv_w, diff_lambda,
           diff_subln_g, router_w, router_bias, moe_w1, moe_w3, moe_w2, final_g):
    n_ctx_b, t_ctx, _ = x_prompt.shape
    n_lat_b, t_lat, _ = x_sample.shape
    past = cache_gqa_k.shape[2]
    n_q = A_HEADS * A_HEAD_DIM

    cvec = jnp.concatenate([c_ctx[None, :], c, jnp.zeros((8 - 1 - n_lat_b, D_MODEL), F32)], axis=0)
    mod = _adaln(cvec, ada_w, ada_b).reshape(DEPTH, 8, N_MOD, D_MODEL)

    w_in_b = jnp.concatenate(
        [_pair_heads_by_kv_group(w_in[:, :, :n_q], 2), w_in[:, :, n_q:]], axis=2).astype(BF16)
    w_out_b = jnp.concatenate(
        [_pair_heads_by_kv_group(w_out[:, :n_q, :], 1), w_out[:, n_q:, :]], axis=1).astype(BF16)
    w1_b = moe_w1.reshape(DEPTH * N_EXPERTS, D_MODEL, EXPERT_FF)
    w3_b = moe_w3.reshape(DEPTH * N_EXPERTS, D_MODEL, EXPERT_FF)
    w2_b = moe_w2.reshape(DEPTH * N_EXPERTS, EXPERT_FF, D_MODEL)
    rw_t = router_w.T
    rw_hi = rw_t.astype(BF16)
    rwt = jnp.concatenate([rw_hi, (rw_t - rw_hi.astype(F32)).astype(BF16)], axis=0)
    rb = router_bias.reshape(N_EXPERTS, 1)
    bd = _block_diag_ones()
    tri = jnp.asarray(np.triu(np.ones((PLAN_BLOCK, PLAN_BLOCK), np.float32), k=1), BF16)
    fg = final_g.reshape(1, D_MODEL)
    tables = _rope_tables(t_lat, A_HEAD_DIM) + _rope_tables(t_lat, C_QK_DIM)

    ck_cache = cache_gqa_k.reshape(n_lat_b * DEPTH, past, A_KV_HEADS * A_HEAD_DIM)
    cv_cache = cache_gqa_v.reshape(n_lat_b * DEPTH, past, A_KV_HEADS * A_HEAD_DIM)
    dk_cache = cache_diff_k.reshape(n_lat_b * DEPTH, past, C_HEADS * 2 * C_QK_DIM)
    dv_cache = cache_diff_v.reshape(n_lat_b * DEPTH, past, C_HEADS * C_V_DIM)

    xp = x_prompt.reshape(n_ctx_b * t_ctx, D_MODEL)
    xs = x_sample.reshape(n_lat_b * t_lat, D_MODEL)
    tm = 1024
    tq = 256
    state = tuple(jnp.zeros((n_ctx_b, DEPTH, t_ctx, w), F32)
                  for w in (LANES, LANES, 2 * LANES, 2 * LANES))
    for l in range(DEPTH):
        lam_init = 0.8 - 0.6 * math.exp(-0.3 * l)
        g1 = norm1_g[l].reshape(1, D_MODEL)
        g2 = norm2_g[l].reshape(1, D_MODEL)
        qg = jnp.tile(gqa_qnorm_g[l], 4).reshape(1, 2 * LANES)
        kg = jnp.tile(gqa_knorm_g[l], 4).reshape(1, 2 * LANES)
        sg = jnp.tile(diff_subln_g[l], 2).reshape(1, LANES)
        final = l == DEPTH - 1

        def run_stream(x, n_batch, t_seq, mod_row_of, tabs, caches, nj_gqa, nj_diff, key_group,
                       seqs, state_in=None):
            ctx = caches is None
            outs = _inproj(x, mod, l, lambda i: mod_row_of(i, tm), g1, w_in_b[l], qg, kg, bd, tabs,
                           tm=tm, ctx=ctx, key_group=key_group, t_seq=t_seq, state_in=state_in)
            q, k, vt, gb, gu, cq, ck, cvt = outs[:8]
            c_gk, c_gv, c_dk, c_dv = caches if caches is not None else (None,) * 4
            oa = _gqa_attention(q, k, vt, c_gk, c_gv, l, n_batch=n_batch, t_seq=t_seq, tq=tq,
                                nj=nj_gqa, seqs=seqs)
            oc = _diff_attention(cq, ck, cvt, c_dk, c_dv, l, diff_lambda, sg, n_batch=n_batch,
                                 t_seq=t_seq, tq=tq, nj=nj_diff, seqs=seqs, lam_init=lam_init)
            x1, h2t, route, wts = _outproj(oa, gb, gu, conv_w, oc, x, mod, l,
                                           lambda i: mod_row_of(i, tm), w_out_b[l], g2, rwt, rb,
                                           tm=tm, t_seq=t_seq)
            x2 = _moe(h2t, route, wts, x1, mod, l, lambda i: mod_row_of(i, PLAN_BLOCK), w1_b, w3_b,
                      w2_b, fg, tri, final=final)
            return x2, outs[8:]

        xp, state = run_stream(xp, n_ctx_b, t_ctx, lambda i, t: 0, None, None, 4, 2, t_ctx, 4, state)
        xs, _ = run_stream(xs, n_lat_b, t_lat, lambda i, t: 1 + (i * t) // t_lat, tables,
                           (ck_cache, cv_cache, dk_cache, dv_cache), 2, 1, 256, 1)

    def heads(a, n_heads, dim):
        return a.reshape(n_ctx_b, DEPTH, t_ctx, n_heads, dim)

    return (xp.reshape(n_ctx_b, t_ctx, D_MODEL), xs.reshape(n_lat_b, t_lat, D_MODEL),
            heads(state[0], A_KV_HEADS, A_HEAD_DIM), heads(state[1], A_KV_HEADS, A_HEAD_DIM),
            heads(state[2], C_HEADS, 2 * C_QK_DIM), heads(state[3], C_HEADS, C_V_DIM))
```
